```python
import math
import numpy as np
import jax, jax.numpy as jnp
from jax import lax

D_MODEL = 1024
BATCH = 2
SEQ = 16384
DEPTH = 2

CTX_LEN = 256
GRID_W = 64
N_EVEN = (DEPTH + 1) // 2
N_ODD = DEPTH // 2
EPS = 1e-6
CHUNK = 128
CONV_W = 4
MIX_WIDTH = D_MODEL

A_HEADS = 4
A_DQK = 64
A_DV = 128
A_WIDTH = A_HEADS * A_DV
B_HEADS = 8
B_HEADDIM = 64
B_WIDTH = B_HEADS * B_HEADDIM
B_GROUPS = 2
B_STATE = 128
B_CONV_CH = B_WIDTH + 2 * B_GROUPS * B_STATE
EV_SPLITS = [A_HEADS * A_DQK, A_HEADS * A_DQK, A_WIDTH, A_WIDTH, 2 * A_HEADS, 2 * A_HEADS,
             B_WIDTH, B_CONV_CH, 2 * B_HEADS]
EV_COLS = sum(EV_SPLITS)
C_HEADS = 8
C_HEADDIM = 64
C_WIDTH = C_HEADS * C_HEADDIM
C_LORA_W = 64
C_LORA_A = 64
C_LORA_G = 128
RWKV_SPLITS = [C_WIDTH, C_WIDTH, C_WIDTH, C_LORA_W, C_LORA_A, C_LORA_G]
RWKV_COLS = sum(RWKV_SPLITS)
RWKV_W_SCALE = math.exp(-0.5)
RWKV_GN_EPS = 64e-5
D_WIDTH = 512
D_BLOCKS = 8
D_BLOCK = D_WIDTH // D_BLOCKS
LRU_C = 8.0
OD_COLS = RWKV_COLS + 2 * D_WIDTH
PEER_HEADS = 8
PEER_DK = 256
N_KEYS = 128
N_EXPERTS = N_KEYS * N_KEYS
PEER_TOPK = 16
PEER_BLOCK = 128

kernel_name = 'hybrid_mlstm_ssd_rwkv7_rglru_peer_prefix_ctx'


def rmsnorm(x, w):
    xf = x.astype(jnp.float32)
    y = xf * lax.rsqrt(jnp.mean(xf * xf, axis=-1, keepdims=True) + EPS)
    return (y * w.astype(jnp.float32)).astype(x.dtype)


def split_cols(p, sizes):
    return jnp.split(p, [int(s) for s in np.cumsum(sizes)[:-1]], axis=-1)


def conv_centred(x, w, b):
    L = x.shape[1]
    left = CONV_W // 2
    xp = jnp.pad(x, ((0, 0), (left, CONV_W - 1 - left), (0, 0)))
    y = b
    for tap in range(CONV_W):
        y = y + xp[:, tap:tap + L] * w[tap]
    return y


def grid_shift(p):
    B_, L, cn = p.shape
    rows = L // GRID_W
    g = p.reshape(B_, rows, GRID_W, cn)
    q = cn // 4
    left = jnp.pad(g[:, :, :-1, :q], ((0, 0), (0, 0), (1, 0), (0, 0)))
    right = jnp.pad(g[:, :, 1:, q:2 * q], ((0, 0), (0, 0), (0, 1), (0, 0)))
    up = jnp.pad(g[:, :-1, :, 2 * q:3 * q], ((0, 0), (1, 0), (0, 0), (0, 0)))
    down = jnp.pad(g[:, 1:, :, 3 * q:], ((0, 0), (0, 1), (0, 0), (0, 0)))
    return jnp.concatenate([left, right, up, down], axis=-1).reshape(B_, L, cn)


def seq_shift(p):
    half = p.shape[-1] // 2
    prev = jnp.pad(p[:, :-1, :half], ((0, 0), (1, 0), (0, 0)))
    nxt = jnp.pad(p[:, 1:, half:], ((0, 0), (0, 1), (0, 0)))
    return jnp.concatenate([prev, nxt], axis=-1)


def block_diag(x, w):
    B_, L, C = x.shape
    return jnp.einsum('blnd,nde->blne', x.reshape(B_, L, D_BLOCKS, D_BLOCK), w).reshape(B_, L, C)


def bidirectional(scan_fn, ctx_f, lat_f, ctx_b, lat_b, state0, axis):
    flip = lambda a: jnp.flip(a, axis)
    yc_f, sc_f = scan_fn(ctx_f, state0)
    yl_f, _ = scan_fn(lat_f, sc_f)
    yc_b, sc_b = scan_fn(tuple(flip(a) for a in ctx_b), state0)
    yl_b, _ = scan_fn(tuple(flip(a) for a in lat_b), sc_b)
    return yc_f + flip(yc_b), yl_f + flip(yl_b)


def mlstm_scan(inputs, state):
    dtype = inputs[2].dtype
    q, k, v, logi, logf = (a.astype(jnp.float32) for a in inputs)
    B_, H, L, _ = q.shape
    nc = L // CHUNK
    chunks = lambda a: jnp.moveaxis(a.reshape(B_, H, nc, CHUNK, *a.shape[3:]), 2, 0)
    tril = jnp.tril(jnp.ones((CHUNK, CHUNK), dtype=bool))

    def step(carry, blk):
        C, n, m = carry
        qc, kc, vc, ic, fc = blk
        b = jnp.cumsum(fc, axis=-1)
        dmat = jnp.where(tril, b[..., :, None] - b[..., None, :] + ic[..., None, :], -jnp.inf)
        inter = b + m[..., None]
        m_t = jnp.maximum(inter, jnp.max(dmat, axis=-1))
        s = jnp.einsum('bhtd,bhsd->bhts', qc, kc) * jnp.exp(dmat - m_t[..., None])
        w_inter = jnp.exp(inter - m_t)
        num = jnp.einsum('bhts,bhsv->bhtv', s, vc) + w_inter[..., None] * jnp.einsum('bhvd,bhtd->bhtv', C, qc)
        den = jnp.sum(s, axis=-1) + w_inter * jnp.einsum('bhd,bhtd->bht', n, qc)
        h = num / jnp.maximum(jnp.abs(den), jnp.exp(-m_t))[..., None]
        b_last = b[..., -1]
        g = b_last[..., None] - b + ic
        m_new = jnp.maximum(b_last + m, jnp.max(g, axis=-1))
        wg = jnp.exp(g - m_new[..., None])
        wc = jnp.exp(b_last + m - m_new)
        C = wc[..., None, None] * C + jnp.einsum('bhs,bhsv,bhsd->bhvd', wg, vc, kc)
        n = wc[..., None] * n + jnp.einsum('bhs,bhsd->bhd', wg, kc)
        return (C, n, m_new), h

    state, hs = lax.scan(step, state, tuple(chunks(a) for a in (q, k, v, logi, logf)))
    h = jnp.moveaxis(hs, 0, 2).reshape(B_, H, L, -1)
    return h.astype(dtype), state


def ssd_scan(inputs, state):
    dtype = inputs[0].dtype
    x, bm, cm, dt, la = (a.astype(jnp.float32) for a in inputs)
    B_, H, L, _ = x.shape
    nc = L // CHUNK
    chunks = lambda a: jnp.moveaxis(a.reshape(B_, H, nc, CHUNK, *a.shape[3:]), 2, 0)
    tril = jnp.tril(jnp.ones((CHUNK, CHUNK), dtype=bool))

    def step(S, blk):
        xc, bc, cc, dtc, lac = blk
        b = jnp.cumsum(lac, axis=-1)
        decay = jnp.exp(jnp.where(tril, b[..., :, None] - b[..., None, :], -jnp.inf))
        s = jnp.einsum('bhtn,bhsn->bhts', cc, bc) * decay * dtc[..., None, :]
        y = jnp.einsum('bhts,bhsp->bhtp', s, xc) + jnp.exp(b)[..., None] * jnp.einsum('bhpn,bhtn->bhtp', S, cc)
        w_end = jnp.exp(b[..., -1:] - b) * dtc
        S = jnp.exp(b[..., -1])[..., None, None] * S + jnp.einsum('bhs,bhsp,bhsn->bhpn', w_end, xc, bc)
        return S, y

    state, ys = lax.scan(step, state, tuple(chunks(a) for a in (x, bm, cm, dt, la)))
    y = jnp.moveaxis(ys, 0, 2).reshape(B_, H, L, -1)
    return y.astype(dtype), state


def rwkv_scan(inputs, state):
    dtype = inputs[3].dtype
    seq = tuple(jnp.moveaxis(a.astype(jnp.float32), 1, 0) for a in inputs)

    def step(S, blk):
        rt, wt, kt, vt, at, bt = blk
        sa = jnp.einsum('bhvk,bhk->bhv', S, at)
        S = S * wt[:, :, None, :] + sa[..., None] * bt[:, :, None, :] + vt[..., None] * kt[:, :, None, :]
        return S, jnp.einsum('bhvk,bhk->bhv', S, rt)

    state, ys = lax.scan(step, state, seq)
    return jnp.moveaxis(ys, 0, 1).astype(dtype), state


def lru_scan(inputs, h0):
    dtype = inputs[1].dtype
    a, b = (t.astype(jnp.float32) for t in inputs)
    b = b.at[:, 0].add(a[:, 0] * h0)

    def combine(left, right):
        a1, b1 = left
        a2, b2 = right
        return a1 * a2, a2 * b1 + b2

    _, h = lax.associative_scan(combine, (a, b), axis=1)
    return h.astype(dtype), h[:, -1]


def mlstm_ssd_mixer(h_ctx, h_lat, w_in, i_bias, f_bias, mlstm_norm, conv_w, conv_b,
                    dt_bias, a_log, d_skip, ssd_norm, need_ctx):
    def prep(h):
        B_, L = h.shape[:2]
        q, k, v, o, ig, fg, z, xbc, dt = split_cols(h @ w_in, EV_SPLITS)
        heads = lambda a, n: jnp.swapaxes(a.reshape(B_, L, n, -1), 1, 2)
        qh = heads(q, A_HEADS) * (A_DQK ** -0.5)
        kh, vh = heads(k, A_HEADS), heads(v, A_HEADS)
        gate = lambda g, bias, d: jnp.swapaxes(
            (g[..., d * A_HEADS:(d + 1) * A_HEADS] + bias[d]).astype(jnp.float32), 1, 2)
        mlstm_in = [(qh, kh, vh, gate(ig, i_bias, d), jax.nn.log_sigmoid(gate(fg, f_bias, d)))
                    for d in range(2)]
        xbc = jax.nn.silu(conv_centred(xbc, conv_w, conv_b))
        xs, bs, cs = split_cols(xbc, [B_WIDTH, B_GROUPS * B_STATE, B_GROUPS * B_STATE])
        xh = heads(xs, B_HEADS)
        bh = jnp.repeat(heads(bs, B_GROUPS), B_HEADS // B_GROUPS, axis=1)
        ch = jnp.repeat(heads(cs, B_GROUPS), B_HEADS // B_GROUPS, axis=1)
        ssd_in = []
        for d in range(2):
            dt_d = jax.nn.softplus(jnp.swapaxes(
                (dt[..., d * B_HEADS:(d + 1) * B_HEADS] + dt_bias[d]).astype(jnp.float32), 1, 2))
            la = -dt_d * jnp.exp(a_log[d].astype(jnp.float32))[None, :, None]
            ssd_in.append((xh, bh, ch, dt_d, la))
        return mlstm_in, ssd_in, (o, z, xh)

    mc, sc, extra_c = prep(h_ctx)
    ml, sl, extra_l = prep(h_lat)
    B_ = h_lat.shape[0]
    m0 = (jnp.zeros((B_, A_HEADS, A_DV, A_DQK), jnp.float32),
          jnp.zeros((B_, A_HEADS, A_DQK), jnp.float32),
          jnp.zeros((B_, A_HEADS), jnp.float32))
    hc, hl = bidirectional(mlstm_scan, mc[0], ml[0], mc[1], ml[1], m0, axis=2)
    s0 = jnp.zeros((B_, B_HEADS, B_HEADDIM, B_STATE), jnp.float32)
    yc, yl = bidirectional(ssd_scan, sc[0], sl[0], sc[1], sl[1], s0, axis=2)

    def finish(h, y, extra):
        o, z, xh = extra
        B_, _, L, _ = h.shape
        h = rmsnorm(jnp.swapaxes(h, 1, 2), mlstm_norm.reshape(A_HEADS, A_DV)).reshape(B_, L, A_WIDTH)
        ya = jax.nn.sigmoid(o) * h
        y = jnp.swapaxes(y + d_skip[None, :, None, None] * xh, 1, 2).reshape(B_, L, B_WIDTH)
        yb = rmsnorm(y * jax.nn.silu(z), ssd_norm)
        return jnp.concatenate([ya, yb], axis=-1)

    f_ctx = finish(hc, yc, extra_c) if need_ctx else None
    return f_ctx, finish(hl, yl, extra_l)


def rwkv_lru_mixer(h_ctx, h_lat, w_in, mu, w0, w_up, a0, a_up, g_up, k_k, k_a, r_k, ln_w, ln_b,
                   conv_w, conv_b, lam, wa, ba, wi, bi, need_ctx):
    def prep(h, shift_fn):
        B_, L = h.shape[:2]
        pr, gate_br, x_br = split_cols(h @ w_in, [RWKV_COLS, D_WIDTH, D_WIDTH])
        pr = pr + mu * (shift_fn(pr) - pr)
        r, k, v, wd, ad, gd = split_cols(pr, RWKV_SPLITS)
        heads = lambda a: a.reshape(B_, L, C_HEADS, C_HEADDIM)
        kk = heads(k * k_k).astype(jnp.float32)
        kk = kk * lax.rsqrt(jnp.sum(kk * kk, axis=-1, keepdims=True) + 1e-12)
        rwkv_in, kts = [], []
        for d in range(2):
            logw = -RWKV_W_SCALE * jax.nn.sigmoid((w0[d] + jnp.tanh(wd) @ w_up[d]).astype(jnp.float32))
            a = jax.nn.sigmoid((a0[d] + ad @ a_up[d]).astype(jnp.float32))
            kt = heads(k * (1.0 + (a - 1.0) * k_a))
            rwkv_in.append((heads(r), heads(jnp.exp(logw)), kt, heads(v), -kk, kk * heads(a)))
            kts.append(kt)
        g = jax.nn.sigmoid(gd) @ g_up
        xc = conv_centred(x_br, conv_w, conv_b)
        lru_in = []
        for d in range(2):
            gr = jax.nn.sigmoid((block_diag(xc, wa[d]) + ba[d]).astype(jnp.float32))
            gi = jax.nn.sigmoid(block_diag(xc, wi[d]) + bi[d])
            log_a = -LRU_C * gr * jax.nn.softplus(-lam[d].astype(jnp.float32))
            lru_in.append((jnp.exp(log_a), jnp.sqrt(-jnp.expm1(2.0 * log_a)) * (gi * xc)))
        return rwkv_in, lru_in, (heads(r), 0.5 * (kts[0] + kts[1]), heads(v), g, gate_br)

    rc, lc, extra_c = prep(h_ctx, seq_shift)
    rl, ll, extra_l = prep(h_lat, grid_shift)
    B_ = h_lat.shape[0]
    S0 = jnp.zeros((B_, C_HEADS, C_HEADDIM, C_HEADDIM), jnp.float32)
    yc, yl = bidirectional(rwkv_scan, rc[0], rl[0], rc[1], rl[1], S0, axis=1)
    h0 = jnp.zeros((B_, D_WIDTH), jnp.float32)
    uc, ul = bidirectional(lru_scan, lc[0], ll[0], lc[1], ll[1], h0, axis=1)

    def finish(y, u, extra):
        r, kb, v, g, gate_br = extra
        B_, L = y.shape[:2]
        yf = y.astype(jnp.float32)
        mean = jnp.mean(yf, axis=-1, keepdims=True)
        var = jnp.mean(jnp.square(yf - mean), axis=-1, keepdims=True)
        yn = (yf - mean) * lax.rsqrt(var + RWKV_GN_EPS) * ln_w.reshape(C_HEADS, C_HEADDIM) \
            + ln_b.reshape(C_HEADS, C_HEADDIM)
        yn = yn + jnp.sum(r * kb * r_k, axis=-1, keepdims=True) * v
        yc_ = (yn.reshape(B_, L, C_WIDTH) * g).astype(y.dtype)
        yd = u * jax.nn.gelu(gate_br, approximate=False)
        return jnp.concatenate([yc_, yd], axis=-1)

    f_ctx = finish(yc, uc, extra_c) if need_ctx else None
    return f_ctx, finish(yl, ul, extra_l)


def peer(h, wq, keys, u_tab, v_tab):
    B_, L, D = h.shape
    blocks = h.reshape(-1, PEER_BLOCK, D)

    def block_fn(xb):
        q = (xb @ wq).reshape(PEER_BLOCK, PEER_HEADS, 2, PEER_DK // 2)
        s = jnp.einsum('thpd,hpkd->thpk', q, keys)
        sv, si = lax.top_k(s, PEER_TOPK)
        cand = (sv[:, :, 0, :, None] + sv[:, :, 1, None, :]).reshape(PEER_BLOCK, PEER_HEADS, -1)
        cidx = (si[:, :, 0, :, None] * N_KEYS + si[:, :, 1, None, :]).reshape(PEER_BLOCK, PEER_HEADS, -1)
        best, pos = lax.top_k(cand, PEER_TOPK)
        idx = jnp.take_along_axis(cidx, pos, axis=-1)
        gate = jax.nn.softmax(best.astype(jnp.float32), axis=-1).astype(xb.dtype)
        act = jax.nn.gelu(jnp.einsum('td,thkd->thk', xb, u_tab[idx]), approximate=False)
        return jnp.einsum('thk,thkd->td', gate * act, v_tab[idx])

    return lax.map(block_fn, blocks).reshape(B_, L, D)


def setup_inputs(seed: int = 0) -> dict:
    key = jax.random.key(seed)
    ks = iter(jax.random.split(key, 64))
    f32 = jnp.float32
    nrm = lambda shape, scale: scale * jax.random.normal(next(ks), shape, f32)
    uni = lambda shape, lo, hi: jax.random.uniform(next(ks), shape, f32, lo, hi)
    D = D_MODEL
    dt = jnp.exp(uni((N_EVEN, 2, B_HEADS), math.log(1e-3), math.log(1e-1)))
    a8 = uni((N_ODD, 2, D_WIDTH), 0.9, 0.999)
    sa = a8 ** (1.0 / 8.0)
    return {
        'x': nrm((BATCH, SEQ, D), 1.0),
        'c': nrm((BATCH, D), 1.0),
        'ctx': nrm((BATCH, CTX_LEN, D), 1.0),
        'c_ctx': nrm((D,), 1.0),
        'mod_w': nrm((DEPTH, D, 6 * D), 0.5 * D ** -0.5),
        'mod_b': nrm((DEPTH, 6 * D), 0.02),
        'norm1': 1.0 + nrm((DEPTH, D), 0.02),
        'norm2': 1.0 + nrm((DEPTH, D), 0.02),
        'peer_wq': nrm((DEPTH, D, PEER_HEADS * PEER_DK), D ** -0.5),
        'peer_keys': nrm((DEPTH, PEER_HEADS, 2, N_KEYS, PEER_DK // 2), (PEER_DK // 2) ** -0.5),
        'peer_u': nrm((DEPTH, N_EXPERTS, D), D ** -0.5),
        'peer_v': nrm((DEPTH, N_EXPERTS, D), 0.5),
        'ev_w_in': nrm((N_EVEN, D, EV_COLS), D ** -0.5),
        'ev_w_out': nrm((N_EVEN, MIX_WIDTH, D), MIX_WIDTH ** -0.5),
        'ev_mlstm_i_bias': nrm((N_EVEN, 2, A_HEADS), 0.1),
        'ev_mlstm_f_bias': uni((N_EVEN, 2, A_HEADS), 3.0, 6.0),
        'ev_mlstm_norm': 1.0 + nrm((N_EVEN, A_WIDTH), 0.02),
        'ev_ssd_conv_w': nrm((N_EVEN, CONV_W, B_CONV_CH), CONV_W ** -0.5),
        'ev_ssd_conv_b': nrm((N_EVEN, B_CONV_CH), 0.02),
        'ev_ssd_dt_bias': dt + jnp.log(-jnp.expm1(-dt)),
        'ev_ssd_a_log': jnp.log(uni((N_EVEN, 2, B_HEADS), 1.0, 16.0)),
        'ev_ssd_d': 1.0 + nrm((N_EVEN, B_HEADS), 0.1),
        'ev_ssd_norm': 1.0 + nrm((N_EVEN, B_WIDTH), 0.02),
        'od_w_in': nrm((N_ODD, D, OD_COLS), D ** -0.5),
        'od_w_out': nrm((N_ODD, MIX_WIDTH, D), MIX_WIDTH ** -0.5),
        'od_rwkv_mu': uni((N_ODD, RWKV_COLS), 0.0, 1.0),
        'od_rwkv_w0': uni((N_ODD, 2, C_WIDTH), -3.0, 3.0),
        'od_rwkv_w_up': nrm((N_ODD, 2, C_LORA_W, C_WIDTH), 0.1),
        'od_rwkv_a0': nrm((N_ODD, 2, C_WIDTH), 0.1),
        'od_rwkv_a_up': nrm((N_ODD, 2, C_LORA_A, C_WIDTH), C_LORA_A ** -0.5),
        'od_rwkv_g_up': nrm((N_ODD, C_LORA_G, C_WIDTH), C_LORA_G ** -0.5),
        'od_rwkv_k_k': 0.85 + nrm((N_ODD, C_WIDTH), 0.05),
        'od_rwkv_k_a': 1.0 + nrm((N_ODD, C_WIDTH), 0.05),
        'od_rwkv_r_k': nrm((N_ODD, C_HEADS, C_HEADDIM), 0.1),
        'od_rwkv_ln_w': 1.0 + nrm((N_ODD, C_WIDTH), 0.02),
        'od_rwkv_ln_b': nrm((N_ODD, C_WIDTH), 0.02),
        'od_lru_conv_w': nrm((N_ODD, CONV_W, D_WIDTH), CONV_W ** -0.5),
        'od_lru_conv_b': nrm((N_ODD, D_WIDTH), 0.02),
        'od_lru_lambda': jnp.log(sa) - jnp.log1p(-sa),
        'od_lru_wa': nrm((N_ODD, 2, D_BLOCKS, D_BLOCK, D_BLOCK), D_BLOCK ** -0.5),
        'od_lru_ba': nrm((N_ODD, 2, D_WIDTH), 0.02),
        'od_lru_wi': nrm((N_ODD, 2, D_BLOCKS, D_BLOCK, D_BLOCK), D_BLOCK ** -0.5),
        'od_lru_bi': nrm((N_ODD, 2, D_WIDTH), 0.02),
        'final_norm': 1.0 + nrm((D,), 0.02),
    }


def reference(x, c, ctx, c_ctx, mod_w, mod_b, norm1, norm2, peer_wq, peer_keys, peer_u, peer_v,
              ev_w_in, ev_w_out, ev_mlstm_i_bias, ev_mlstm_f_bias, ev_mlstm_norm, ev_ssd_conv_w,
              ev_ssd_conv_b, ev_ssd_dt_bias, ev_ssd_a_log, ev_ssd_d, ev_ssd_norm,
              od_w_in, od_w_out, od_rwkv_mu, od_rwkv_w0, od_rwkv_w_up, od_rwkv_a0, od_rwkv_a_up,
              od_rwkv_g_up, od_rwkv_k_k, od_rwkv_k_a, od_rwkv_r_k, od_rwkv_ln_w, od_rwkv_ln_b,
              od_lru_conv_w, od_lru_conv_b, od_lru_lambda, od_lru_wa, od_lru_ba, od_lru_wi, od_lru_bi,
              final_norm):
    x_lat, x_ctx = x, ctx
    s_lat = jax.nn.silu(c)
    s_ctx = jax.nn.silu(c_ctx)
    for i in range(DEPTH):
        last = i == DEPTH - 1
        sh1, sc1, g1, sh2, sc2, g2 = [m[:, None, :] for m in jnp.split(s_lat @ mod_w[i] + mod_b[i], 6, axis=-1)]
        csh1, csc1, cg1, csh2, csc2, cg2 = jnp.split(s_ctx @ mod_w[i] + mod_b[i], 6, axis=-1)
        h_lat = rmsnorm(x_lat, norm1[i]) * (1.0 + sc1) + sh1
        h_ctx = rmsnorm(x_ctx, norm1[i]) * (1.0 + csc1) + csh1
        j = i // 2
        if i % 2 == 0:
            f_ctx, f_lat = mlstm_ssd_mixer(
                h_ctx, h_lat, ev_w_in[j], ev_mlstm_i_bias[j], ev_mlstm_f_bias[j], ev_mlstm_norm[j],
                ev_ssd_conv_w[j], ev_ssd_conv_b[j], ev_ssd_dt_bias[j], ev_ssd_a_log[j], ev_ssd_d[j],
                ev_ssd_norm[j], not last)
            w_out = ev_w_out[j]
        else:
            f_ctx, f_lat = rwkv_lru_mixer(
                h_ctx, h_lat, od_w_in[j], od_rwkv_mu[j], od_rwkv_w0[j], od_rwkv_w_up[j], od_rwkv_a0[j],
                od_rwkv_a_up[j], od_rwkv_g_up[j], od_rwkv_k_k[j], od_rwkv_k_a[j], od_rwkv_r_k[j],
                od_rwkv_ln_w[j], od_rwkv_ln_b[j], od_lru_conv_w[j], od_lru_conv_b[j], od_lru_lambda[j],
                od_lru_wa[j], od_lru_ba[j], od_lru_wi[j], od_lru_bi[j], not last)
            w_out = od_w_out[j]
        x_lat = x_lat + g1 * (f_lat @ w_out)
        x_lat = x_lat + g2 * peer(rmsnorm(x_lat, norm2[i]) * (1.0 + sc2) + sh2,
                                  peer_wq[i], peer_keys[i], peer_u[i], peer_v[i])
        if not last:
            x_ctx = x_ctx + cg1 * (f_ctx @ w_out)
            x_ctx = x_ctx + cg2 * peer(rmsnorm(x_ctx, norm2[i]) * (1.0 + csc2) + csh2,
                                      peer_wq[i], peer_keys[i], peer_u[i], peer_v[i])
    return rmsnorm(x_lat, final_norm)
```

```python
import functools
import math

import jax
import jax.numpy as jnp
from jax import lax
from jax.experimental import pallas as pl
from jax.experimental.pallas import tpu as pltpu

F32 = jnp.float32
BF16 = jnp.bfloat16
HIGHEST = lax.Precision.HIGHEST

D_MODEL = 1024
EPS = 1e-6
CHUNK = 128
ROWS = 256
GRID_W = 64
LANES = 128
SUBLANES = 8
MIB = 1024 * 1024

A_HEADS, A_DQK, A_DV = 4, 64, 128
A_WIDTH = A_HEADS * A_DV
B_HEADS, B_HEADDIM, B_GROUPS, B_STATE = 8, 64, 2, 128
B_WIDTH = B_HEADS * B_HEADDIM
C_HEADS, C_HEADDIM = 8, 64
C_WIDTH = C_HEADS * C_HEADDIM
C_LORA_W, C_LORA_A, C_LORA_G = 64, 64, 128
RWKV_COLS = 3 * C_WIDTH + C_LORA_W + C_LORA_A + C_LORA_G
RWKV_W_SCALE = math.exp(-0.5)
RWKV_GN_EPS = 64e-5
D_WIDTH = 512
LRU_C = 8.0
PEER_HEADS, PEER_DK, N_KEYS, PEER_TOPK = 8, 256, 128, 16
PEER_SLOTS = PEER_HEADS * PEER_TOPK
PEER_TB = 128
PEER_NBUF = 4
ROW_TILES = D_MODEL // LANES


def _cparams(sem, vmem_mib=48):
    return pltpu.CompilerParams(dimension_semantics=sem, vmem_limit_bytes=vmem_mib * MIB)


def _softplus(x):
    return jnp.maximum(x, 0.0) + jnp.log1p(jnp.exp(-jnp.abs(x)))


def _sigmoid(x):
    return 1.0 / (1.0 + jnp.exp(-x))


def _silu(x):
    return x * _sigmoid(x)


def _gelu(x):
    return 0.5 * x * (1.0 + lax.erf(x * (1.0 / math.sqrt(2.0))))


def _norm_mod(x, nw, mod_ref, slot):
    ms = jnp.mean(x * x, axis=-1, keepdims=True)
    y = x * lax.rsqrt(ms + EPS) * nw
    sh = mod_ref[0, :, slot * D_MODEL:(slot + 1) * D_MODEL]
    sc = mod_ref[0, :, (slot + 1) * D_MODEL:(slot + 2) * D_MODEL]
    return y * (1.0 + sc) + sh


def _chunk_order(c, nc_ctx, nc, reverse):
    if not reverse:
        return c
    return jnp.where(c < nc_ctx, nc_ctx - 1 - c, nc + nc_ctx - 1 - c)


def _mod_kernel(s_ref, w_ref, b_ref, o_ref):
    s = _silu(s_ref[...])
    o_ref[...] = jnp.dot(s, w_ref[...], precision=HIGHEST, preferred_element_type=F32) + b_ref[...]


def _modulation(srows, w, b):
    d = srows.shape[1]
    nt = w.shape[1] // d
    return pl.pallas_call(
        _mod_kernel,
        grid=(nt,),
        in_specs=[pl.BlockSpec((SUBLANES, d), lambda j: (0, 0)),
                  pl.BlockSpec((d, d), lambda j: (0, j)),
                  pl.BlockSpec((1, d), lambda j: (0, j))],
        out_specs=pl.BlockSpec((SUBLANES, d), lambda j: (0, j)),
        out_shape=jax.ShapeDtypeStruct((SUBLANES, w.shape[1]), F32),
        compiler_params=_cparams(("arbitrary",)),
        name="modulation",
    )(srows, w, b)


def _proj_kernel(x_ref, nw_ref, mod_ref, w_ref, *o_refs, widths):
    h = _norm_mod(x_ref[0], nw_ref[...], mod_ref, 0)
    out = jnp.dot(h.astype(BF16), w_ref[...], preferred_element_type=F32)
    off = 0
    for o_ref, wd in zip(o_refs, widths):
        o_ref[0] = out[:, off:off + wd]
        off += wd


def _mod_spec(ctx_tiles):
    return pl.BlockSpec((1, 1, 6 * D_MODEL),
                        lambda b, i: (2 * b + jnp.where(i >= ctx_tiles, 1, 0), 0, 0))


def _project(xs, nw, modsel, w, widths, ctx_tiles):
    bsz, n, d = xs.shape
    nt = n // ROWS
    return pl.pallas_call(
        functools.partial(_proj_kernel, widths=widths),
        grid=(bsz, nt),
        in_specs=[pl.BlockSpec((1, ROWS, d), lambda b, i: (b, i, 0)),
                  pl.BlockSpec((1, d), lambda b, i: (0, 0)),
                  _mod_spec(ctx_tiles),
                  pl.BlockSpec(w.shape, lambda b, i: (0, 0))],
        out_specs=[pl.BlockSpec((1, ROWS, wd), lambda b, i: (b, i, 0)) for wd in widths],
        out_shape=[jax.ShapeDtypeStruct((bsz, n, wd), F32) for wd in widths],
        compiler_params=_cparams(("parallel", "arbitrary")),
        name="norm_mod_project",
    )(xs, nw, modsel, w)


def _conv_kernel(x_ref, p_ref, n_ref, w_ref, b_ref, o_ref, *, ctx_tiles, ntiles, act):
    i = pl.program_id(1)
    x = x_ref[0]
    rows = x.shape[0]
    prev_ok = jnp.logical_and(i != 0, i != ctx_tiles)
    next_ok = jnp.logical_and(i != ctx_tiles - 1, i != ntiles - 1)
    p = jnp.where(prev_ok, p_ref[0], 0.0)
    nx = jnp.where(next_ok, n_ref[0], 0.0)
    row = lax.broadcasted_iota(jnp.int32, (rows, 1), 0)
    xm1 = jnp.where(row == 0, p[7:8], pltpu.roll(x, 1, 0))
    xm2 = jnp.where(row == 0, p[6:7], jnp.where(row == 1, p[7:8], pltpu.roll(x, 2, 0)))
    xp1 = jnp.where(row == rows - 1, nx[0:1], pltpu.roll(x, rows - 1, 0))
    w = w_ref[...]
    y = b_ref[...] + xm2 * w[0:1] + xm1 * w[1:2] + x * w[2:3] + xp1 * w[3:4]
    o_ref[0] = _silu(y) if act else y


def _conv4(x, width, w, b, ctx_tiles, act):
    bsz, n, _ = x.shape
    nt = n // ROWS
    hb = ROWS // SUBLANES
    nh = n // SUBLANES
    return pl.pallas_call(
        functools.partial(_conv_kernel, ctx_tiles=ctx_tiles, ntiles=nt, act=act),
        grid=(bsz, nt),
        in_specs=[pl.BlockSpec((1, ROWS, width), lambda b_, i: (b_, i, 0)),
                  pl.BlockSpec((1, SUBLANES, width), lambda b_, i: (b_, jnp.maximum(i * hb - 1, 0), 0)),
                  pl.BlockSpec((1, SUBLANES, width),
                               lambda b_, i: (b_, jnp.minimum((i + 1) * hb, nh - 1), 0)),
                  pl.BlockSpec((4, width), lambda b_, i: (0, 0)),
                  pl.BlockSpec((1, width), lambda b_, i: (0, 0))],
        out_specs=pl.BlockSpec((1, ROWS, width), lambda b_, i: (b_, i, 0)),
        out_shape=jax.ShapeDtypeStruct((bsz, n, width), F32),
        compiler_params=_cparams(("parallel", "arbitrary")),
        name="conv4",
    )(x, x, x, w, b)


def _tri(reverse):
    t = lax.broadcasted_iota(jnp.int32, (CHUNK, CHUNK), 0)
    s = lax.broadcasted_iota(jnp.int32, (CHUNK, CHUNK), 1)
    return (s >= t) if reverse else (s <= t)


def _cumsums(col, row, mask):
    mf = mask.astype(F32)
    b_col = jnp.dot(mf, col, precision=HIGHEST, preferred_element_type=F32)
    b_row = lax.dot_general(row, mf, (((1,), (1,)), ((), ())), precision=HIGHEST,
                            preferred_element_type=F32)
    return b_col, b_row


def _mlstm_kernel(qkv_ref, gc_ref, gr_ref, bc_ref, br_ref, o_ref, ct_ref, n_ref, m_ref, *,
                  direction):
    c = pl.program_id(1)

    @pl.when(c == 0)
    def _():
        ct_ref[...] = jnp.zeros_like(ct_ref)
        n_ref[...] = jnp.zeros_like(n_ref)
        m_ref[...] = jnp.zeros_like(m_ref)

    reverse = direction == 1
    mask = _tri(reverse)
    gc = gc_ref[0] + bc_ref[...]
    gr = gr_ref[0] + br_ref[...]
    d4 = direction * A_HEADS
    li_col = gc[:, d4:d4 + A_HEADS]
    lf_col = -_softplus(-gc[:, 8 + d4:8 + d4 + A_HEADS])
    li_row = gr[d4:d4 + A_HEADS, :]
    lf_row = -_softplus(-gr[8 + d4:8 + d4 + A_HEADS, :])
    b_col, b_row = _cumsums(lf_col, lf_row, mask)
    tot = jnp.sum(lf_col, axis=0, keepdims=True)
    outs = []
    for h in range(A_HEADS):
        q = qkv_ref[0, :, h * A_DQK:(h + 1) * A_DQK] * (A_DQK ** -0.5)
        k = qkv_ref[0, :, A_HEADS * A_DQK + h * A_DQK:A_HEADS * A_DQK + (h + 1) * A_DQK]
        v = qkv_ref[0, :, 2 * A_HEADS * A_DQK + h * A_DV:2 * A_HEADS * A_DQK + (h + 1) * A_DV]
        bc, br = b_col[:, h:h + 1], b_row[h:h + 1, :]
        ic, ir = li_col[:, h:h + 1], li_row[h:h + 1, :]
        th = tot[:, h:h + 1]
        m_prev = m_ref[h]
        dmat = jnp.where(mask, bc - br + ir, -jnp.inf)
        inter = bc + m_prev
        m_t = jnp.maximum(inter, jnp.max(dmat, axis=1, keepdims=True))
        qk = lax.dot_general(q, k, (((1,), (1,)), ((), ())), preferred_element_type=F32)
        s = qk * jnp.exp(dmat - m_t)
        w_inter = jnp.exp(inter - m_t)
        num = jnp.dot(s, v, preferred_element_type=F32) \
            + w_inter * jnp.dot(q, ct_ref[h], preferred_element_type=F32)
        den = jnp.sum(s, axis=1, keepdims=True) \
            + w_inter * jnp.sum(q * n_ref[h], axis=1, keepdims=True)
        outs.append(num / jnp.maximum(jnp.abs(den), jnp.exp(-m_t)))
        g = th - bc + ic
        m_new = jnp.maximum(th + m_prev, jnp.max(g, axis=0, keepdims=True))
        wg = jnp.exp(g - m_new)
        wc = jnp.exp(th + m_prev - m_new)
        ct_ref[h] = wc * ct_ref[h] + lax.dot_general(k * wg, v, (((0,), (0,)), ((), ())),
                                                     preferred_element_type=F32)
        n_ref[h] = wc * n_ref[h] + jnp.sum(wg * k, axis=0, keepdims=True)
        m_ref[h] = m_new
    o_ref[0] = jnp.concatenate(outs, axis=1)


def _mlstm_scan(qkvo, gates, gates_row, bias_col, bias_row, nc_ctx, direction):
    bsz, n, _ = qkvo.shape
    nc = n // CHUNK
    reverse = direction == 1
    order = lambda c: _chunk_order(c, nc_ctx, nc, reverse)
    return pl.pallas_call(
        functools.partial(_mlstm_kernel, direction=direction),
        grid=(bsz, nc),
        in_specs=[pl.BlockSpec((1, CHUNK, 2 * A_HEADS * A_DQK + A_WIDTH), lambda b, c: (b, order(c), 0)),
                  pl.BlockSpec((1, CHUNK, LANES), lambda b, c: (b, order(c), 0)),
                  pl.BlockSpec((1, 32, CHUNK), lambda b, c: (b, 0, order(c))),
                  pl.BlockSpec((1, LANES), lambda b, c: (0, 0)),
                  pl.BlockSpec((32, 1), lambda b, c: (0, 0))],
        out_specs=pl.BlockSpec((1, CHUNK, A_WIDTH), lambda b, c: (b, order(c), 0)),
        out_shape=jax.ShapeDtypeStruct((bsz, n, A_WIDTH), F32),
        scratch_shapes=[pltpu.VMEM((A_HEADS, A_DQK, A_DV), F32),
                        pltpu.VMEM((A_HEADS, 1, A_DQK), F32),
                        pltpu.VMEM((A_HEADS, 1, 1), F32)],
        compiler_params=_cparams(("parallel", "arbitrary")),
        name="mlstm_scan",
    )(qkvo, gates, gates_row, bias_col, bias_row)


def _ssd_kernel(xbc_ref, gc_ref, gr_ref, bc_ref, br_ref, ac_ref, ar_ref, o_ref, st_ref, *,
                direction):
    c = pl.program_id(1)

    @pl.when(c == 0)
    def _():
        st_ref[...] = jnp.zeros_like(st_ref)

    reverse = direction == 1
    mask = _tri(reverse)
    gc = gc_ref[0] + bc_ref[...]
    gr = gr_ref[0] + br_ref[...]
    d8 = 16 + direction * B_HEADS
    dt_col = _softplus(gc[:, d8:d8 + B_HEADS])
    dt_row = _softplus(gr[d8:d8 + B_HEADS, :])
    la_col = -dt_col * jnp.exp(ac_ref[:, d8:d8 + B_HEADS])
    la_row = -dt_row * jnp.exp(ar_ref[d8:d8 + B_HEADS, :])
    b_col, b_row = _cumsums(la_col, la_row, mask)
    tot = jnp.sum(la_col, axis=0, keepdims=True)
    outs = []
    hpg = B_HEADS // B_GROUPS
    for g in range(B_GROUPS):
        bm = xbc_ref[0, :, B_WIDTH + g * B_STATE:B_WIDTH + (g + 1) * B_STATE]
        cm = xbc_ref[0, :, B_WIDTH + (B_GROUPS + g) * B_STATE:B_WIDTH + (B_GROUPS + g + 1) * B_STATE]
        cb = lax.dot_general(cm, bm, (((1,), (1,)), ((), ())), preferred_element_type=F32)
        for h in range(g * hpg, (g + 1) * hpg):
            xh = xbc_ref[0, :, h * B_HEADDIM:(h + 1) * B_HEADDIM]
            bc, br = b_col[:, h:h + 1], b_row[h:h + 1, :]
            th = tot[:, h:h + 1]
            decay = jnp.exp(jnp.where(mask, bc - br, -jnp.inf))
            s = cb * decay * dt_row[h:h + 1, :]
            y = jnp.dot(s, xh, preferred_element_type=F32) \
                + jnp.exp(bc) * jnp.dot(cm, st_ref[h], preferred_element_type=F32)
            outs.append(y)
            w_end = jnp.exp(th - bc) * dt_col[:, h:h + 1]
            st_ref[h] = jnp.exp(th) * st_ref[h] + lax.dot_general(
                bm * w_end, xh, (((0,), (0,)), ((), ())), preferred_element_type=F32)
    o_ref[0] = jnp.concatenate(outs, axis=1)


def _ssd_scan(xbc, gates, gates_row, bias_col, bias_row, alog_col, alog_row, nc_ctx, direction):
    bsz, n, width = xbc.shape
    nc = n // CHUNK
    reverse = direction == 1
    order = lambda c: _chunk_order(c, nc_ctx, nc, reverse)
    return pl.pallas_call(
        functools.partial(_ssd_kernel, direction=direction),
        grid=(bsz, nc),
        in_specs=[pl.BlockSpec((1, CHUNK, width), lambda b, c: (b, order(c), 0)),
                  pl.BlockSpec((1, CHUNK, LANES), lambda b, c: (b, order(c), 0)),
                  pl.BlockSpec((1, 32, CHUNK), lambda b, c: (b, 0, order(c))),
                  pl.BlockSpec((1, LANES), lambda b, c: (0, 0)),
                  pl.BlockSpec((32, 1), lambda b, c: (0, 0)),
                  pl.BlockSpec((1, LANES), lambda b, c: (0, 0)),
                  pl.BlockSpec((32, 1), lambda b, c: (0, 0))],
        out_specs=pl.BlockSpec((1, CHUNK, B_WIDTH), lambda b, c: (b, order(c), 0)),
        out_shape=jax.ShapeDtypeStruct((bsz, n, B_WIDTH), F32),
        scratch_shapes=[pltpu.VMEM((B_HEADS, B_STATE, B_HEADDIM), F32)],
        compiler_params=_cparams(("parallel", "arbitrary")),
        name="ssd_scan",
    )(xbc, gates, gates_row, bias_col, bias_row, alog_col, alog_row)


def _finish0_kernel(hf_ref, hb_ref, yf_ref, yb_ref, o_ref_in, z_ref, xs_ref, x_ref, mod_ref,
                    mn_ref, sn_ref, dsk_ref, w_ref, out_ref):
    h = hf_ref[0] + hb_ref[0]
    parts = []
    for hd in range(A_HEADS):
        hh = h[:, hd * A_DV:(hd + 1) * A_DV]
        ms = jnp.mean(hh * hh, axis=-1, keepdims=True)
        parts.append(hh * lax.rsqrt(ms + EPS))
    hn = jnp.concatenate(parts, axis=1) * mn_ref[...]
    ya = _sigmoid(o_ref_in[0]) * hn
    y = yf_ref[0] + yb_ref[0] + dsk_ref[...] * xs_ref[0]
    t = y * _silu(z_ref[0])
    ms = jnp.mean(t * t, axis=-1, keepdims=True)
    yb = t * lax.rsqrt(ms + EPS) * sn_ref[...]
    f = jnp.concatenate([ya, yb], axis=1)
    g1 = mod_ref[0, :, 2 * D_MODEL:3 * D_MODEL]
    out_ref[0] = x_ref[0] + g1 * jnp.dot(f.astype(BF16), w_ref[...], preferred_element_type=F32)


def _finish0(hf, hb, yf, yb, qkvo, zx, xbc_act, xs, modsel, mnorm, snorm, dskip, w_out, ctx_tiles):
    bsz, n, d = xs.shape
    nt = n // ROWS
    half = lambda j: pl.BlockSpec((1, ROWS, 512), lambda b, i, j=j: (b, i, j))
    vec = pl.BlockSpec((1, 512), lambda b, i: (0, 0))
    return pl.pallas_call(
        _finish0_kernel,
        grid=(bsz, nt),
        in_specs=[half(0), half(0), half(0), half(0), half(2), half(2), half(0),
                  pl.BlockSpec((1, ROWS, d), lambda b, i: (b, i, 0)),
                  _mod_spec(ctx_tiles), vec, vec, vec,
                  pl.BlockSpec((d, d), lambda b, i: (0, 0))],
        out_specs=pl.BlockSpec((1, ROWS, d), lambda b, i: (b, i, 0)),
        out_shape=jax.ShapeDtypeStruct((bsz, n, d), F32),
        compiler_params=_cparams(("parallel", "arbitrary")),
        name="finish_mlstm_ssd",
    )(hf, hb, yf, yb, qkvo, zx, xbc_act, xs, modsel, mnorm, snorm, dskip, w_out)


def _head_ones(n, hd):
    r = lax.broadcasted_iota(jnp.int32, (n, n), 0) // hd
    c = lax.broadcasted_iota(jnp.int32, (n, n), 1) // hd
    return (r == c).astype(F32)


def _rwkv_prep_kernel(pr_ref, p_ref, n_ref, mu_ref, w0_ref, wup_ref, a0_ref, aup_ref, gup_ref,
                      kk_ref, ka_ref, r_o, v_o, kk_o, g_o, w0_o, w1_o, kt0_o, kt1_o, al0_o, al1_o,
                      *, ctx_tiles, ntiles):
    i = pl.program_id(1)
    pr = pr_ref[0]
    rows, cols = pr.shape
    q = cols // 4
    row = lax.broadcasted_iota(jnp.int32, (rows, 1), 0)
    col = lax.broadcasted_iota(jnp.int32, (1, cols), 1)
    is_ctx = i < ctx_tiles
    l1 = pltpu.roll(pr, 1, 0)
    r1 = pltpu.roll(pr, rows - 1, 0)
    left = jnp.where(row % GRID_W == 0, 0.0, l1)
    right = jnp.where(row % GRID_W == GRID_W - 1, 0.0, r1)
    up = jnp.concatenate([p_ref[0], pr[:rows - GRID_W]], axis=0)
    up = jnp.where(jnp.logical_and(i == ctx_tiles, row < GRID_W), 0.0, up)
    down = jnp.concatenate([pr[GRID_W:], n_ref[0]], axis=0)
    down = jnp.where(jnp.logical_and(i == ntiles - 1, row >= rows - GRID_W), 0.0, down)
    grid_sh = jnp.where(col < q, left, jnp.where(col < 2 * q, right, jnp.where(col < 3 * q, up, down)))
    prev = jnp.where(row == 0, 0.0, l1)
    nxt = jnp.where(row == rows - 1, 0.0, r1)
    seq_sh = jnp.where(col < 2 * q, prev, nxt)
    shifted = jnp.where(is_ctx, seq_sh, grid_sh)
    pr = pr + mu_ref[...] * (shifted - pr)
    r = pr[:, 0:C_WIDTH]
    k = pr[:, C_WIDTH:2 * C_WIDTH]
    v = pr[:, 2 * C_WIDTH:3 * C_WIDTH]
    o = 3 * C_WIDTH
    wd = pr[:, o:o + C_LORA_W]
    ad = pr[:, o + C_LORA_W:o + C_LORA_W + C_LORA_A]
    gd = pr[:, o + C_LORA_W + C_LORA_A:]
    kk = k * kk_ref[...]
    ss = jnp.dot(kk * kk, _head_ones(C_WIDTH, C_HEADDIM), precision=HIGHEST, preferred_element_type=F32)
    kk = kk * lax.rsqrt(ss + 1e-12)
    tw = jnp.tanh(wd)
    for d, (w_o, kt_o, al_o) in enumerate(((w0_o, kt0_o, al0_o), (w1_o, kt1_o, al1_o))):
        logw = -RWKV_W_SCALE * _sigmoid(w0_ref[d:d + 1, :] + jnp.dot(tw, wup_ref[d], preferred_element_type=F32))
        a = _sigmoid(a0_ref[d:d + 1, :] + jnp.dot(ad, aup_ref[d], preferred_element_type=F32))
        w_o[0] = jnp.exp(logw)
        kt_o[0] = k * (1.0 + (a - 1.0) * ka_ref[...])
        al_o[0] = a
    r_o[0] = r
    v_o[0] = v
    kk_o[0] = kk
    g_o[0] = jnp.dot(_sigmoid(gd), gup_ref[...], preferred_element_type=F32)


def _rwkv_prep(pr, mu, w0, w_up, a0, a_up, g_up, k_k, k_a, ctx_tiles):
    bsz, n, cols = pr.shape
    nt = n // ROWS
    hb = ROWS // GRID_W
    nh = n // GRID_W
    full = lambda a: pl.BlockSpec(a.shape, lambda b, i, nd=a.ndim: (0,) * nd)
    out = pl.BlockSpec((1, ROWS, C_WIDTH), lambda b, i: (b, i, 0))
    return pl.pallas_call(
        functools.partial(_rwkv_prep_kernel, ctx_tiles=ctx_tiles, ntiles=nt),
        grid=(bsz, nt),
        in_specs=[pl.BlockSpec((1, ROWS, cols), lambda b, i: (b, i, 0)),
                  pl.BlockSpec((1, GRID_W, cols), lambda b, i: (b, jnp.maximum(i * hb - 1, 0), 0)),
                  pl.BlockSpec((1, GRID_W, cols), lambda b, i: (b, jnp.minimum((i + 1) * hb, nh - 1), 0)),
                  full(mu), full(w0), full(w_up), full(a0), full(a_up), full(g_up), full(k_k), full(k_a)],
        out_specs=[out] * 10,
        out_shape=[jax.ShapeDtypeStruct((bsz, n, C_WIDTH), F32)] * 10,
        compiler_params=_cparams(("parallel", "arbitrary")),
        name="rwkv_prep",
    )(pr, pr, pr, mu, w0, w_up, a0, a_up, g_up, k_k, k_a)


def _rwkv_kernel(r_ref, w_ref, k_ref, v_ref, kk_ref, al_ref, o_ref, s_ref, *, reverse):
    c = pl.program_id(0)

    @pl.when(c == 0)
    def _():
        s_ref[...] = jnp.zeros_like(s_ref)

    nb = r_ref.shape[0]
    t_len = r_ref.shape[1]
    sub = lax.broadcasted_iota(jnp.int32, (C_HEADDIM, C_WIDTH), 0)
    lane = lax.broadcasted_iota(jnp.int32, (C_HEADDIM, C_WIDTH), 1)
    diag = (lane % C_HEADDIM == sub).astype(F32)
    ones = _head_ones(LANES, C_HEADDIM).astype(BF16)

    def head_sum(x):
        hi = x.astype(BF16)
        lo = (x - hi.astype(F32)).astype(BF16)
        parts = []
        for j in range(C_WIDTH // LANES):
            sl = slice(j * LANES, (j + 1) * LANES)
            parts.append(jnp.dot(hi[:, sl], ones, preferred_element_type=F32)
                         + jnp.dot(lo[:, sl], ones, preferred_element_type=F32))
        return jnp.concatenate(parts, axis=1)

    def step(i, states):
        t = (t_len - 1 - i) if reverse else i
        new = []
        for b in range(nb):
            s = states[b]
            rr = r_ref[b, pl.ds(t, 1), :]
            ww = w_ref[b, pl.ds(t, 1), :]
            kt = k_ref[b, pl.ds(t, 1), :]
            vv = v_ref[b, pl.ds(t, 1), :]
            kk = kk_ref[b, pl.ds(t, 1), :]
            al = al_ref[b, pl.ds(t, 1), :]
            sa = head_sum(s * (-kk))
            vcol = head_sum(diag * vv)
            s = s * ww + sa * (kk * al) + vcol * kt
            y = head_sum(s * rr)
            o_ref[b, pl.ds(t, 1), :] = jnp.sum(y * diag, axis=0, keepdims=True)
            new.append(s)
        return tuple(new)

    states = lax.fori_loop(0, t_len, step, tuple(s_ref[b] for b in range(nb)))
    for b in range(nb):
        s_ref[b] = states[b]


def _rwkv_scan(r, w, kt, v, kk, al, nc_ctx, reverse):
    bsz, n, width = r.shape
    nc = n // CHUNK
    order = lambda c: _chunk_order(c, nc_ctx, nc, reverse)
    spec = pl.BlockSpec((bsz, CHUNK, width), lambda c: (0, order(c), 0))
    return pl.pallas_call(
        functools.partial(_rwkv_kernel, reverse=reverse),
        grid=(nc,),
        in_specs=[spec] * 6,
        out_specs=spec,
        out_shape=jax.ShapeDtypeStruct((bsz, n, width), F32),
        scratch_shapes=[pltpu.VMEM((bsz, C_HEADDIM, width), F32)],
        compiler_params=_cparams(("arbitrary",)),
        name="rwkv_scan",
    )(r, w, kt, v, kk, al)


def _lru_gate_kernel(xc_ref, w_ref, b_ref, lam_ref, a0_o, b0_o, a1_o, b1_o):
    xc = xc_ref[0]
    z = jnp.dot(xc, w_ref[...], preferred_element_type=F32) + b_ref[...]
    for d, (a_o, b_o) in enumerate(((a0_o, b0_o), (a1_o, b1_o))):
        gr = _sigmoid(z[:, 2 * d * D_WIDTH:(2 * d + 1) * D_WIDTH])
        gi = _sigmoid(z[:, (2 * d + 1) * D_WIDTH:(2 * d + 2) * D_WIDTH])
        log_a = -LRU_C * gr * _softplus(-lam_ref[d:d + 1, :])
        th = jnp.tanh(log_a)
        one_minus_a2 = -2.0 * th / (1.0 - th)
        a_o[0] = jnp.exp(log_a)
        b_o[0] = jnp.sqrt(one_minus_a2) * (gi * xc)


def _lru_gates(xc, w, b, lam):
    bsz, n, width = xc.shape
    nt = n // ROWS
    out = pl.BlockSpec((1, ROWS, width), lambda b_, i: (b_, i, 0))
    return pl.pallas_call(
        _lru_gate_kernel,
        grid=(bsz, nt),
        in_specs=[out,
                  pl.BlockSpec(w.shape, lambda b_, i: (0, 0)),
                  pl.BlockSpec(b.shape, lambda b_, i: (0, 0)),
                  pl.BlockSpec(lam.shape, lambda b_, i: (0, 0))],
        out_specs=[out] * 4,
        out_shape=[jax.ShapeDtypeStruct((bsz, n, width), F32)] * 4,
        compiler_params=_cparams(("parallel", "arbitrary")),
        name="lru_gates",
    )(xc, w, b, lam)


def _lru_kernel(a_ref, b_ref, o_ref, h_ref, *, reverse):
    c = pl.program_id(0)

    @pl.when(c == 0)
    def _():
        h_ref[...] = jnp.zeros_like(h_ref)

    nb = a_ref.shape[0]
    t_len = a_ref.shape[1]

    def step(i, hs):
        t = (t_len - 1 - i) if reverse else i
        new = []
        for b in range(nb):
            h = a_ref[b, pl.ds(t, 1), :] * hs[b] + b_ref[b, pl.ds(t, 1), :]
            o_ref[b, pl.ds(t, 1), :] = h
            new.append(h)
        return tuple(new)

    hs = lax.fori_loop(0, t_len, step, tuple(h_ref[b] for b in range(nb)))
    for b in range(nb):
        h_ref[b] = hs[b]


def _lru_scan(a, b, nc_ctx, reverse):
    bsz, n, width = a.shape
    nc = n // CHUNK
    order = lambda c: _chunk_order(c, nc_ctx, nc, reverse)
    spec = pl.BlockSpec((bsz, CHUNK, width), lambda c: (0, order(c), 0))
    return pl.pallas_call(
        functools.partial(_lru_kernel, reverse=reverse),
        grid=(nc,),
        in_specs=[spec, spec],
        out_specs=spec,
        out_shape=jax.ShapeDtypeStruct((bsz, n, width), F32),
        scratch_shapes=[pltpu.VMEM((bsz, 1, width), F32)],
        compiler_params=_cparams(("arbitrary",)),
        name="lru_scan",
    )(a, b)


def _finish1_kernel(yf_ref, yb_ref, uf_ref, ub_ref, r_ref, kt0_ref, kt1_ref, v_ref, g_ref, gb_ref,
                    x_ref, mod_ref, rk_ref, lw_ref, lb_ref, w_ref, out_ref):
    y = yf_ref[0] + yb_ref[0]
    hs = _head_ones(C_WIDTH, C_HEADDIM)
    mean = jnp.dot(y, hs, precision=HIGHEST, preferred_element_type=F32) * (1.0 / C_HEADDIM)
    yc = y - mean
    var = jnp.dot(yc * yc, hs, precision=HIGHEST, preferred_element_type=F32) * (1.0 / C_HEADDIM)
    yn = yc * lax.rsqrt(var + RWKV_GN_EPS) * lw_ref[...] + lb_ref[...]
    kb = 0.5 * (kt0_ref[0] + kt1_ref[0])
    bonus = jnp.dot(r_ref[0] * kb * rk_ref[...], hs, precision=HIGHEST, preferred_element_type=F32)
    yn = yn + bonus * v_ref[0]
    yc_ = yn * g_ref[0]
    yd = (uf_ref[0] + ub_ref[0]) * _gelu(gb_ref[0])
    f = jnp.concatenate([yc_, yd], axis=1)
    g1 = mod_ref[0, :, 2 * D_MODEL:3 * D_MODEL]
    out_ref[0] = x_ref[0] + g1 * jnp.dot(f.astype(BF16), w_ref[...], preferred_element_type=F32)


def _finish1(yf, yb, uf, ub, r, kt0, kt1, v, g, gate_br, xs, modsel, r_k, ln_w, ln_b, w_out, ctx_tiles):
    bsz, n, d = xs.shape
    nt = n // ROWS
    half = pl.BlockSpec((1, ROWS, 512), lambda b, i: (b, i, 0))
    vec = pl.BlockSpec((1, 512), lambda b, i: (0, 0))
    return pl.pallas_call(
        _finish1_kernel,
        grid=(bsz, nt),
        in_specs=[half] * 10 + [pl.BlockSpec((1, ROWS, d), lambda b, i: (b, i, 0)),
                                _mod_spec(ctx_tiles), vec, vec, vec,
                                pl.BlockSpec((d, d), lambda b, i: (0, 0))],
        out_specs=pl.BlockSpec((1, ROWS, d), lambda b, i: (b, i, 0)),
        out_shape=jax.ShapeDtypeStruct((bsz, n, d), F32),
        compiler_params=_cparams(("parallel", "arbitrary")),
        name="finish_rwkv_lru",
    )(yf, yb, uf, ub, r, kt0, kt1, v, g, gate_br, xs, modsel, r_k, ln_w, ln_b, w_out)


def _topk_rows(s, payload=None):
    m = s.shape[0]
    iota = lax.broadcasted_iota(jnp.int32, s.shape, 0)
    vals, idxs = [], []
    for _ in range(PEER_TOPK):
        mx = jnp.max(s, axis=0, keepdims=True)
        am = jnp.min(jnp.where(s == mx, iota, m), axis=0, keepdims=True)
        sel = iota == am
        vals.append(mx)
        if payload is None:
            idxs.append(am)
        else:
            idxs.append(jnp.max(jnp.where(sel, payload, -1), axis=0, keepdims=True))
        s = jnp.where(sel, -jnp.inf, s)
    return vals, idxs


def _peer_select_kernel(x_ref, nw_ref, mod_ref, wq_ref, keys_ref, eid_ref, gate_ref):
    h = _norm_mod(x_ref[0], nw_ref[...], mod_ref, 3)
    q = jnp.dot(h.astype(BF16), wq_ref[...], preferred_element_type=F32)
    half = PEER_DK // 2
    for hd in range(PEER_HEADS):
        sv, si = [], []
        for p in range(2):
            hp = 2 * hd + p
            s = lax.dot_general(keys_ref[hp], q[:, hp * half:(hp + 1) * half],
                                (((1,), (1,)), ((), ())), preferred_element_type=F32)
            v, ix = _topk_rows(s)
            sv.append(v)
            si.append(ix)
        sv1 = jnp.concatenate(sv[1], axis=0)
        si1 = jnp.concatenate(si[1], axis=0)
        cand = jnp.concatenate([sv[0][a] + sv1 for a in range(PEER_TOPK)], axis=0)
        cidx = jnp.concatenate([si[0][a] * N_KEYS + si1 for a in range(PEER_TOPK)], axis=0)
        best, eid = _topk_rows(cand, cidx)
        bestm = jnp.concatenate(best, axis=0)
        e = jnp.exp(bestm - best[0])
        gate = e / jnp.sum(e, axis=0, keepdims=True)
        eid_ref[0, hd * PEER_TOPK:(hd + 1) * PEER_TOPK, :] = jnp.concatenate(eid, axis=0)
        gate_ref[0, hd * PEER_TOPK:(hd + 1) * PEER_TOPK, :] = gate


def _peer_select(xs, nw, modsel, wq, keys, ctx_tiles):
    bsz, n, d = xs.shape
    nt = n // ROWS
    out = pl.BlockSpec((1, PEER_SLOTS, ROWS), lambda b, i: (b * nt + i, 0, 0))
    return pl.pallas_call(
        _peer_select_kernel,
        grid=(bsz, nt),
        in_specs=[pl.BlockSpec((1, ROWS, d), lambda b, i: (b, i, 0)),
                  pl.BlockSpec((1, d), lambda b, i: (0, 0)),
                  _mod_spec(ctx_tiles),
                  pl.BlockSpec(wq.shape, lambda b, i: (0, 0)),
                  pl.BlockSpec(keys.shape, lambda b, i: (0, 0, 0))],
        out_specs=[out, out],
        out_shape=[jax.ShapeDtypeStruct((bsz * nt, PEER_SLOTS, ROWS), jnp.int32),
                   jax.ShapeDtypeStruct((bsz * nt, PEER_SLOTS, ROWS), F32)],
        compiler_params=_cparams(("parallel", "arbitrary")),
        name="peer_select",
    )(xs, nw, modsel, wq, keys)


def _peer_gather_kernel(eid_hbm, x_ref, nw_ref, mod_ref, gate_ref, uv_hbm, o_ref,
                        hbuf, buf, idx_smem, gsem, isem, *, nblk):
    i = pl.program_id(0)
    cur = i % 2
    tb = x_ref.shape[0]
    pair = 2 * ROW_TILES

    def idx_copy(blk, slot):
        return pltpu.make_async_copy(eid_hbm.at[blk], idx_smem.at[slot], isem.at[slot])

    @pl.when(i == 0)
    def _():
        idx_copy(0, 0).start()

    @pl.when(i + 1 < nblk)
    def _():
        idx_copy(i + 1, 1 - cur).start()

    idx_copy(i, cur).wait()

    hbuf[...] = _norm_mod(x_ref[...], nw_ref[...], mod_ref, 3)
    g2 = mod_ref[0, :, 5 * D_MODEL:6 * D_MODEL]

    def issue(t, slot):
        for j in range(PEER_SLOTS):
            e = idx_smem[cur, t, j]
            pltpu.make_async_copy(uv_hbm.at[pl.ds(e * pair, pair)],
                                  buf.at[slot, pl.ds(j * pair, pair)], gsem.at[slot]).start()

    for t0 in range(PEER_NBUF - 1):
        issue(t0, t0)

    lane = lax.broadcasted_iota(jnp.int32, (PEER_SLOTS, tb), 1)

    def body(t, carry):
        slot = t % PEER_NBUF
        nt = t + PEER_NBUF - 1

        @pl.when(nt < tb)
        def _():
            issue(nt, nt % PEER_NBUF)

        pltpu.make_async_copy(uv_hbm.at[pl.ds(0, PEER_SLOTS * pair)], buf.at[slot], gsem.at[slot]).wait()
        xrow = hbuf[pl.ds(t, 1), :]
        acc = jnp.zeros((PEER_SLOTS, LANES), F32)
        for s in range(ROW_TILES):
            us = buf[slot, pl.ds(s, PEER_SLOTS, stride=pair), :]
            acc = acc + us * xrow[:, s * LANES:(s + 1) * LANES]
        act = jnp.sum(acc, axis=1, keepdims=True)
        gcol = jnp.sum(jnp.where(lane == t, gate_ref[0], 0.0), axis=1, keepdims=True)
        coef = jnp.broadcast_to(gcol * _gelu(act), (PEER_SLOTS, LANES))
        outs = []
        for s in range(ROW_TILES):
            vs = buf[slot, pl.ds(ROW_TILES + s, PEER_SLOTS, stride=pair), :]
            outs.append(jnp.sum(vs * coef, axis=0, keepdims=True))
        orow = jnp.concatenate(outs, axis=1)
        o_ref[pl.ds(t, 1), :] = x_ref[pl.ds(t, 1), :] + g2 * orow
        return carry

    lax.fori_loop(0, tb, body, 0)


def _peer_gather(eid_t, gate, xs2, nw, modsel, uv, tiles_per_batch, ctx_tiles):
    ntok, d = xs2.shape
    nblk = ntok // PEER_TB
    pair = 2 * ROW_TILES
    return pl.pallas_call(
        functools.partial(_peer_gather_kernel, nblk=nblk),
        grid=(nblk,),
        in_specs=[pl.BlockSpec(memory_space=pl.ANY),
                  pl.BlockSpec((PEER_TB, d), lambda i: (i, 0)),
                  pl.BlockSpec((1, d), lambda i: (0, 0)),
                  pl.BlockSpec((1, 1, 6 * D_MODEL),
                               lambda i: (2 * (i // tiles_per_batch)
                                          + jnp.where(i % tiles_per_batch >= ctx_tiles, 1, 0), 0, 0)),
                  pl.BlockSpec((1, PEER_SLOTS, PEER_TB), lambda i: (i, 0, 0)),
                  pl.BlockSpec(memory_space=pl.ANY)],
        out_specs=pl.BlockSpec((PEER_TB, d), lambda i: (i, 0)),
        out_shape=jax.ShapeDtypeStruct((ntok, d), F32),
        scratch_shapes=[pltpu.VMEM((PEER_TB, d), F32),
                        pltpu.VMEM((PEER_NBUF, PEER_SLOTS * pair, LANES), F32),
                        pltpu.SMEM((2, PEER_TB, PEER_SLOTS), jnp.int32),
                        pltpu.SemaphoreType.DMA((PEER_NBUF,)),
                        pltpu.SemaphoreType.DMA((2,))],
        compiler_params=_cparams(("arbitrary",)),
        name="peer_gather",
    )(eid_t, xs2, nw, modsel, gate, uv)


def _peer(xs, nw, modsel, wq, keys, u_tab, v_tab, ctx_tiles):
    bsz, n, d = xs.shape
    eid, gate = _peer_select(xs, nw, modsel, wq, keys, ctx_tiles)
    nblk = bsz * n // PEER_TB
    split = ROWS // PEER_TB
    eid_t = eid.reshape(-1, PEER_SLOTS, split, PEER_TB).transpose(0, 2, 3, 1).reshape(nblk, PEER_TB, PEER_SLOTS)
    gate_b = gate.reshape(-1, PEER_SLOTS, split, PEER_TB).transpose(0, 2, 1, 3).reshape(nblk, PEER_SLOTS, PEER_TB)
    ne = u_tab.shape[0]
    uv = jnp.concatenate([u_tab.reshape(ne, ROW_TILES, LANES), v_tab.reshape(ne, ROW_TILES, LANES)],
                         axis=1).reshape(ne * 2 * ROW_TILES, LANES)
    out = _peer_gather(eid_t, gate_b, xs.reshape(bsz * n, d), nw, modsel, uv,
                       n // PEER_TB, ctx_tiles * ROWS // PEER_TB)
    return out.reshape(bsz, n, d)


def _final_kernel(x_ref, w_ref, o_ref):
    x = x_ref[0]
    ms = jnp.mean(x * x, axis=-1, keepdims=True)
    o_ref[0] = x * lax.rsqrt(ms + EPS) * w_ref[...]


def _final_norm(xs, w, ctx_tiles, seq):
    bsz, n, d = xs.shape
    return pl.pallas_call(
        _final_kernel,
        grid=(bsz, seq // ROWS),
        in_specs=[pl.BlockSpec((1, ROWS, d), lambda b, i: (b, i + ctx_tiles, 0)),
                  pl.BlockSpec((1, d), lambda b, i: (0, 0))],
        out_specs=pl.BlockSpec((1, ROWS, d), lambda b, i: (b, i, 0)),
        out_shape=jax.ShapeDtypeStruct((bsz, seq, d), F32),
        compiler_params=_cparams(("parallel", "arbitrary")),
        name="final_norm",
    )(xs, w)


def _block_diag(w):
    nb, d, e = w.shape
    eye = jnp.eye(nb, dtype=w.dtype)
    return (eye[:, None, :, None] * w[:, :, None, :]).reshape(nb * d, nb * e)


def _mixer0(xs, modsel, norm1, w_in, w_out, i_bias, f_bias, mlstm_norm, conv_w, conv_b, dt_bias,
            a_log, d_skip, ssd_norm, ctx_tiles, nc_ctx):
    q0, k0, v0, o0, ig0, fg0, z0, xbc0, dt0, end = 0, 256, 512, 1024, 1536, 1544, 1552, 2064, 3088, 3104
    pad = jnp.zeros((D_MODEL, LANES - 32), w_in.dtype)
    w_cat = jnp.concatenate([w_in[:, q0:ig0], w_in[:, xbc0:dt0], w_in[:, z0:xbc0],
                             w_in[:, ig0:z0], w_in[:, dt0:end], pad], axis=1).astype(BF16)
    qkvo, zx, gates = _project(xs, norm1, modsel, w_cat, (1536, 1536, LANES), ctx_tiles)
    gates_row = jnp.swapaxes(gates[:, :, :32], 1, 2)
    bias = jnp.concatenate([i_bias.reshape(-1), f_bias.reshape(-1), dt_bias.reshape(-1)])
    bias_col = jnp.pad(bias, (0, LANES - 32)).reshape(1, LANES)
    bias_row = bias.reshape(32, 1)
    alog = jnp.concatenate([jnp.zeros((16,), F32), a_log.reshape(-1)])
    alog_col = jnp.pad(alog, (0, LANES - 32)).reshape(1, LANES)
    alog_row = alog.reshape(32, 1)
    xbc_act = _conv4(zx, 1024, conv_w, conv_b.reshape(1, -1), ctx_tiles, True)
    hf = _mlstm_scan(qkvo, gates, gates_row, bias_col, bias_row, nc_ctx, 0)
    hb = _mlstm_scan(qkvo, gates, gates_row, bias_col, bias_row, nc_ctx, 1)
    yf = _ssd_scan(xbc_act, gates, gates_row, bias_col, bias_row, alog_col, alog_row, nc_ctx, 0)
    yb = _ssd_scan(xbc_act, gates, gates_row, bias_col, bias_row, alog_col, alog_row, nc_ctx, 1)
    dskip = jnp.repeat(d_skip, B_HEADDIM).reshape(1, -1)
    return _finish0(hf, hb, yf, yb, qkvo, zx, xbc_act, xs, modsel, mlstm_norm.reshape(1, -1),
                    ssd_norm.reshape(1, -1), dskip, w_out.astype(BF16), ctx_tiles)


def _mixer1(xs, modsel, norm1, w_in, w_out, mu, w0, w_up, a0, a_up, g_up, k_k, k_a, r_k, ln_w, ln_b,
            conv_w, conv_b, lam, wa, ba, wi, bi, ctx_tiles, nc_ctx):
    pr, gate_br, x_br = _project(xs, norm1, modsel, w_in.astype(BF16), (RWKV_COLS, D_WIDTH, D_WIDTH),
                                 ctx_tiles)
    r, v, kk, g, w_f, w_b, kt_f, kt_b, al_f, al_b = _rwkv_prep(
        pr, mu.reshape(1, -1), w0, w_up, a0, a_up, g_up, k_k.reshape(1, -1), k_a.reshape(1, -1), ctx_tiles)
    yf = _rwkv_scan(r, w_f, kt_f, v, kk, al_f, nc_ctx, False)
    yb = _rwkv_scan(r, w_b, kt_b, v, kk, al_b, nc_ctx, True)
    xc = _conv4(x_br, D_WIDTH, conv_w, conv_b.reshape(1, -1), ctx_tiles, False)
    w_gate = jnp.concatenate([_block_diag(wa[0]), _block_diag(wi[0]),
                              _block_diag(wa[1]), _block_diag(wi[1])], axis=1)
    b_gate = jnp.concatenate([ba[0], bi[0], ba[1], bi[1]]).reshape(1, -1)
    a_f, b_f, a_b, b_b = _lru_gates(xc, w_gate, b_gate, lam)
    uf = _lru_scan(a_f, b_f, nc_ctx, False)
    ub = _lru_scan(a_b, b_b, nc_ctx, True)
    return _finish1(yf, yb, uf, ub, r, kt_f, kt_b, v, g, gate_br, xs, modsel, r_k.reshape(1, -1),
                    ln_w.reshape(1, -1), ln_b.reshape(1, -1), w_out.astype(BF16), ctx_tiles)


def kernel(x, c, ctx, c_ctx, mod_w, mod_b, norm1, norm2, peer_wq, peer_keys, peer_u, peer_v, ev_w_in, ev_w_out, ev_mlstm_i_bias, ev_mlstm_f_bias, ev_mlstm_norm, ev_ssd_conv_w, ev_ssd_conv_b, ev_ssd_dt_bias, ev_ssd_a_log, ev_ssd_d, ev_ssd_norm, od_w_in, od_w_out, od_rwkv_mu, od_rwkv_w0, od_rwkv_w_up, od_rwkv_a0, od_rwkv_a_up, od_rwkv_g_up, od_rwkv_k_k, od_rwkv_k_a, od_rwkv_r_k, od_rwkv_ln_w, od_rwkv_ln_b, od_lru_conv_w, od_lru_conv_b, od_lru_lambda, od_lru_wa, od_lru_ba, od_lru_wi, od_lru_bi, final_norm):
    bsz, seq, d = x.shape
    ctx_len = ctx.shape[1]
    depth = mod_w.shape[0]
    assert d == D_MODEL and ctx_len == ROWS and seq % ROWS == 0 and bsz < SUBLANES
    ctx_tiles = ctx_len // ROWS
    nc_ctx = ctx_len // CHUNK
    xs = jnp.concatenate([ctx, x], axis=1)
    srows = jnp.concatenate([c, c_ctx[None, :], jnp.zeros((SUBLANES - bsz - 1, d), F32)], axis=0)
    for i in range(depth):
        mod = _modulation(srows, mod_w[i], mod_b[i].reshape(1, -1))
        modsel = jnp.stack([jnp.broadcast_to(mod[bsz], (bsz, 6 * d)), mod[:bsz]], axis=1)
        modsel = modsel.reshape(2 * bsz, 1, 6 * d)
        j = i // 2
        n1 = norm1[i].reshape(1, -1)
        if i % 2 == 0:
            xs = _mixer0(xs, modsel, n1, ev_w_in[j], ev_w_out[j], ev_mlstm_i_bias[j], ev_mlstm_f_bias[j],
                         ev_mlstm_norm[j], ev_ssd_conv_w[j], ev_ssd_conv_b[j], ev_ssd_dt_bias[j],
                         ev_ssd_a_log[j], ev_ssd_d[j], ev_ssd_norm[j], ctx_tiles, nc_ctx)
        else:
            xs = _mixer1(xs, modsel, n1, od_w_in[j], od_w_out[j], od_rwkv_mu[j], od_rwkv_w0[j],
                         od_rwkv_w_up[j], od_rwkv_a0[j], od_rwkv_a_up[j], od_rwkv_g_up[j], od_rwkv_k_k[j],
                         od_rwkv_k_a[j], od_rwkv_r_k[j].reshape(-1), od_rwkv_ln_w[j], od_rwkv_ln_b[j],
                         od_lru_conv_w[j], od_lru_conv_b[j], od_lru_lambda[j], od_lru_wa[j], od_lru_ba[j],
                         od_lru_wi[j], od_lru_bi[j], ctx_tiles, nc_ctx)
        keys = peer_keys[i].reshape(2 * PEER_HEADS, N_KEYS, PEER_DK // 2)
        xs = _peer(xs, norm2[i].reshape(1, -1), modsel, peer_wq[i].astype(BF16), keys,
                   peer_u[i], peer_v[i], ctx_tiles)
    return _final_norm(xs, final_norm.reshape(1, -1), ctx_tiles, seq)
```

```python
import functools
import math

import jax
import jax.numpy as jnp
from jax import lax
from jax.experimental import pallas as pl
from jax.experimental.pallas import tpu as pltpu

F32 = jnp.float32
BF16 = jnp.bfloat16
HIGHEST = lax.Precision.HIGHEST

D_MODEL = 1024
EPS = 1e-6
CHUNK = 128
ROWS = 256
GRID_W = 64
LANES = 128
SUBLANES = 8
MIB = 1024 * 1024

A_HEADS, A_DQK, A_DV = 4, 64, 128
A_WIDTH = A_HEADS * A_DV
B_HEADS, B_HEADDIM, B_GROUPS, B_STATE = 8, 64, 2, 128
B_WIDTH = B_HEADS * B_HEADDIM
C_HEADS, C_HEADDIM = 8, 64
C_WIDTH = C_HEADS * C_HEADDIM
C_LORA_W, C_LORA_A, C_LORA_G = 64, 64, 128
RWKV_COLS = 3 * C_WIDTH + C_LORA_W + C_LORA_A + C_LORA_G
RWKV_W_SCALE = math.exp(-0.5)
RWKV_GN_EPS = 64e-5
D_WIDTH = 512
LRU_C = 8.0
PEER_HEADS, PEER_DK, N_KEYS, PEER_TOPK = 8, 256, 128, 16
PEER_SLOTS = PEER_HEADS * PEER_TOPK
PEER_TB = 128
PEER_NBUF = 4
ROW_TILES = D_MODEL // LANES


def _cparams(sem, vmem_mib=48):
    return pltpu.CompilerParams(dimension_semantics=sem, vmem_limit_bytes=vmem_mib * MIB)


def _softplus(x):
    return jnp.maximum(x, 0.0) + jnp.log1p(jnp.exp(-jnp.abs(x)))


def _sigmoid(x):
    return 1.0 / (1.0 + jnp.exp(-x))


def _silu(x):
    return x * _sigmoid(x)


def _gelu(x):
    return 0.5 * x * (1.0 + lax.erf(x * (1.0 / math.sqrt(2.0))))


def _norm_mod(x, nw, mod_ref, slot):
    ms = jnp.mean(x * x, axis=-1, keepdims=True)
    y = x * lax.rsqrt(ms + EPS) * nw
    sh = mod_ref[0, :, slot * D_MODEL:(slot + 1) * D_MODEL]
    sc = mod_ref[0, :, (slot + 1) * D_MODEL:(slot + 2) * D_MODEL]
    return y * (1.0 + sc) + sh


def _chunk_order(c, nc_ctx, nc, reverse):
    if not reverse:
        return c
    return jnp.where(c < nc_ctx, nc_ctx - 1 - c, nc + nc_ctx - 1 - c)


def _mod_kernel(s_ref, w_ref, b_ref, o_ref):
    s = _silu(s_ref[...])
    o_ref[...] = jnp.dot(s, w_ref[...], precision=HIGHEST, preferred_element_type=F32) + b_ref[...]


def _modulation(srows, w, b):
    d = srows.shape[1]
    nt = w.shape[1] // d
    return pl.pallas_call(
        _mod_kernel,
        grid=(nt,),
        in_specs=[pl.BlockSpec((SUBLANES, d), lambda j: (0, 0)),
                  pl.BlockSpec((d, d), lambda j: (0, j)),
                  pl.BlockSpec((1, d), lambda j: (0, j))],
        out_specs=pl.BlockSpec((SUBLANES, d), lambda j: (0, j)),
        out_shape=jax.ShapeDtypeStruct((SUBLANES, w.shape[1]), F32),
        compiler_params=_cparams(("arbitrary",)),
        name="modulation",
    )(srows, w, b)


def _proj_kernel(x_ref, nw_ref, mod_ref, w_ref, *o_refs, widths):
    h = _norm_mod(x_ref[0], nw_ref[...], mod_ref, 0)
    out = jnp.dot(h.astype(BF16), w_ref[...], preferred_element_type=F32)
    off = 0
    for o_ref, wd in zip(o_refs, widths):
        o_ref[0] = out[:, off:off + wd]
        off += wd


def _mod_spec(ctx_tiles):
    return pl.BlockSpec((1, 1, 6 * D_MODEL),
                        lambda b, i: (2 * b + jnp.where(i >= ctx_tiles, 1, 0), 0, 0))


def _project(xs, nw, modsel, w, widths, ctx_tiles):
    bsz, n, d = xs.shape
    nt = n // ROWS
    return pl.pallas_call(
        functools.partial(_proj_kernel, widths=widths),
        grid=(bsz, nt),
        in_specs=[pl.BlockSpec((1, ROWS, d), lambda b, i: (b, i, 0)),
                  pl.BlockSpec((1, d), lambda b, i: (0, 0)),
                  _mod_spec(ctx_tiles),
                  pl.BlockSpec(w.shape, lambda b, i: (0, 0))],
        out_specs=[pl.BlockSpec((1, ROWS, wd), lambda b, i: (b, i, 0)) for wd in widths],
        out_shape=[jax.ShapeDtypeStruct((bsz, n, wd), F32) for wd in widths],
        compiler_params=_cparams(("parallel", "arbitrary")),
        name="norm_mod_project",
    )(xs, nw, modsel, w)


def _conv_kernel(x_ref, p_ref, n_ref, w_ref, b_ref, o_ref, *, ctx_tiles, ntiles, act):
    i = pl.program_id(1)
    x = x_ref[0]
    rows = x.shape[0]
    prev_ok = jnp.logical_and(i != 0, i != ctx_tiles)
    next_ok = jnp.logical_and(i != ctx_tiles - 1, i != ntiles - 1)
    p = jnp.where(prev_ok, p_ref[0], 0.0)
    nx = jnp.where(next_ok, n_ref[0], 0.0)
    row = lax.broadcasted_iota(jnp.int32, (rows, 1), 0)
    xm1 = jnp.where(row == 0, p[7:8], pltpu.roll(x, 1, 0))
    xm2 = jnp.where(row == 0, p[6:7], jnp.where(row == 1, p[7:8], pltpu.roll(x, 2, 0)))
    xp1 = jnp.where(row == rows - 1, nx[0:1], pltpu.roll(x, rows - 1, 0))
    w = w_ref[...]
    y = b_ref[...] + xm2 * w[0:1] + xm1 * w[1:2] + x * w[2:3] + xp1 * w[3:4]
    o_ref[0] = _silu(y) if act else y


def _conv4(x, width, w, b, ctx_tiles, act):
    bsz, n, _ = x.shape
    nt = n // ROWS
    hb = ROWS // SUBLANES
    nh = n // SUBLANES
    return pl.pallas_call(
        functools.partial(_conv_kernel, ctx_tiles=ctx_tiles, ntiles=nt, act=act),
        grid=(bsz, nt),
        in_specs=[pl.BlockSpec((1, ROWS, width), lambda b_, i: (b_, i, 0)),
                  pl.BlockSpec((1, SUBLANES, width), lambda b_, i: (b_, jnp.maximum(i * hb - 1, 0), 0)),
                  pl.BlockSpec((1, SUBLANES, width),
                               lambda b_, i: (b_, jnp.minimum((i + 1) * hb, nh - 1), 0)),
                  pl.BlockSpec((4, width), lambda b_, i: (0, 0)),
                  pl.BlockSpec((1, width), lambda b_, i: (0, 0))],
        out_specs=pl.BlockSpec((1, ROWS, width), lambda b_, i: (b_, i, 0)),
        out_shape=jax.ShapeDtypeStruct((bsz, n, width), F32),
        compiler_params=_cparams(("parallel", "arbitrary")),
        name="conv4",
    )(x, x, x, w, b)


def _tri(reverse):
    t = lax.broadcasted_iota(jnp.int32, (CHUNK, CHUNK), 0)
    s = lax.broadcasted_iota(jnp.int32, (CHUNK, CHUNK), 1)
    return (s >= t) if reverse else (s <= t)


def _cumsums(col, row, mask):
    mf = mask.astype(F32)
    b_col = jnp.dot(mf, col, precision=HIGHEST, preferred_element_type=F32)
    b_row = lax.dot_general(row, mf, (((1,), (1,)), ((), ())), precision=HIGHEST,
                            preferred_element_type=F32)
    return b_col, b_row


def _mlstm_kernel(qkv_ref, gc_ref, gr_ref, bc_ref, br_ref, o_ref, ct_ref, n_ref, m_ref, *,
                  direction):
    c = pl.program_id(1)

    @pl.when(c == 0)
    def _():
        ct_ref[...] = jnp.zeros_like(ct_ref)
        n_ref[...] = jnp.zeros_like(n_ref)
        m_ref[...] = jnp.zeros_like(m_ref)

    reverse = direction == 1
    mask = _tri(reverse)
    gc = gc_ref[0] + bc_ref[...]
    gr = gr_ref[0] + br_ref[...]
    d4 = direction * A_HEADS
    li_col = gc[:, d4:d4 + A_HEADS]
    lf_col = -_softplus(-gc[:, 8 + d4:8 + d4 + A_HEADS])
    li_row = gr[d4:d4 + A_HEADS, :]
    lf_row = -_softplus(-gr[8 + d4:8 + d4 + A_HEADS, :])
    b_col, b_row = _cumsums(lf_col, lf_row, mask)
    tot = jnp.sum(lf_col, axis=0, keepdims=True)
    outs = []
    for h in range(A_HEADS):
        q = qkv_ref[0, :, h * A_DQK:(h + 1) * A_DQK] * (A_DQK ** -0.5)
        k = qkv_ref[0, :, A_HEADS * A_DQK + h * A_DQK:A_HEADS * A_DQK + (h + 1) * A_DQK]
        v = qkv_ref[0, :, 2 * A_HEADS * A_DQK + h * A_DV:2 * A_HEADS * A_DQK + (h + 1) * A_DV]
        bc, br = b_col[:, h:h + 1], b_row[h:h + 1, :]
        ic, ir = li_col[:, h:h + 1], li_row[h:h + 1, :]
        th = tot[:, h:h + 1]
        m_prev = m_ref[h]
        dmat = jnp.where(mask, bc - br + ir, -jnp.inf)
        inter = bc + m_prev
        m_t = jnp.maximum(inter, jnp.max(dmat, axis=1, keepdims=True))
        qk = lax.dot_general(q, k, (((1,), (1,)), ((), ())), preferred_element_type=F32)
        s = qk * jnp.exp(dmat - m_t)
        w_inter = jnp.exp(inter - m_t)
        num = jnp.dot(s, v, preferred_element_type=F32) \
            + w_inter * jnp.dot(q, ct_ref[h], preferred_element_type=F32)
        den = jnp.sum(s, axis=1, keepdims=True) \
            + w_inter * jnp.sum(q * n_ref[h], axis=1, keepdims=True)
        outs.append(num / jnp.maximum(jnp.abs(den), jnp.exp(-m_t)))
        g = th - bc + ic
        m_new = jnp.maximum(th + m_prev, jnp.max(g, axis=0, keepdims=True))
        wg = jnp.exp(g - m_new)
        wc = jnp.exp(th + m_prev - m_new)
        ct_ref[h] = wc * ct_ref[h] + lax.dot_general(k * wg, v, (((0,), (0,)), ((), ())),
                                                     preferred_element_type=F32)
        n_ref[h] = wc * n_ref[h] + jnp.sum(wg * k, axis=0, keepdims=True)
        m_ref[h] = m_new
    o_ref[0] = jnp.concatenate(outs, axis=1)


def _mlstm_scan(qkvo, gates, gates_row, bias_col, bias_row, nc_ctx, direction):
    bsz, n, _ = qkvo.shape
    nc = n // CHUNK
    reverse = direction == 1
    order = lambda c: _chunk_order(c, nc_ctx, nc, reverse)
    return pl.pallas_call(
        functools.partial(_mlstm_kernel, direction=direction),
        grid=(bsz, nc),
        in_specs=[pl.BlockSpec((1, CHUNK, 2 * A_HEADS * A_DQK + A_WIDTH), lambda b, c: (b, order(c), 0)),
                  pl.BlockSpec((1, CHUNK, LANES), lambda b, c: (b, order(c), 0)),
                  pl.BlockSpec((1, 32, CHUNK), lambda b, c: (b, 0, order(c))),
                  pl.BlockSpec((1, LANES), lambda b, c: (0, 0)),
                  pl.BlockSpec((32, 1), lambda b, c: (0, 0))],
        out_specs=pl.BlockSpec((1, CHUNK, A_WIDTH), lambda b, c: (b, order(c), 0)),
        out_shape=jax.ShapeDtypeStruct((bsz, n, A_WIDTH), F32),
        scratch_shapes=[pltpu.VMEM((A_HEADS, A_DQK, A_DV), F32),
                        pltpu.VMEM((A_HEADS, 1, A_DQK), F32),
                        pltpu.VMEM((A_HEADS, 1, 1), F32)],
        compiler_params=_cparams(("parallel", "arbitrary")),
        name="mlstm_scan",
    )(qkvo, gates, gates_row, bias_col, bias_row)


def _ssd_kernel(xbc_ref, gc_ref, gr_ref, bc_ref, br_ref, ac_ref, ar_ref, o_ref, st_ref, *,
                direction):
    c = pl.program_id(1)

    @pl.when(c == 0)
    def _():
        st_ref[...] = jnp.zeros_like(st_ref)

    reverse = direction == 1
    mask = _tri(reverse)
    gc = gc_ref[0] + bc_ref[...]
    gr = gr_ref[0] + br_ref[...]
    d8 = 16 + direction * B_HEADS
    dt_col = _softplus(gc[:, d8:d8 + B_HEADS])
    dt_row = _softplus(gr[d8:d8 + B_HEADS, :])
    la_col = -dt_col * jnp.exp(ac_ref[:, d8:d8 + B_HEADS])
    la_row = -dt_row * jnp.exp(ar_ref[d8:d8 + B_HEADS, :])
    b_col, b_row = _cumsums(la_col, la_row, mask)
    tot = jnp.sum(la_col, axis=0, keepdims=True)
    outs = []
    hpg = B_HEADS // B_GROUPS
    for g in range(B_GROUPS):
        bm = xbc_ref[0, :, B_WIDTH + g * B_STATE:B_WIDTH + (g + 1) * B_STATE]
        cm = xbc_ref[0, :, B_WIDTH + (B_GROUPS + g) * B_STATE:B_WIDTH + (B_GROUPS + g + 1) * B_STATE]
        cb = lax.dot_general(cm, bm, (((1,), (1,)), ((), ())), preferred_element_type=F32)
        for h in range(g * hpg, (g + 1) * hpg):
            xh = xbc_ref[0, :, h * B_HEADDIM:(h + 1) * B_HEADDIM]
            bc, br = b_col[:, h:h + 1], b_row[h:h + 1, :]
            th = tot[:, h:h + 1]
            decay = jnp.exp(jnp.where(mask, bc - br, -jnp.inf))
            s = cb * decay * dt_row[h:h + 1, :]
            y = jnp.dot(s, xh, preferred_element_type=F32) \
                + jnp.exp(bc) * jnp.dot(cm, st_ref[h], preferred_element_type=F32)
            outs.append(y)
            w_end = jnp.exp(th - bc) * dt_col[:, h:h + 1]
            st_ref[h] = jnp.exp(th) * st_ref[h] + lax.dot_general(
                bm * w_end, xh, (((0,), (0,)), ((), ())), preferred_element_type=F32)
    o_ref[0] = jnp.concatenate(outs, axis=1)


def _ssd_scan(xbc, gates, gates_row, bias_col, bias_row, alog_col, alog_row, nc_ctx, direction):
    bsz, n, width = xbc.shape
    nc = n // CHUNK
    reverse = direction == 1
    order = lambda c: _chunk_order(c, nc_ctx, nc, reverse)
    return pl.pallas_call(
        functools.partial(_ssd_kernel, direction=direction),
        grid=(bsz, nc),
        in_specs=[pl.BlockSpec((1, CHUNK, width), lambda b, c: (b, order(c), 0)),
                  pl.BlockSpec((1, CHUNK, LANES), lambda b, c: (b, order(c), 0)),
                  pl.BlockSpec((1, 32, CHUNK), lambda b, c: (b, 0, order(c))),
                  pl.BlockSpec((1, LANES), lambda b, c: (0, 0)),
                  pl.BlockSpec((32, 1), lambda b, c: (0, 0)),
                  pl.BlockSpec((1, LANES), lambda b, c: (0, 0)),
                  pl.BlockSpec((32, 1), lambda b, c: (0, 0))],
        out_specs=pl.BlockSpec((1, CHUNK, B_WIDTH), lambda b, c: (b, order(c), 0)),
        out_shape=jax.ShapeDtypeStruct((bsz, n, B_WIDTH), F32),
        scratch_shapes=[pltpu.VMEM((B_HEADS, B_STATE, B_HEADDIM), F32)],
        compiler_params=_cparams(("parallel", "arbitrary")),
        name="ssd_scan",
    )(xbc, gates, gates_row, bias_col, bias_row, alog_col, alog_row)


def _finish0_kernel(hf_ref, hb_ref, yf_ref, yb_ref, o_ref_in, z_ref, xs_ref, x_ref, mod_ref,
                    mn_ref, sn_ref, dsk_ref, w_ref, out_ref):
    h = hf_ref[0] + hb_ref[0]
    parts = []
    for hd in range(A_HEADS):
        hh = h[:, hd * A_DV:(hd + 1) * A_DV]
        ms = jnp.mean(hh * hh, axis=-1, keepdims=True)
        parts.append(hh * lax.rsqrt(ms + EPS))
    hn = jnp.concatenate(parts, axis=1) * mn_ref[...]
    ya = _sigmoid(o_ref_in[0]) * hn
    y = yf_ref[0] + yb_ref[0] + dsk_ref[...] * xs_ref[0]
    t = y * _silu(z_ref[0])
    ms = jnp.mean(t * t, axis=-1, keepdims=True)
    yb = t * lax.rsqrt(ms + EPS) * sn_ref[...]
    f = jnp.concatenate([ya, yb], axis=1)
    g1 = mod_ref[0, :, 2 * D_MODEL:3 * D_MODEL]
    out_ref[0] = x_ref[0] + g1 * jnp.dot(f.astype(BF16), w_ref[...], preferred_element_type=F32)


def _finish0(hf, hb, yf, yb, qkvo, zx, xbc_act, xs, modsel, mnorm, snorm, dskip, w_out, ctx_tiles):
    bsz, n, d = xs.shape
    nt = n // ROWS
    half = lambda j: pl.BlockSpec((1, ROWS, 512), lambda b, i, j=j: (b, i, j))
    vec = pl.BlockSpec((1, 512), lambda b, i: (0, 0))
    return pl.pallas_call(
        _finish0_kernel,
        grid=(bsz, nt),
        in_specs=[half(0), half(0), half(0), half(0), half(2), half(2), half(0),
                  pl.BlockSpec((1, ROWS, d), lambda b, i: (b, i, 0)),
                  _mod_spec(ctx_tiles), vec, vec, vec,
                  pl.BlockSpec((d, d), lambda b, i: (0, 0))],
        out_specs=pl.BlockSpec((1, ROWS, d), lambda b, i: (b, i, 0)),
        out_shape=jax.ShapeDtypeStruct((bsz, n, d), F32),
        compiler_params=_cparams(("parallel", "arbitrary")),
        name="finish_mlstm_ssd",
    )(hf, hb, yf, yb, qkvo, zx, xbc_act, xs, modsel, mnorm, snorm, dskip, w_out)


def _head_ones(n, hd):
    r = lax.broadcasted_iota(jnp.int32, (n, n), 0) // hd
    c = lax.broadcasted_iota(jnp.int32, (n, n), 1) // hd
    return (r == c).astype(F32)


def _rwkv_prep_kernel(pr_ref, p_ref, n_ref, mu_ref, w0_ref, wup_ref, a0_ref, aup_ref, gup_ref,
                      kk_ref, ka_ref, r_o, v_o, kk_o, g_o, w0_o, w1_o, kt0_o, kt1_o, al0_o, al1_o,
                      *, ctx_tiles, ntiles):
    i = pl.program_id(1)
    pr = pr_ref[0]
    rows, cols = pr.shape
    q = cols // 4
    row = lax.broadcasted_iota(jnp.int32, (rows, 1), 0)
    col = lax.broadcasted_iota(jnp.int32, (1, cols), 1)
    is_ctx = i < ctx_tiles
    l1 = pltpu.roll(pr, 1, 0)
    r1 = pltpu.roll(pr, rows - 1, 0)
    left = jnp.where(row % GRID_W == 0, 0.0, l1)
    right = jnp.where(row % GRID_W == GRID_W - 1, 0.0, r1)
    up = jnp.concatenate([p_ref[0], pr[:rows - GRID_W]], axis=0)
    up = jnp.where(jnp.logical_and(i == ctx_tiles, row < GRID_W), 0.0, up)
    down = jnp.concatenate([pr[GRID_W:], n_ref[0]], axis=0)
    down = jnp.where(jnp.logical_and(i == ntiles - 1, row >= rows - GRID_W), 0.0, down)
    grid_sh = jnp.where(col < q, left, jnp.where(col < 2 * q, right, jnp.where(col < 3 * q, up, down)))
    prev = jnp.where(row == 0, 0.0, l1)
    nxt = jnp.where(row == rows - 1, 0.0, r1)
    seq_sh = jnp.where(col < 2 * q, prev, nxt)
    shifted = jnp.where(is_ctx, seq_sh, grid_sh)
    pr = pr + mu_ref[...] * (shifted - pr)
    r = pr[:, 0:C_WIDTH]
    k = pr[:, C_WIDTH:2 * C_WIDTH]
    v = pr[:, 2 * C_WIDTH:3 * C_WIDTH]
    o = 3 * C_WIDTH
    wd = pr[:, o:o + C_LORA_W]
    ad = pr[:, o + C_LORA_W:o + C_LORA_W + C_LORA_A]
    gd = pr[:, o + C_LORA_W + C_LORA_A:]
    kk = k * kk_ref[...]
    ss = jnp.dot(kk * kk, _head_ones(C_WIDTH, C_HEADDIM), precision=HIGHEST, preferred_element_type=F32)
    kk = kk * lax.rsqrt(ss + 1e-12)
    tw = jnp.tanh(wd)
    for d, (w_o, kt_o, al_o) in enumerate(((w0_o, kt0_o, al0_o), (w1_o, kt1_o, al1_o))):
        logw = -RWKV_W_SCALE * _sigmoid(w0_ref[d:d + 1, :] + jnp.dot(tw, wup_ref[d], preferred_element_type=F32))
        a = _sigmoid(a0_ref[d:d + 1, :] + jnp.dot(ad, aup_ref[d], preferred_element_type=F32))
        w_o[0] = jnp.exp(logw)
        kt_o[0] = k * (1.0 + (a - 1.0) * ka_ref[...])
        al_o[0] = a
    r_o[0] = r
    v_o[0] = v
    kk_o[0] = kk
    g_o[0] = jnp.dot(_sigmoid(gd), gup_ref[...], preferred_element_type=F32)


def _rwkv_prep(pr, mu, w0, w_up, a0, a_up, g_up, k_k, k_a, ctx_tiles):
    bsz, n, cols = pr.shape
    nt = n // ROWS
    hb = ROWS // GRID_W
    nh = n // GRID_W
    full = lambda a: pl.BlockSpec(a.shape, lambda b, i, nd=a.ndim: (0,) * nd)
    out = pl.BlockSpec((1, ROWS, C_WIDTH), lambda b, i: (b, i, 0))
    return pl.pallas_call(
        functools.partial(_rwkv_prep_kernel, ctx_tiles=ctx_tiles, ntiles=nt),
        grid=(bsz, nt),
        in_specs=[pl.BlockSpec((1, ROWS, cols), lambda b, i: (b, i, 0)),
                  pl.BlockSpec((1, GRID_W, cols), lambda b, i: (b, jnp.maximum(i * hb - 1, 0), 0)),
                  pl.BlockSpec((1, GRID_W, cols), lambda b, i: (b, jnp.minimum((i + 1) * hb, nh - 1), 0)),
                  full(mu), full(w0), full(w_up), full(a0), full(a_up), full(g_up), full(k_k), full(k_a)],
        out_specs=[out] * 10,
        out_shape=[jax.ShapeDtypeStruct((bsz, n, C_WIDTH), F32)] * 10,
        compiler_params=_cparams(("parallel", "arbitrary")),
        name="rwkv_prep",
    )(pr, pr, pr, mu, w0, w_up, a0, a_up, g_up, k_k, k_a)


RWKV_UNROLL = 2


def _rwkv_kernel(rf, wf, kf, vf, kkf, alf, rb, wb, kb, vb, kkb, alb, of_ref, ob_ref, s_ref):
    c = pl.program_id(0)

    @pl.when(c == 0)
    def _():
        s_ref[...] = jnp.zeros_like(s_ref)

    nb = rf.shape[0]
    t_len = rf.shape[1]
    nt = C_WIDTH // LANES
    sub = lax.broadcasted_iota(jnp.int32, (C_HEADDIM, C_WIDTH), 0)
    lane = lax.broadcasted_iota(jnp.int32, (C_HEADDIM, C_WIDTH), 1)
    diag = (lane % C_HEADDIM == sub).astype(F32)
    ones = _head_ones(LANES, C_HEADDIM).astype(BF16)
    ones2 = jnp.concatenate([ones, ones], axis=0)
    chains = [(refs, b) for refs in ((rf, wf, kf, vf, kkf, alf, of_ref, False),
                                     (rb, wb, kb, vb, kkb, alb, ob_ref, True)) for b in range(nb)]

    def head_sum(xs, split):
        x = jnp.concatenate([a[:, j * LANES:(j + 1) * LANES] for a in xs for j in range(nt)], axis=0)
        if split:
            hi = x.astype(BF16)
            lo = (x - hi.astype(F32)).astype(BF16)
            out = jnp.dot(jnp.concatenate([hi, lo], axis=1), ones2, preferred_element_type=F32)
        else:
            out = jnp.dot(x.astype(BF16), ones, preferred_element_type=F32)
        res = []
        for n in range(len(xs)):
            res.append(jnp.concatenate(
                [out[(nt * n + j) * C_HEADDIM:(nt * n + j + 1) * C_HEADDIM] for j in range(nt)], axis=1))
        return res

    def step(i, states):
        rows = []
        for refs, b in chains:
            t = (t_len - 1 - i) if refs[7] else i
            rows.append([ref[b, pl.ds(t, 1), :] for ref in refs[:6]] + [t])
        sa = [head_sum([s * (-row[4])], True)[0] for s, row in zip(states, rows)]
        vcol = [head_sum([diag * row[3]], False)[0] for row in rows]
        new = [s * row[1] + a * (row[4] * row[5]) + vc * row[2]
               for s, row, a, vc in zip(states, rows, sa, vcol)]
        ys = [head_sum([s * row[0]], False)[0] for s, row in zip(new, rows)]
        for (refs, b), row, y in zip(chains, rows, ys):
            refs[6][b, pl.ds(row[6], 1), :] = jnp.sum(y * diag, axis=0, keepdims=True)
        return new

    def body(g, states):
        states = list(states)
        for u in range(RWKV_UNROLL):
            states = step(g * RWKV_UNROLL + u, states)
        return tuple(states)

    states = lax.fori_loop(0, t_len // RWKV_UNROLL, body,
                           tuple(s_ref[n] for n in range(len(chains))))
    for n in range(len(chains)):
        s_ref[n] = states[n]


def _rwkv_scan(r, v, kk, w_f, kt_f, al_f, w_b, kt_b, al_b, nc_ctx):
    bsz, n, width = r.shape
    nc = n // CHUNK
    fwd = pl.BlockSpec((bsz, CHUNK, width), lambda c: (0, c, 0))
    bwd = pl.BlockSpec((bsz, CHUNK, width), lambda c: (0, _chunk_order(c, nc_ctx, nc, True), 0))
    return pl.pallas_call(
        _rwkv_kernel,
        grid=(nc,),
        in_specs=[fwd] * 6 + [bwd] * 6,
        out_specs=[fwd, bwd],
        out_shape=[jax.ShapeDtypeStruct((bsz, n, width), F32)] * 2,
        scratch_shapes=[pltpu.VMEM((2 * bsz, C_HEADDIM, width), F32)],
        compiler_params=_cparams(("arbitrary",)),
        name="rwkv_scan",
    )(r, w_f, kt_f, v, kk, al_f, r, w_b, kt_b, v, kk, al_b)


def _lru_gate_kernel(xc_ref, w_ref, b_ref, lam_ref, a0_o, b0_o, a1_o, b1_o):
    xc = xc_ref[0]
    z = jnp.dot(xc, w_ref[...], preferred_element_type=F32) + b_ref[...]
    for d, (a_o, b_o) in enumerate(((a0_o, b0_o), (a1_o, b1_o))):
        gr = _sigmoid(z[:, 2 * d * D_WIDTH:(2 * d + 1) * D_WIDTH])
        gi = _sigmoid(z[:, (2 * d + 1) * D_WIDTH:(2 * d + 2) * D_WIDTH])
        log_a = -LRU_C * gr * _softplus(-lam_ref[d:d + 1, :])
        th = jnp.tanh(log_a)
        one_minus_a2 = -2.0 * th / (1.0 - th)
        a_o[0] = jnp.exp(log_a)
        b_o[0] = jnp.sqrt(one_minus_a2) * (gi * xc)


def _lru_gates(xc, w, b, lam):
    bsz, n, width = xc.shape
    nt = n // ROWS
    out = pl.BlockSpec((1, ROWS, width), lambda b_, i: (b_, i, 0))
    return pl.pallas_call(
        _lru_gate_kernel,
        grid=(bsz, nt),
        in_specs=[out,
                  pl.BlockSpec(w.shape, lambda b_, i: (0, 0)),
                  pl.BlockSpec(b.shape, lambda b_, i: (0, 0)),
                  pl.BlockSpec(lam.shape, lambda b_, i: (0, 0))],
        out_specs=[out] * 4,
        out_shape=[jax.ShapeDtypeStruct((bsz, n, width), F32)] * 4,
        compiler_params=_cparams(("parallel", "arbitrary")),
        name="lru_gates",
    )(xc, w, b, lam)


def _lru_kernel(a_ref, b_ref, o_ref, h_ref, *, reverse):
    c = pl.program_id(0)

    @pl.when(c == 0)
    def _():
        h_ref[...] = jnp.zeros_like(h_ref)

    nb = a_ref.shape[0]
    t_len = a_ref.shape[1]

    def step(i, hs):
        t = (t_len - 1 - i) if reverse else i
        new = []
        for b in range(nb):
            h = a_ref[b, pl.ds(t, 1), :] * hs[b] + b_ref[b, pl.ds(t, 1), :]
            o_ref[b, pl.ds(t, 1), :] = h
            new.append(h)
        return tuple(new)

    hs = lax.fori_loop(0, t_len, step, tuple(h_ref[b] for b in range(nb)))
    for b in range(nb):
        h_ref[b] = hs[b]


def _lru_scan(a, b, nc_ctx, reverse):
    bsz, n, width = a.shape
    nc = n // CHUNK
    order = lambda c: _chunk_order(c, nc_ctx, nc, reverse)
    spec = pl.BlockSpec((bsz, CHUNK, width), lambda c: (0, order(c), 0))
    return pl.pallas_call(
        functools.partial(_lru_kernel, reverse=reverse),
        grid=(nc,),
        in_specs=[spec, spec],
        out_specs=spec,
        out_shape=jax.ShapeDtypeStruct((bsz, n, width), F32),
        scratch_shapes=[pltpu.VMEM((bsz, 1, width), F32)],
        compiler_params=_cparams(("arbitrary",)),
        name="lru_scan",
    )(a, b)


def _finish1_kernel(yf_ref, yb_ref, uf_ref, ub_ref, r_ref, kt0_ref, kt1_ref, v_ref, g_ref, gb_ref,
                    x_ref, mod_ref, rk_ref, lw_ref, lb_ref, w_ref, out_ref):
    y = yf_ref[0] + yb_ref[0]
    hs = _head_ones(C_WIDTH, C_HEADDIM)
    mean = jnp.dot(y, hs, precision=HIGHEST, preferred_element_type=F32) * (1.0 / C_HEADDIM)
    yc = y - mean
    var = jnp.dot(yc * yc, hs, precision=HIGHEST, preferred_element_type=F32) * (1.0 / C_HEADDIM)
    yn = yc * lax.rsqrt(var + RWKV_GN_EPS) * lw_ref[...] + lb_ref[...]
    kb = 0.5 * (kt0_ref[0] + kt1_ref[0])
    bonus = jnp.dot(r_ref[0] * kb * rk_ref[...], hs, precision=HIGHEST, preferred_element_type=F32)
    yn = yn + bonus * v_ref[0]
    yc_ = yn * g_ref[0]
    yd = (uf_ref[0] + ub_ref[0]) * _gelu(gb_ref[0])
    f = jnp.concatenate([yc_, yd], axis=1)
    g1 = mod_ref[0, :, 2 * D_MODEL:3 * D_MODEL]
    out_ref[0] = x_ref[0] + g1 * jnp.dot(f.astype(BF16), w_ref[...], preferred_element_type=F32)


def _finish1(yf, yb, uf, ub, r, kt0, kt1, v, g, gate_br, xs, modsel, r_k, ln_w, ln_b, w_out, ctx_tiles):
    bsz, n, d = xs.shape
    nt = n // ROWS
    half = pl.BlockSpec((1, ROWS, 512), lambda b, i: (b, i, 0))
    vec = pl.BlockSpec((1, 512), lambda b, i: (0, 0))
    return pl.pallas_call(
        _finish1_kernel,
        grid=(bsz, nt),
        in_specs=[half] * 10 + [pl.BlockSpec((1, ROWS, d), lambda b, i: (b, i, 0)),
                                _mod_spec(ctx_tiles), vec, vec, vec,
                                pl.BlockSpec((d, d), lambda b, i: (0, 0))],
        out_specs=pl.BlockSpec((1, ROWS, d), lambda b, i: (b, i, 0)),
        out_shape=jax.ShapeDtypeStruct((bsz, n, d), F32),
        compiler_params=_cparams(("parallel", "arbitrary")),
        name="finish_rwkv_lru",
    )(yf, yb, uf, ub, r, kt0, kt1, v, g, gate_br, xs, modsel, r_k, ln_w, ln_b, w_out)


def _max_arg(s):
    tiles = s.shape[0] // SUBLANES
    vals = [s[i * SUBLANES:(i + 1) * SUBLANES] for i in range(tiles)]
    row = lax.broadcasted_iota(jnp.int32, (SUBLANES, s.shape[1]), 0)
    ids = [row + i * SUBLANES for i in range(tiles)]
    while len(vals) > 1:
        nv, ni = [], []
        for a in range(0, len(vals) - 1, 2):
            take = vals[a + 1] > vals[a]
            nv.append(jnp.where(take, vals[a + 1], vals[a]))
            ni.append(jnp.where(take, ids[a + 1], ids[a]))
        if len(vals) % 2:
            nv.append(vals[-1])
            ni.append(ids[-1])
        vals, ids = nv, ni
    mx = jnp.max(vals[0], axis=0, keepdims=True)
    am = jnp.min(jnp.where(vals[0] == mx, ids[0], s.shape[0]), axis=0, keepdims=True)
    return mx, am


def _topk_rows(s, payload=None):
    iota = lax.broadcasted_iota(jnp.int32, s.shape, 0)
    vals, idxs = [], []
    for _ in range(PEER_TOPK):
        mx, am = _max_arg(s)
        sel = iota == am
        vals.append(mx)
        if payload is None:
            idxs.append(am)
        else:
            idxs.append(jnp.max(jnp.where(sel, payload, -1), axis=0, keepdims=True))
        s = jnp.where(sel, -jnp.inf, s)
    return vals, idxs


def _peer_candidates(sv0, si0, sv1, si1):
    grp = SUBLANES
    v1_all, i1_all = jnp.concatenate(sv1, axis=0), jnp.concatenate(si1, axis=0)
    v1_lo, i1_lo = v1_all[:grp], i1_all[:grp]
    row = lax.broadcasted_iota(jnp.int32, v1_lo.shape, 0)
    cand, cidx = [sv0[0] + v1_all], [si0[0] * N_KEYS + i1_all]
    for a in range(1, grp):
        cand.append(jnp.where(row < PEER_TOPK // (a + 1), sv0[a] + v1_lo, -jnp.inf))
        cidx.append(si0[a] * N_KEYS + i1_lo)
    cand.append(jnp.concatenate(sv0[grp:], axis=0) + sv1[0])
    cidx.append(jnp.concatenate(si0[grp:], axis=0) * N_KEYS + si1[0])
    return jnp.concatenate(cand, axis=0), jnp.concatenate(cidx, axis=0)


def _peer_select_kernel(x_ref, nw_ref, mod_ref, wq_ref, keys_ref, eid_ref, gate_ref):
    h = _norm_mod(x_ref[0], nw_ref[...], mod_ref, 3)
    q = jnp.dot(h.astype(BF16), wq_ref[...], preferred_element_type=F32)
    half = PEER_DK // 2
    for hd in range(PEER_HEADS):
        sv, si = [], []
        for p in range(2):
            hp = 2 * hd + p
            s = lax.dot_general(keys_ref[hp], q[:, hp * half:(hp + 1) * half],
                                (((1,), (1,)), ((), ())), preferred_element_type=F32)
            v, ix = _topk_rows(s)
            sv.append(v)
            si.append(ix)
        cand, cidx = _peer_candidates(sv[0], si[0], sv[1], si[1])
        best, eid = _topk_rows(cand, cidx)
        bestm = jnp.concatenate(best, axis=0)
        e = jnp.exp(bestm - best[0])
        gate = e / jnp.sum(e, axis=0, keepdims=True)
        eid_ref[0, hd * PEER_TOPK:(hd + 1) * PEER_TOPK, :] = jnp.concatenate(eid, axis=0)
        gate_ref[0, hd * PEER_TOPK:(hd + 1) * PEER_TOPK, :] = gate


def _peer_select(xs, nw, modsel, wq, keys, ctx_tiles):
    bsz, n, d = xs.shape
    nt = n // ROWS
    out = pl.BlockSpec((1, PEER_SLOTS, ROWS), lambda b, i: (b * nt + i, 0, 0))
    return pl.pallas_call(
        _peer_select_kernel,
        grid=(bsz, nt),
        in_specs=[pl.BlockSpec((1, ROWS, d), lambda b, i: (b, i, 0)),
                  pl.BlockSpec((1, d), lambda b, i: (0, 0)),
                  _mod_spec(ctx_tiles),
                  pl.BlockSpec(wq.shape, lambda b, i: (0, 0)),
                  pl.BlockSpec(keys.shape, lambda b, i: (0, 0, 0))],
        out_specs=[out, out],
        out_shape=[jax.ShapeDtypeStruct((bsz * nt, PEER_SLOTS, ROWS), jnp.int32),
                   jax.ShapeDtypeStruct((bsz * nt, PEER_SLOTS, ROWS), F32)],
        compiler_params=_cparams(("parallel", "arbitrary")),
        name="peer_select",
    )(xs, nw, modsel, wq, keys)


def _peer_gather_kernel(eid_hbm, x_ref, nw_ref, mod_ref, gate_ref, uv_hbm, o_ref,
                        hbuf, buf, idx_smem, gsem, isem, *, nblk):
    i = pl.program_id(0)
    tb = x_ref.shape[0]
    pair = 2 * ROW_TILES
    nidx = idx_smem.shape[0]
    cur = i % nidx
    nxt = (i + 1) % nidx

    def idx_copy(blk):
        slot = blk % nidx
        return pltpu.make_async_copy(eid_hbm.at[blk], idx_smem.at[slot], isem.at[slot])

    @pl.when(i == 0)
    def _():
        idx_copy(0).start()
        if nblk > 1:
            idx_copy(1).start()
        idx_copy(0).wait()

    @pl.when(i + 2 < nblk)
    def _():
        idx_copy(i + 2).start()

    @pl.when(i + 1 < nblk)
    def _():
        idx_copy(i + 1).wait()

    hbuf[...] = _norm_mod(x_ref[...], nw_ref[...], mod_ref, 3)
    g2 = mod_ref[0, :, 5 * D_MODEL:6 * D_MODEL]

    def issue(islot, t, slot):
        for j in range(PEER_SLOTS):
            e = idx_smem[islot, t, j]
            pltpu.make_async_copy(uv_hbm.at[pl.ds(e * pair, pair)],
                                  buf.at[slot, pl.ds(j * pair, pair)], gsem.at[slot]).start()

    @pl.when(i == 0)
    def _():
        for t0 in range(PEER_NBUF - 1):
            issue(0, t0, t0)

    lane = lax.broadcasted_iota(jnp.int32, (PEER_SLOTS, tb), 1)

    def body(t, carry):
        slot = t % PEER_NBUF
        nt = t + PEER_NBUF - 1
        here = nt < tb

        @pl.when(jnp.logical_or(here, i + 1 < nblk))
        def _():
            issue(jnp.where(here, cur, nxt), jnp.where(here, nt, nt - tb), nt % PEER_NBUF)

        pltpu.make_async_copy(uv_hbm.at[pl.ds(0, PEER_SLOTS * pair)], buf.at[slot], gsem.at[slot]).wait()
        xrow = hbuf[pl.ds(t, 1), :]
        acc = jnp.zeros((PEER_SLOTS, LANES), F32)
        for s in range(ROW_TILES):
            us = buf[slot, pl.ds(s, PEER_SLOTS, stride=pair), :]
            acc = acc + us * xrow[:, s * LANES:(s + 1) * LANES]
        act = jnp.sum(acc, axis=1, keepdims=True)
        gcol = jnp.sum(jnp.where(lane == t, gate_ref[0], 0.0), axis=1, keepdims=True)
        coef = jnp.broadcast_to(gcol * _gelu(act), (PEER_SLOTS, LANES))
        outs = []
        for s in range(ROW_TILES):
            vs = buf[slot, pl.ds(ROW_TILES + s, PEER_SLOTS, stride=pair), :]
            outs.append(jnp.sum(vs * coef, axis=0, keepdims=True))
        orow = jnp.concatenate(outs, axis=1)
        o_ref[pl.ds(t, 1), :] = x_ref[pl.ds(t, 1), :] + g2 * orow
        return carry

    lax.fori_loop(0, tb, body, 0)


def _peer_gather(eid_t, gate, xs2, nw, modsel, uv, tiles_per_batch, ctx_tiles):
    ntok, d = xs2.shape
    nblk = ntok // PEER_TB
    pair = 2 * ROW_TILES
    assert PEER_TB % PEER_NBUF == 0
    return pl.pallas_call(
        functools.partial(_peer_gather_kernel, nblk=nblk),
        grid=(nblk,),
        in_specs=[pl.BlockSpec(memory_space=pl.ANY),
                  pl.BlockSpec((PEER_TB, d), lambda i: (i, 0)),
                  pl.BlockSpec((1, d), lambda i: (0, 0)),
                  pl.BlockSpec((1, 1, 6 * D_MODEL),
                               lambda i: (2 * (i // tiles_per_batch)
                                          + jnp.where(i % tiles_per_batch >= ctx_tiles, 1, 0), 0, 0)),
                  pl.BlockSpec((1, PEER_SLOTS, PEER_TB), lambda i: (i, 0, 0)),
                  pl.BlockSpec(memory_space=pl.ANY)],
        out_specs=pl.BlockSpec((PEER_TB, d), lambda i: (i, 0)),
        out_shape=jax.ShapeDtypeStruct((ntok, d), F32),
        scratch_shapes=[pltpu.VMEM((PEER_TB, d), F32),
                        pltpu.VMEM((PEER_NBUF, PEER_SLOTS * pair, LANES), F32),
                        pltpu.SMEM((3, PEER_TB, PEER_SLOTS), jnp.int32),
                        pltpu.SemaphoreType.DMA((PEER_NBUF,)),
                        pltpu.SemaphoreType.DMA((3,))],
        compiler_params=_cparams(("arbitrary",)),
        name="peer_gather",
    )(eid_t, xs2, nw, modsel, gate, uv)


def _peer(xs, nw, modsel, wq, keys, u_tab, v_tab, ctx_tiles):
    bsz, n, d = xs.shape
    eid, gate = _peer_select(xs, nw, modsel, wq, keys, ctx_tiles)
    nblk = bsz * n // PEER_TB
    split = ROWS // PEER_TB
    eid_t = eid.reshape(-1, PEER_SLOTS, split, PEER_TB).transpose(0, 2, 3, 1).reshape(nblk, PEER_TB, PEER_SLOTS)
    gate_b = gate.reshape(-1, PEER_SLOTS, split, PEER_TB).transpose(0, 2, 1, 3).reshape(nblk, PEER_SLOTS, PEER_TB)
    ne = u_tab.shape[0]
    uv = jnp.concatenate([u_tab.reshape(ne, ROW_TILES, LANES), v_tab.reshape(ne, ROW_TILES, LANES)],
                         axis=1).reshape(ne * 2 * ROW_TILES, LANES)
    out = _peer_gather(eid_t, gate_b, xs.reshape(bsz * n, d), nw, modsel, uv,
                       n // PEER_TB, ctx_tiles * ROWS // PEER_TB)
    return out.reshape(bsz, n, d)


def _final_kernel(x_ref, w_ref, o_ref):
    x = x_ref[0]
    ms = jnp.mean(x * x, axis=-1, keepdims=True)
    o_ref[0] = x * lax.rsqrt(ms + EPS) * w_ref[...]


def _final_norm(xs, w, ctx_tiles, seq):
    bsz, n, d = xs.shape
    return pl.pallas_call(
        _final_kernel,
        grid=(bsz, seq // ROWS),
        in_specs=[pl.BlockSpec((1, ROWS, d), lambda b, i: (b, i + ctx_tiles, 0)),
                  pl.BlockSpec((1, d), lambda b, i: (0, 0))],
        out_specs=pl.BlockSpec((1, ROWS, d), lambda b, i: (b, i, 0)),
        out_shape=jax.ShapeDtypeStruct((bsz, seq, d), F32),
        compiler_params=_cparams(("parallel", "arbitrary")),
        name="final_norm",
    )(xs, w)


def _block_diag(w):
    nb, d, e = w.shape
    eye = jnp.eye(nb, dtype=w.dtype)
    return (eye[:, None, :, None] * w[:, :, None, :]).reshape(nb * d, nb * e)


def _mixer0(xs, modsel, norm1, w_in, w_out, i_bias, f_bias, mlstm_norm, conv_w, conv_b, dt_bias,
            a_log, d_skip, ssd_norm, ctx_tiles, nc_ctx):
    q0, k0, v0, o0, ig0, fg0, z0, xbc0, dt0, end = 0, 256, 512, 1024, 1536, 1544, 1552, 2064, 3088, 3104
    pad = jnp.zeros((D_MODEL, LANES - 32), w_in.dtype)
    w_cat = jnp.concatenate([w_in[:, q0:ig0], w_in[:, xbc0:dt0], w_in[:, z0:xbc0],
                             w_in[:, ig0:z0], w_in[:, dt0:end], pad], axis=1).astype(BF16)
    qkvo, zx, gates = _project(xs, norm1, modsel, w_cat, (1536, 1536, LANES), ctx_tiles)
    gates_row = jnp.swapaxes(gates[:, :, :32], 1, 2)
    bias = jnp.concatenate([i_bias.reshape(-1), f_bias.reshape(-1), dt_bias.reshape(-1)])
    bias_col = jnp.pad(bias, (0, LANES - 32)).reshape(1, LANES)
    bias_row = bias.reshape(32, 1)
    alog = jnp.concatenate([jnp.zeros((16,), F32), a_log.reshape(-1)])
    alog_col = jnp.pad(alog, (0, LANES - 32)).reshape(1, LANES)
    alog_row = alog.reshape(32, 1)
    xbc_act = _conv4(zx, 1024, conv_w, conv_b.reshape(1, -1), ctx_tiles, True)
    hf = _mlstm_scan(qkvo, gates, gates_row, bias_col, bias_row, nc_ctx, 0)
    hb = _mlstm_scan(qkvo, gates, gates_row, bias_col, bias_row, nc_ctx, 1)
    yf = _ssd_scan(xbc_act, gates, gates_row, bias_col, bias_row, alog_col, alog_row, nc_ctx, 0)
    yb = _ssd_scan(xbc_act, gates, gates_row, bias_col, bias_row, alog_col, alog_row, nc_ctx, 1)
    dskip = jnp.repeat(d_skip, B_HEADDIM).reshape(1, -1)
    return _finish0(hf, hb, yf, yb, qkvo, zx, xbc_act, xs, modsel, mlstm_norm.reshape(1, -1),
                    ssd_norm.reshape(1, -1), dskip, w_out.astype(BF16), ctx_tiles)


def _mixer1(xs, modsel, norm1, w_in, w_out, mu, w0, w_up, a0, a_up, g_up, k_k, k_a, r_k, ln_w, ln_b,
            conv_w, conv_b, lam, wa, ba, wi, bi, ctx_tiles, nc_ctx):
    pr, gate_br, x_br = _project(xs, norm1, modsel, w_in.astype(BF16), (RWKV_COLS, D_WIDTH, D_WIDTH),
                                 ctx_tiles)
    r, v, kk, g, w_f, w_b, kt_f, kt_b, al_f, al_b = _rwkv_prep(
        pr, mu.reshape(1, -1), w0, w_up, a0, a_up, g_up, k_k.reshape(1, -1), k_a.reshape(1, -1), ctx_tiles)
    yf, yb = _rwkv_scan(r, v, kk, w_f, kt_f, al_f, w_b, kt_b, al_b, nc_ctx)
    xc = _conv4(x_br, D_WIDTH, conv_w, conv_b.reshape(1, -1), ctx_tiles, False)
    w_gate = jnp.concatenate([_block_diag(wa[0]), _block_diag(wi[0]),
                              _block_diag(wa[1]), _block_diag(wi[1])], axis=1)
    b_gate = jnp.concatenate([ba[0], bi[0], ba[1], bi[1]]).reshape(1, -1)
    a_f, b_f, a_b, b_b = _lru_gates(xc, w_gate, b_gate, lam)
    uf = _lru_scan(a_f, b_f, nc_ctx, False)
    ub = _lru_scan(a_b, b_b, nc_ctx, True)
    return _finish1(yf, yb, uf, ub, r, kt_f, kt_b, v, g, gate_br, xs, modsel, r_k.reshape(1, -1),
                    ln_w.reshape(1, -1), ln_b.reshape(1, -1), w_out.astype(BF16), ctx_tiles)


def kernel(x, c, ctx, c_ctx, mod_w, mod_b, norm1, norm2, peer_wq, peer_keys, peer_u, peer_v, ev_w_in, ev_w_out, ev_mlstm_i_bias, ev_mlstm_f_bias, ev_mlstm_norm, ev_ssd_conv_w, ev_ssd_conv_b, ev_ssd_dt_bias, ev_ssd_a_log, ev_ssd_d, ev_ssd_norm, od_w_in, od_w_out, od_rwkv_mu, od_rwkv_w0, od_rwkv_w_up, od_rwkv_a0, od_rwkv_a_up, od_rwkv_g_up, od_rwkv_k_k, od_rwkv_k_a, od_rwkv_r_k, od_rwkv_ln_w, od_rwkv_ln_b, od_lru_conv_w, od_lru_conv_b, od_lru_lambda, od_lru_wa, od_lru_ba, od_lru_wi, od_lru_bi, final_norm):
    bsz, seq, d = x.shape
    ctx_len = ctx.shape[1]
    depth = mod_w.shape[0]
    assert d == D_MODEL and ctx_len == ROWS and seq % ROWS == 0 and bsz < SUBLANES
    ctx_tiles = ctx_len // ROWS
    nc_ctx = ctx_len // CHUNK
    xs = jnp.concatenate([ctx, x], axis=1)
    srows = jnp.concatenate([c, c_ctx[None, :], jnp.zeros((SUBLANES - bsz - 1, d), F32)], axis=0)
    for i in range(depth):
        mod = _modulation(srows, mod_w[i], mod_b[i].reshape(1, -1))
        modsel = jnp.stack([jnp.broadcast_to(mod[bsz], (bsz, 6 * d)), mod[:bsz]], axis=1)
        modsel = modsel.reshape(2 * bsz, 1, 6 * d)
        j = i // 2
        n1 = norm1[i].reshape(1, -1)
        if i % 2 == 0:
            xs = _mixer0(xs, modsel, n1, ev_w_in[j], ev_w_out[j], ev_mlstm_i_bias[j], ev_mlstm_f_bias[j],
                         ev_mlstm_norm[j], ev_ssd_conv_w[j], ev_ssd_conv_b[j], ev_ssd_dt_bias[j],
                         ev_ssd_a_log[j], ev_ssd_d[j], ev_ssd_norm[j], ctx_tiles, nc_ctx)
        else:
            xs = _mixer1(xs, modsel, n1, od_w_in[j], od_w_out[j], od_rwkv_mu[j], od_rwkv_w0[j],
                         od_rwkv_w_up[j], od_rwkv_a0[j], od_rwkv_a_up[j], od_rwkv_g_up[j], od_rwkv_k_k[j],
                         od_rwkv_k_a[j], od_rwkv_r_k[j].reshape(-1), od_rwkv_ln_w[j], od_rwkv_ln_b[j],
                         od_lru_conv_w[j], od_lru_conv_b[j], od_lru_lambda[j], od_lru_wa[j], od_lru_ba[j],
                         od_lru_wi[j], od_lru_bi[j], ctx_tiles, nc_ctx)
        keys = peer_keys[i].reshape(2 * PEER_HEADS, N_KEYS, PEER_DK // 2)
        xs = _peer(xs, norm2[i].reshape(1, -1), modsel, peer_wq[i].astype(BF16), keys,
                   peer_u[i], peer_v[i], ctx_tiles)
    return _final_norm(xs, final_norm.reshape(1, -1), ctx_tiles, seq)
```

```python
import functools
import math

import jax
import jax.numpy as jnp
from jax import lax
from jax.experimental import pallas as pl
from jax.experimental.pallas import tpu as pltpu

F32 = jnp.float32
BF16 = jnp.bfloat16
HIGHEST = lax.Precision.HIGHEST

D_MODEL = 1024
EPS = 1e-6
CHUNK = 128
ROWS = 256
GRID_W = 64
LANES = 128
SUBLANES = 8
MIB = 1024 * 1024

A_HEADS, A_DQK, A_DV = 4, 64, 128
A_WIDTH = A_HEADS * A_DV
B_HEADS, B_HEADDIM, B_GROUPS, B_STATE = 8, 64, 2, 128
B_WIDTH = B_HEADS * B_HEADDIM
C_HEADS, C_HEADDIM = 8, 64
C_WIDTH = C_HEADS * C_HEADDIM
C_LORA_W, C_LORA_A, C_LORA_G = 64, 64, 128
RWKV_COLS = 3 * C_WIDTH + C_LORA_W + C_LORA_A + C_LORA_G
RWKV_W_SCALE = math.exp(-0.5)
RWKV_GN_EPS = 64e-5
D_WIDTH = 512
LRU_C = 8.0
PEER_HEADS, PEER_DK, N_KEYS, PEER_TOPK = 8, 256, 128, 16
PEER_SLOTS = PEER_HEADS * PEER_TOPK
PEER_TB = 128
PEER_NBUF = 4
ROW_TILES = D_MODEL // LANES


def _cparams(sem, vmem_mib=48):
    return pltpu.CompilerParams(dimension_semantics=sem, vmem_limit_bytes=vmem_mib * MIB)


def _softplus(x):
    return jnp.maximum(x, 0.0) + jnp.log1p(jnp.exp(-jnp.abs(x)))


def _sigmoid(x):
    return 1.0 / (1.0 + jnp.exp(-x))


def _silu(x):
    return x * _sigmoid(x)


def _gelu(x):
    return 0.5 * x * (1.0 + lax.erf(x * (1.0 / math.sqrt(2.0))))


def _norm_mod(x, nw, mod_ref, slot):
    ms = jnp.mean(x * x, axis=-1, keepdims=True)
    y = x * lax.rsqrt(ms + EPS) * nw
    sh = mod_ref[0, :, slot * D_MODEL:(slot + 1) * D_MODEL]
    sc = mod_ref[0, :, (slot + 1) * D_MODEL:(slot + 2) * D_MODEL]
    return y * (1.0 + sc) + sh


def _chunk_order(c, nc_ctx, nc, reverse):
    if not reverse:
        return c
    return jnp.where(c < nc_ctx, nc_ctx - 1 - c, nc + nc_ctx - 1 - c)


def _mod_kernel(s_ref, w_ref, b_ref, o_ref):
    s = _silu(s_ref[...])
    o_ref[...] = jnp.dot(s, w_ref[...], precision=HIGHEST, preferred_element_type=F32) + b_ref[...]


def _modulation(srows, w, b):
    d = srows.shape[1]
    nt = w.shape[1] // d
    return pl.pallas_call(
        _mod_kernel,
        grid=(nt,),
        in_specs=[pl.BlockSpec((SUBLANES, d), lambda j: (0, 0)),
                  pl.BlockSpec((d, d), lambda j: (0, j)),
                  pl.BlockSpec((1, d), lambda j: (0, j))],
        out_specs=pl.BlockSpec((SUBLANES, d), lambda j: (0, j)),
        out_shape=jax.ShapeDtypeStruct((SUBLANES, w.shape[1]), F32),
        compiler_params=_cparams(("arbitrary",)),
        name="modulation",
    )(srows, w, b)


def _proj_kernel(x_ref, nw_ref, mod_ref, w_ref, *o_refs, widths):
    h = _norm_mod(x_ref[0], nw_ref[...], mod_ref, 0)
    out = jnp.dot(h.astype(BF16), w_ref[...], preferred_element_type=F32)
    off = 0
    for o_ref, wd in zip(o_refs, widths):
        o_ref[0] = out[:, off:off + wd]
        off += wd


def _mod_spec(ctx_tiles):
    return pl.BlockSpec((1, 1, 6 * D_MODEL),
                        lambda b, i: (2 * b + jnp.where(i >= ctx_tiles, 1, 0), 0, 0))


def _project(xs, nw, modsel, w, widths, ctx_tiles):
    bsz, n, d = xs.shape
    nt = n // ROWS
    return pl.pallas_call(
        functools.partial(_proj_kernel, widths=widths),
        grid=(bsz, nt),
        in_specs=[pl.BlockSpec((1, ROWS, d), lambda b, i: (b, i, 0)),
                  pl.BlockSpec((1, d), lambda b, i: (0, 0)),
                  _mod_spec(ctx_tiles),
                  pl.BlockSpec(w.shape, lambda b, i: (0, 0))],
        out_specs=[pl.BlockSpec((1, ROWS, wd), lambda b, i: (b, i, 0)) for wd in widths],
        out_shape=[jax.ShapeDtypeStruct((bsz, n, wd), F32) for wd in widths],
        compiler_params=_cparams(("parallel", "arbitrary")),
        name="norm_mod_project",
    )(xs, nw, modsel, w)


def _conv_kernel(x_ref, p_ref, n_ref, w_ref, b_ref, o_ref, *, ctx_tiles, ntiles, act):
    i = pl.program_id(1)
    x = x_ref[0]
    rows = x.shape[0]
    prev_ok = jnp.logical_and(i != 0, i != ctx_tiles)
    next_ok = jnp.logical_and(i != ctx_tiles - 1, i != ntiles - 1)
    p = jnp.where(prev_ok, p_ref[0], 0.0)
    nx = jnp.where(next_ok, n_ref[0], 0.0)
    row = lax.broadcasted_iota(jnp.int32, (rows, 1), 0)
    xm1 = jnp.where(row == 0, p[7:8], pltpu.roll(x, 1, 0))
    xm2 = jnp.where(row == 0, p[6:7], jnp.where(row == 1, p[7:8], pltpu.roll(x, 2, 0)))
    xp1 = jnp.where(row == rows - 1, nx[0:1], pltpu.roll(x, rows - 1, 0))
    w = w_ref[...]
    y = b_ref[...] + xm2 * w[0:1] + xm1 * w[1:2] + x * w[2:3] + xp1 * w[3:4]
    o_ref[0] = _silu(y) if act else y


def _conv4(x, width, w, b, ctx_tiles, act):
    bsz, n, _ = x.shape
    nt = n // ROWS
    hb = ROWS // SUBLANES
    nh = n // SUBLANES
    return pl.pallas_call(
        functools.partial(_conv_kernel, ctx_tiles=ctx_tiles, ntiles=nt, act=act),
        grid=(bsz, nt),
        in_specs=[pl.BlockSpec((1, ROWS, width), lambda b_, i: (b_, i, 0)),
                  pl.BlockSpec((1, SUBLANES, width), lambda b_, i: (b_, jnp.maximum(i * hb - 1, 0), 0)),
                  pl.BlockSpec((1, SUBLANES, width),
                               lambda b_, i: (b_, jnp.minimum((i + 1) * hb, nh - 1), 0)),
                  pl.BlockSpec((4, width), lambda b_, i: (0, 0)),
                  pl.BlockSpec((1, width), lambda b_, i: (0, 0))],
        out_specs=pl.BlockSpec((1, ROWS, width), lambda b_, i: (b_, i, 0)),
        out_shape=jax.ShapeDtypeStruct((bsz, n, width), F32),
        compiler_params=_cparams(("parallel", "arbitrary")),
        name="conv4",
    )(x, x, x, w, b)


def _tri(reverse):
    t = lax.broadcasted_iota(jnp.int32, (CHUNK, CHUNK), 0)
    s = lax.broadcasted_iota(jnp.int32, (CHUNK, CHUNK), 1)
    return (s >= t) if reverse else (s <= t)


def _cumsums(col, row, mask):
    mf = mask.astype(F32)
    b_col = jnp.dot(mf, col, precision=HIGHEST, preferred_element_type=F32)
    b_row = lax.dot_general(row, mf, (((1,), (1,)), ((), ())), precision=HIGHEST,
                            preferred_element_type=F32)
    return b_col, b_row


def _mlstm_kernel(qkv_ref, gc_ref, gr_ref, bc_ref, br_ref, o_ref, ct_ref, n_ref, m_ref, *,
                  direction):
    c = pl.program_id(1)

    @pl.when(c == 0)
    def _():
        ct_ref[...] = jnp.zeros_like(ct_ref)
        n_ref[...] = jnp.zeros_like(n_ref)
        m_ref[...] = jnp.zeros_like(m_ref)

    reverse = direction == 1
    mask = _tri(reverse)
    gc = gc_ref[0] + bc_ref[...]
    gr = gr_ref[0] + br_ref[...]
    d4 = direction * A_HEADS
    li_col = gc[:, d4:d4 + A_HEADS]
    lf_col = -_softplus(-gc[:, 8 + d4:8 + d4 + A_HEADS])
    li_row = gr[d4:d4 + A_HEADS, :]
    lf_row = -_softplus(-gr[8 + d4:8 + d4 + A_HEADS, :])
    b_col, b_row = _cumsums(lf_col, lf_row, mask)
    tot = jnp.sum(lf_col, axis=0, keepdims=True)
    outs = []
    for h in range(A_HEADS):
        q = qkv_ref[0, :, h * A_DQK:(h + 1) * A_DQK] * (A_DQK ** -0.5)
        k = qkv_ref[0, :, A_HEADS * A_DQK + h * A_DQK:A_HEADS * A_DQK + (h + 1) * A_DQK]
        v = qkv_ref[0, :, 2 * A_HEADS * A_DQK + h * A_DV:2 * A_HEADS * A_DQK + (h + 1) * A_DV]
        bc, br = b_col[:, h:h + 1], b_row[h:h + 1, :]
        ic, ir = li_col[:, h:h + 1], li_row[h:h + 1, :]
        th = tot[:, h:h + 1]
        m_prev = m_ref[h]
        dmat = jnp.where(mask, bc - br + ir, -jnp.inf)
        inter = bc + m_prev
        m_t = jnp.maximum(inter, jnp.max(dmat, axis=1, keepdims=True))
        qk = lax.dot_general(q, k, (((1,), (1,)), ((), ())), preferred_element_type=F32)
        s = qk * jnp.exp(dmat - m_t)
        w_inter = jnp.exp(inter - m_t)
        num = jnp.dot(s, v, preferred_element_type=F32) \
            + w_inter * jnp.dot(q, ct_ref[h], preferred_element_type=F32)
        den = jnp.sum(s, axis=1, keepdims=True) \
            + w_inter * jnp.sum(q * n_ref[h], axis=1, keepdims=True)
        outs.append(num / jnp.maximum(jnp.abs(den), jnp.exp(-m_t)))
        g = th - bc + ic
        m_new = jnp.maximum(th + m_prev, jnp.max(g, axis=0, keepdims=True))
        wg = jnp.exp(g - m_new)
        wc = jnp.exp(th + m_prev - m_new)
        ct_ref[h] = wc * ct_ref[h] + lax.dot_general(k * wg, v, (((0,), (0,)), ((), ())),
                                                     preferred_element_type=F32)
        n_ref[h] = wc * n_ref[h] + jnp.sum(wg * k, axis=0, keepdims=True)
        m_ref[h] = m_new
    o_ref[0] = jnp.concatenate(outs, axis=1)


def _mlstm_scan(qkvo, gates, gates_row, bias_col, bias_row, nc_ctx, direction):
    bsz, n, _ = qkvo.shape
    nc = n // CHUNK
    reverse = direction == 1
    order = lambda c: _chunk_order(c, nc_ctx, nc, reverse)
    return pl.pallas_call(
        functools.partial(_mlstm_kernel, direction=direction),
        grid=(bsz, nc),
        in_specs=[pl.BlockSpec((1, CHUNK, 2 * A_HEADS * A_DQK + A_WIDTH), lambda b, c: (b, order(c), 0)),
                  pl.BlockSpec((1, CHUNK, LANES), lambda b, c: (b, order(c), 0)),
                  pl.BlockSpec((1, 32, CHUNK), lambda b, c: (b, 0, order(c))),
                  pl.BlockSpec((1, LANES), lambda b, c: (0, 0)),
                  pl.BlockSpec((32, 1), lambda b, c: (0, 0))],
        out_specs=pl.BlockSpec((1, CHUNK, A_WIDTH), lambda b, c: (b, order(c), 0)),
        out_shape=jax.ShapeDtypeStruct((bsz, n, A_WIDTH), F32),
        scratch_shapes=[pltpu.VMEM((A_HEADS, A_DQK, A_DV), F32),
                        pltpu.VMEM((A_HEADS, 1, A_DQK), F32),
                        pltpu.VMEM((A_HEADS, 1, 1), F32)],
        compiler_params=_cparams(("parallel", "arbitrary")),
        name="mlstm_scan",
    )(qkvo, gates, gates_row, bias_col, bias_row)


def _ssd_kernel(xbc_ref, gc_ref, gr_ref, bc_ref, br_ref, ac_ref, ar_ref, o_ref, st_ref, *,
                direction):
    c = pl.program_id(1)

    @pl.when(c == 0)
    def _():
        st_ref[...] = jnp.zeros_like(st_ref)

    reverse = direction == 1
    mask = _tri(reverse)
    gc = gc_ref[0] + bc_ref[...]
    gr = gr_ref[0] + br_ref[...]
    d8 = 16 + direction * B_HEADS
    dt_col = _softplus(gc[:, d8:d8 + B_HEADS])
    dt_row = _softplus(gr[d8:d8 + B_HEADS, :])
    la_col = -dt_col * jnp.exp(ac_ref[:, d8:d8 + B_HEADS])
    la_row = -dt_row * jnp.exp(ar_ref[d8:d8 + B_HEADS, :])
    b_col, b_row = _cumsums(la_col, la_row, mask)
    tot = jnp.sum(la_col, axis=0, keepdims=True)
    outs = []
    hpg = B_HEADS // B_GROUPS
    for g in range(B_GROUPS):
        bm = xbc_ref[0, :, B_WIDTH + g * B_STATE:B_WIDTH + (g + 1) * B_STATE]
        cm = xbc_ref[0, :, B_WIDTH + (B_GROUPS + g) * B_STATE:B_WIDTH + (B_GROUPS + g + 1) * B_STATE]
        cb = lax.dot_general(cm, bm, (((1,), (1,)), ((), ())), preferred_element_type=F32)
        for h in range(g * hpg, (g + 1) * hpg):
            xh = xbc_ref[0, :, h * B_HEADDIM:(h + 1) * B_HEADDIM]
            bc, br = b_col[:, h:h + 1], b_row[h:h + 1, :]
            th = tot[:, h:h + 1]
            decay = jnp.exp(jnp.where(mask, bc - br, -jnp.inf))
            s = cb * decay * dt_row[h:h + 1, :]
            y = jnp.dot(s, xh, preferred_element_type=F32) \
                + jnp.exp(bc) * jnp.dot(cm, st_ref[h], preferred_element_type=F32)
            outs.append(y)
            w_end = jnp.exp(th - bc) * dt_col[:, h:h + 1]
            st_ref[h] = jnp.exp(th) * st_ref[h] + lax.dot_general(
                bm * w_end, xh, (((0,), (0,)), ((), ())), preferred_element_type=F32)
    o_ref[0] = jnp.concatenate(outs, axis=1)


def _ssd_scan(xbc, gates, gates_row, bias_col, bias_row, alog_col, alog_row, nc_ctx, direction):
    bsz, n, width = xbc.shape
    nc = n // CHUNK
    reverse = direction == 1
    order = lambda c: _chunk_order(c, nc_ctx, nc, reverse)
    return pl.pallas_call(
        functools.partial(_ssd_kernel, direction=direction),
        grid=(bsz, nc),
        in_specs=[pl.BlockSpec((1, CHUNK, width), lambda b, c: (b, order(c), 0)),
                  pl.BlockSpec((1, CHUNK, LANES), lambda b, c: (b, order(c), 0)),
                  pl.BlockSpec((1, 32, CHUNK), lambda b, c: (b, 0, order(c))),
                  pl.BlockSpec((1, LANES), lambda b, c: (0, 0)),
                  pl.BlockSpec((32, 1), lambda b, c: (0, 0)),
                  pl.BlockSpec((1, LANES), lambda b, c: (0, 0)),
                  pl.BlockSpec((32, 1), lambda b, c: (0, 0))],
        out_specs=pl.BlockSpec((1, CHUNK, B_WIDTH), lambda b, c: (b, order(c), 0)),
        out_shape=jax.ShapeDtypeStruct((bsz, n, B_WIDTH), F32),
        scratch_shapes=[pltpu.VMEM((B_HEADS, B_STATE, B_HEADDIM), F32)],
        compiler_params=_cparams(("parallel", "arbitrary")),
        name="ssd_scan",
    )(xbc, gates, gates_row, bias_col, bias_row, alog_col, alog_row)


def _finish0_kernel(hf_ref, hb_ref, yf_ref, yb_ref, o_ref_in, z_ref, xs_ref, x_ref, mod_ref,
                    mn_ref, sn_ref, dsk_ref, w_ref, out_ref):
    h = hf_ref[0] + hb_ref[0]
    parts = []
    for hd in range(A_HEADS):
        hh = h[:, hd * A_DV:(hd + 1) * A_DV]
        ms = jnp.mean(hh * hh, axis=-1, keepdims=True)
        parts.append(hh * lax.rsqrt(ms + EPS))
    hn = jnp.concatenate(parts, axis=1) * mn_ref[...]
    ya = _sigmoid(o_ref_in[0]) * hn
    y = yf_ref[0] + yb_ref[0] + dsk_ref[...] * xs_ref[0]
    t = y * _silu(z_ref[0])
    ms = jnp.mean(t * t, axis=-1, keepdims=True)
    yb = t * lax.rsqrt(ms + EPS) * sn_ref[...]
    f = jnp.concatenate([ya, yb], axis=1)
    g1 = mod_ref[0, :, 2 * D_MODEL:3 * D_MODEL]
    out_ref[0] = x_ref[0] + g1 * jnp.dot(f.astype(BF16), w_ref[...], preferred_element_type=F32)


def _finish0(hf, hb, yf, yb, qkvo, zx, xbc_act, xs, modsel, mnorm, snorm, dskip, w_out, ctx_tiles):
    bsz, n, d = xs.shape
    nt = n // ROWS
    half = lambda j: pl.BlockSpec((1, ROWS, 512), lambda b, i, j=j: (b, i, j))
    vec = pl.BlockSpec((1, 512), lambda b, i: (0, 0))
    return pl.pallas_call(
        _finish0_kernel,
        grid=(bsz, nt),
        in_specs=[half(0), half(0), half(0), half(0), half(2), half(2), half(0),
                  pl.BlockSpec((1, ROWS, d), lambda b, i: (b, i, 0)),
                  _mod_spec(ctx_tiles), vec, vec, vec,
                  pl.BlockSpec((d, d), lambda b, i: (0, 0))],
        out_specs=pl.BlockSpec((1, ROWS, d), lambda b, i: (b, i, 0)),
        out_shape=jax.ShapeDtypeStruct((bsz, n, d), F32),
        compiler_params=_cparams(("parallel", "arbitrary")),
        name="finish_mlstm_ssd",
    )(hf, hb, yf, yb, qkvo, zx, xbc_act, xs, modsel, mnorm, snorm, dskip, w_out)


def _head_ones(n, hd):
    r = lax.broadcasted_iota(jnp.int32, (n, n), 0) // hd
    c = lax.broadcasted_iota(jnp.int32, (n, n), 1) // hd
    return (r == c).astype(F32)


def _rwkv_prep_kernel(pr_ref, p_ref, n_ref, mu_ref, w0_ref, wup_ref, a0_ref, aup_ref, gup_ref,
                      kk_ref, ka_ref, r_o, v_o, kk_o, g_o, w0_o, w1_o, kt0_o, kt1_o, al0_o, al1_o,
                      *, ctx_tiles, ntiles):
    i = pl.program_id(1)
    pr = pr_ref[0]
    rows, cols = pr.shape
    q = cols // 4
    row = lax.broadcasted_iota(jnp.int32, (rows, 1), 0)
    col = lax.broadcasted_iota(jnp.int32, (1, cols), 1)
    is_ctx = i < ctx_tiles
    l1 = pltpu.roll(pr, 1, 0)
    r1 = pltpu.roll(pr, rows - 1, 0)
    left = jnp.where(row % GRID_W == 0, 0.0, l1)
    right = jnp.where(row % GRID_W == GRID_W - 1, 0.0, r1)
    up = jnp.concatenate([p_ref[0], pr[:rows - GRID_W]], axis=0)
    up = jnp.where(jnp.logical_and(i == ctx_tiles, row < GRID_W), 0.0, up)
    down = jnp.concatenate([pr[GRID_W:], n_ref[0]], axis=0)
    down = jnp.where(jnp.logical_and(i == ntiles - 1, row >= rows - GRID_W), 0.0, down)
    grid_sh = jnp.where(col < q, left, jnp.where(col < 2 * q, right, jnp.where(col < 3 * q, up, down)))
    prev = jnp.where(row == 0, 0.0, l1)
    nxt = jnp.where(row == rows - 1, 0.0, r1)
    seq_sh = jnp.where(col < 2 * q, prev, nxt)
    shifted = jnp.where(is_ctx, seq_sh, grid_sh)
    pr = pr + mu_ref[...] * (shifted - pr)
    r = pr[:, 0:C_WIDTH]
    k = pr[:, C_WIDTH:2 * C_WIDTH]
    v = pr[:, 2 * C_WIDTH:3 * C_WIDTH]
    o = 3 * C_WIDTH
    wd = pr[:, o:o + C_LORA_W]
    ad = pr[:, o + C_LORA_W:o + C_LORA_W + C_LORA_A]
    gd = pr[:, o + C_LORA_W + C_LORA_A:]
    kk = k * kk_ref[...]
    ss = jnp.dot(kk * kk, _head_ones(C_WIDTH, C_HEADDIM), precision=HIGHEST, preferred_element_type=F32)
    kk = kk * lax.rsqrt(ss + 1e-12)
    tw = jnp.tanh(wd)
    for d, (w_o, kt_o, al_o) in enumerate(((w0_o, kt0_o, al0_o), (w1_o, kt1_o, al1_o))):
        logw = -RWKV_W_SCALE * _sigmoid(w0_ref[d:d + 1, :] + jnp.dot(tw, wup_ref[d], preferred_element_type=F32))
        a = _sigmoid(a0_ref[d:d + 1, :] + jnp.dot(ad, aup_ref[d], preferred_element_type=F32))
        w_o[0] = jnp.exp(logw)
        kt_o[0] = k * (1.0 + (a - 1.0) * ka_ref[...])
        al_o[0] = a
    r_o[0] = r
    v_o[0] = v
    kk_o[0] = kk
    g_o[0] = jnp.dot(_sigmoid(gd), gup_ref[...], preferred_element_type=F32)


def _rwkv_prep(pr, mu, w0, w_up, a0, a_up, g_up, k_k, k_a, ctx_tiles):
    bsz, n, cols = pr.shape
    nt = n // ROWS
    hb = ROWS // GRID_W
    nh = n // GRID_W
    full = lambda a: pl.BlockSpec(a.shape, lambda b, i, nd=a.ndim: (0,) * nd)
    out = pl.BlockSpec((1, ROWS, C_WIDTH), lambda b, i: (b, i, 0))
    return pl.pallas_call(
        functools.partial(_rwkv_prep_kernel, ctx_tiles=ctx_tiles, ntiles=nt),
        grid=(bsz, nt),
        in_specs=[pl.BlockSpec((1, ROWS, cols), lambda b, i: (b, i, 0)),
                  pl.BlockSpec((1, GRID_W, cols), lambda b, i: (b, jnp.maximum(i * hb - 1, 0), 0)),
                  pl.BlockSpec((1, GRID_W, cols), lambda b, i: (b, jnp.minimum((i + 1) * hb, nh - 1), 0)),
                  full(mu), full(w0), full(w_up), full(a0), full(a_up), full(g_up), full(k_k), full(k_a)],
        out_specs=[out] * 10,
        out_shape=[jax.ShapeDtypeStruct((bsz, n, C_WIDTH), F32)] * 10,
        compiler_params=_cparams(("parallel", "arbitrary")),
        name="rwkv_prep",
    )(pr, pr, pr, mu, w0, w_up, a0, a_up, g_up, k_k, k_a)


RWKV_UNROLL = 4


def _rwkv_kernel(rf, wf, kf, vf, kkf, alf, rb, wb, kb, vb, kkb, alb, of_ref, ob_ref, s_ref):
    c = pl.program_id(0)

    @pl.when(c == 0)
    def _():
        s_ref[...] = jnp.zeros_like(s_ref)

    nb = rf.shape[0]
    t_len = rf.shape[1]
    nt = C_WIDTH // LANES
    sub = lax.broadcasted_iota(jnp.int32, (C_HEADDIM, C_WIDTH), 0)
    lane = lax.broadcasted_iota(jnp.int32, (C_HEADDIM, C_WIDTH), 1)
    diag = (lane % C_HEADDIM == sub).astype(F32)
    ones = _head_ones(LANES, C_HEADDIM).astype(BF16)
    ones2 = jnp.concatenate([ones, ones], axis=0)
    chains = [(refs, b) for refs in ((rf, wf, kf, vf, kkf, alf, of_ref, False),
                                     (rb, wb, kb, vb, kkb, alb, ob_ref, True)) for b in range(nb)]

    def head_sum(xs, split):
        x = jnp.concatenate([a[:, j * LANES:(j + 1) * LANES] for a in xs for j in range(nt)], axis=0)
        if split:
            hi = x.astype(BF16)
            lo = (x - hi.astype(F32)).astype(BF16)
            out = jnp.dot(jnp.concatenate([hi, lo], axis=1), ones2, preferred_element_type=F32)
        else:
            out = jnp.dot(x.astype(BF16), ones, preferred_element_type=F32)
        res = []
        for n in range(len(xs)):
            res.append(jnp.concatenate(
                [out[(nt * n + j) * C_HEADDIM:(nt * n + j + 1) * C_HEADDIM] for j in range(nt)], axis=1))
        return res

    def step(i, states):
        rows = []
        for refs, b in chains:
            t = (t_len - 1 - i) if refs[7] else i
            rows.append([ref[b, pl.ds(t, 1), :] for ref in refs[:6]] + [t])
        sa = [head_sum([s * (-row[4])], True)[0] for s, row in zip(states, rows)]
        vcol = [head_sum([diag * row[3]], False)[0] for row in rows]
        new = [s * row[1] + a * (row[4] * row[5]) + vc * row[2]
               for s, row, a, vc in zip(states, rows, sa, vcol)]
        ys = [head_sum([s * row[0]], False)[0] for s, row in zip(new, rows)]
        for (refs, b), row, y in zip(chains, rows, ys):
            refs[6][b, pl.ds(row[6], 1), :] = jnp.sum(y * diag, axis=0, keepdims=True)
        return new

    def body(g, states):
        states = list(states)
        for u in range(RWKV_UNROLL):
            states = step(g * RWKV_UNROLL + u, states)
        return tuple(states)

    states = lax.fori_loop(0, t_len // RWKV_UNROLL, body,
                           tuple(s_ref[n] for n in range(len(chains))))
    for n in range(len(chains)):
        s_ref[n] = states[n]


def _rwkv_scan(r, v, kk, w_f, kt_f, al_f, w_b, kt_b, al_b, nc_ctx):
    bsz, n, width = r.shape
    nc = n // CHUNK
    fwd = pl.BlockSpec((bsz, CHUNK, width), lambda c: (0, c, 0))
    bwd = pl.BlockSpec((bsz, CHUNK, width), lambda c: (0, _chunk_order(c, nc_ctx, nc, True), 0))
    return pl.pallas_call(
        _rwkv_kernel,
        grid=(nc,),
        in_specs=[fwd] * 6 + [bwd] * 6,
        out_specs=[fwd, bwd],
        out_shape=[jax.ShapeDtypeStruct((bsz, n, width), F32)] * 2,
        scratch_shapes=[pltpu.VMEM((2 * bsz, C_HEADDIM, width), F32)],
        compiler_params=_cparams(("arbitrary",)),
        name="rwkv_scan",
    )(r, w_f, kt_f, v, kk, al_f, r, w_b, kt_b, v, kk, al_b)


def _lru_gate_kernel(xc_ref, w_ref, b_ref, lam_ref, a0_o, b0_o, a1_o, b1_o):
    xc = xc_ref[0]
    z = jnp.dot(xc, w_ref[...], preferred_element_type=F32) + b_ref[...]
    for d, (a_o, b_o) in enumerate(((a0_o, b0_o), (a1_o, b1_o))):
        gr = _sigmoid(z[:, 2 * d * D_WIDTH:(2 * d + 1) * D_WIDTH])
        gi = _sigmoid(z[:, (2 * d + 1) * D_WIDTH:(2 * d + 2) * D_WIDTH])
        log_a = -LRU_C * gr * _softplus(-lam_ref[d:d + 1, :])
        th = jnp.tanh(log_a)
        one_minus_a2 = -2.0 * th / (1.0 - th)
        a_o[0] = jnp.exp(log_a)
        b_o[0] = jnp.sqrt(one_minus_a2) * (gi * xc)


def _lru_gates(xc, w, b, lam):
    bsz, n, width = xc.shape
    nt = n // ROWS
    out = pl.BlockSpec((1, ROWS, width), lambda b_, i: (b_, i, 0))
    return pl.pallas_call(
        _lru_gate_kernel,
        grid=(bsz, nt),
        in_specs=[out,
                  pl.BlockSpec(w.shape, lambda b_, i: (0, 0)),
                  pl.BlockSpec(b.shape, lambda b_, i: (0, 0)),
                  pl.BlockSpec(lam.shape, lambda b_, i: (0, 0))],
        out_specs=[out] * 4,
        out_shape=[jax.ShapeDtypeStruct((bsz, n, width), F32)] * 4,
        compiler_params=_cparams(("parallel", "arbitrary")),
        name="lru_gates",
    )(xc, w, b, lam)


def _lru_kernel(a_ref, b_ref, o_ref, h_ref, *, reverse):
    c = pl.program_id(0)

    @pl.when(c == 0)
    def _():
        h_ref[...] = jnp.zeros_like(h_ref)

    nb = a_ref.shape[0]
    t_len = a_ref.shape[1]

    def step(i, hs):
        t = (t_len - 1 - i) if reverse else i
        new = []
        for b in range(nb):
            h = a_ref[b, pl.ds(t, 1), :] * hs[b] + b_ref[b, pl.ds(t, 1), :]
            o_ref[b, pl.ds(t, 1), :] = h
            new.append(h)
        return tuple(new)

    hs = lax.fori_loop(0, t_len, step, tuple(h_ref[b] for b in range(nb)))
    for b in range(nb):
        h_ref[b] = hs[b]


def _lru_scan(a, b, nc_ctx, reverse):
    bsz, n, width = a.shape
    nc = n // CHUNK
    order = lambda c: _chunk_order(c, nc_ctx, nc, reverse)
    spec = pl.BlockSpec((bsz, CHUNK, width), lambda c: (0, order(c), 0))
    return pl.pallas_call(
        functools.partial(_lru_kernel, reverse=reverse),
        grid=(nc,),
        in_specs=[spec, spec],
        out_specs=spec,
        out_shape=jax.ShapeDtypeStruct((bsz, n, width), F32),
        scratch_shapes=[pltpu.VMEM((bsz, 1, width), F32)],
        compiler_params=_cparams(("arbitrary",)),
        name="lru_scan",
    )(a, b)


def _finish1_kernel(yf_ref, yb_ref, uf_ref, ub_ref, r_ref, kt0_ref, kt1_ref, v_ref, g_ref, gb_ref,
                    x_ref, mod_ref, rk_ref, lw_ref, lb_ref, w_ref, out_ref):
    y = yf_ref[0] + yb_ref[0]
    hs = _head_ones(C_WIDTH, C_HEADDIM)
    mean = jnp.dot(y, hs, precision=HIGHEST, preferred_element_type=F32) * (1.0 / C_HEADDIM)
    yc = y - mean
    var = jnp.dot(yc * yc, hs, precision=HIGHEST, preferred_element_type=F32) * (1.0 / C_HEADDIM)
    yn = yc * lax.rsqrt(var + RWKV_GN_EPS) * lw_ref[...] + lb_ref[...]
    kb = 0.5 * (kt0_ref[0] + kt1_ref[0])
    bonus = jnp.dot(r_ref[0] * kb * rk_ref[...], hs, precision=HIGHEST, preferred_element_type=F32)
    yn = yn + bonus * v_ref[0]
    yc_ = yn * g_ref[0]
    yd = (uf_ref[0] + ub_ref[0]) * _gelu(gb_ref[0])
    f = jnp.concatenate([yc_, yd], axis=1)
    g1 = mod_ref[0, :, 2 * D_MODEL:3 * D_MODEL]
    out_ref[0] = x_ref[0] + g1 * jnp.dot(f.astype(BF16), w_ref[...], preferred_element_type=F32)


def _finish1(yf, yb, uf, ub, r, kt0, kt1, v, g, gate_br, xs, modsel, r_k, ln_w, ln_b, w_out, ctx_tiles):
    bsz, n, d = xs.shape
    nt = n // ROWS
    half = pl.BlockSpec((1, ROWS, 512), lambda b, i: (b, i, 0))
    vec = pl.BlockSpec((1, 512), lambda b, i: (0, 0))
    return pl.pallas_call(
        _finish1_kernel,
        grid=(bsz, nt),
        in_specs=[half] * 10 + [pl.BlockSpec((1, ROWS, d), lambda b, i: (b, i, 0)),
                                _mod_spec(ctx_tiles), vec, vec, vec,
                                pl.BlockSpec((d, d), lambda b, i: (0, 0))],
        out_specs=pl.BlockSpec((1, ROWS, d), lambda b, i: (b, i, 0)),
        out_shape=jax.ShapeDtypeStruct((bsz, n, d), F32),
        compiler_params=_cparams(("parallel", "arbitrary")),
        name="finish_rwkv_lru",
    )(yf, yb, uf, ub, r, kt0, kt1, v, g, gate_br, xs, modsel, r_k, ln_w, ln_b, w_out)


def _max_arg(s):
    tiles = s.shape[0] // SUBLANES
    vals = [s[i * SUBLANES:(i + 1) * SUBLANES] for i in range(tiles)]
    row = lax.broadcasted_iota(jnp.int32, (SUBLANES, s.shape[1]), 0)
    ids = [row + i * SUBLANES for i in range(tiles)]
    while len(vals) > 1:
        nv, ni = [], []
        for a in range(0, len(vals) - 1, 2):
            take = vals[a + 1] > vals[a]
            nv.append(jnp.where(take, vals[a + 1], vals[a]))
            ni.append(jnp.where(take, ids[a + 1], ids[a]))
        if len(vals) % 2:
            nv.append(vals[-1])
            ni.append(ids[-1])
        vals, ids = nv, ni
    mx = jnp.max(vals[0], axis=0, keepdims=True)
    am = jnp.min(jnp.where(vals[0] == mx, ids[0], s.shape[0]), axis=0, keepdims=True)
    return mx, am


def _topk_rows(s, payload=None):
    iota = lax.broadcasted_iota(jnp.int32, s.shape, 0)
    vals, idxs = [], []
    for _ in range(PEER_TOPK):
        mx, am = _max_arg(s)
        sel = iota == am
        vals.append(mx)
        if payload is None:
            idxs.append(am)
        else:
            idxs.append(jnp.max(jnp.where(sel, payload, -1), axis=0, keepdims=True))
        s = jnp.where(sel, -jnp.inf, s)
    return vals, idxs


def _peer_candidates(sv0, si0, sv1, si1):
    grp = SUBLANES
    v1_all, i1_all = jnp.concatenate(sv1, axis=0), jnp.concatenate(si1, axis=0)
    v1_lo, i1_lo = v1_all[:grp], i1_all[:grp]
    row = lax.broadcasted_iota(jnp.int32, v1_lo.shape, 0)
    cand, cidx = [sv0[0] + v1_all], [si0[0] * N_KEYS + i1_all]
    for a in range(1, grp):
        cand.append(jnp.where(row < PEER_TOPK // (a + 1), sv0[a] + v1_lo, -jnp.inf))
        cidx.append(si0[a] * N_KEYS + i1_lo)
    cand.append(jnp.concatenate(sv0[grp:], axis=0) + sv1[0])
    cidx.append(jnp.concatenate(si0[grp:], axis=0) * N_KEYS + si1[0])
    return jnp.concatenate(cand, axis=0), jnp.concatenate(cidx, axis=0)


def _peer_select_kernel(x_ref, nw_ref, mod_ref, wq_ref, keys_ref, eid_ref, gate_ref):
    h = _norm_mod(x_ref[0], nw_ref[...], mod_ref, 3)
    q = jnp.dot(h.astype(BF16), wq_ref[...], preferred_element_type=F32)
    half = PEER_DK // 2
    for hd in range(PEER_HEADS):
        sv, si = [], []
        for p in range(2):
            hp = 2 * hd + p
            s = lax.dot_general(keys_ref[hp], q[:, hp * half:(hp + 1) * half],
                                (((1,), (1,)), ((), ())), preferred_element_type=F32)
            v, ix = _topk_rows(s)
            sv.append(v)
            si.append(ix)
        cand, cidx = _peer_candidates(sv[0], si[0], sv[1], si[1])
        best, eid = _topk_rows(cand, cidx)
        bestm = jnp.concatenate(best, axis=0)
        e = jnp.exp(bestm - best[0])
        gate = e / jnp.sum(e, axis=0, keepdims=True)
        eid_ref[0, hd * PEER_TOPK:(hd + 1) * PEER_TOPK, :] = jnp.concatenate(eid, axis=0)
        gate_ref[0, hd * PEER_TOPK:(hd + 1) * PEER_TOPK, :] = gate


def _peer_select(xs, nw, modsel, wq, keys, ctx_tiles):
    bsz, n, d = xs.shape
    nt = n // ROWS
    out = pl.BlockSpec((1, PEER_SLOTS, ROWS), lambda b, i: (b * nt + i, 0, 0))
    return pl.pallas_call(
        _peer_select_kernel,
        grid=(bsz, nt),
        in_specs=[pl.BlockSpec((1, ROWS, d), lambda b, i: (b, i, 0)),
                  pl.BlockSpec((1, d), lambda b, i: (0, 0)),
                  _mod_spec(ctx_tiles),
                  pl.BlockSpec(wq.shape, lambda b, i: (0, 0)),
                  pl.BlockSpec(keys.shape, lambda b, i: (0, 0, 0))],
        out_specs=[out, out],
        out_shape=[jax.ShapeDtypeStruct((bsz * nt, PEER_SLOTS, ROWS), jnp.int32),
                   jax.ShapeDtypeStruct((bsz * nt, PEER_SLOTS, ROWS), F32)],
        compiler_params=_cparams(("parallel", "arbitrary")),
        name="peer_select",
    )(xs, nw, modsel, wq, keys)


def _peer_gather_kernel(eid_hbm, x_ref, nw_ref, mod_ref, gate_ref, uv_hbm, uvw_hbm, o_ref, hbuf, *scratch, nblk):
    bufs = scratch[:PEER_NBUF]
    idx_smem, gsem, isem = scratch[PEER_NBUF:]
    i = pl.program_id(0)
    tb = x_ref.shape[0]
    ahead = PEER_NBUF - 1
    nidx = idx_smem.shape[0]
    cur = i % nidx
    nxt = (i + 1) % nidx

    def idx_copy(blk):
        slot = blk % nidx
        return pltpu.make_async_copy(eid_hbm.at[blk], idx_smem.at[slot], isem.at[slot])

    @pl.when(i == 0)
    def _():
        idx_copy(0).start()
        if nblk > 1:
            idx_copy(1).start()
        idx_copy(0).wait()

    @pl.when(i + 2 < nblk)
    def _():
        idx_copy(i + 2).start()

    @pl.when(i + 1 < nblk)
    def _():
        idx_copy(i + 1).wait()

    hbuf[...] = _norm_mod(x_ref[...], nw_ref[...], mod_ref, 3)
    g2 = mod_ref[0, :, 5 * D_MODEL:6 * D_MODEL]
    lane = lax.broadcasted_iota(jnp.int32, (PEER_SLOTS, tb), 1)

    def issue(islot, t, n, lo, hi):
        for j in range(lo, hi):
            e = idx_smem[islot, t, j]
            pltpu.make_async_copy(uv_hbm.at[e], bufs[n].at[:, pl.ds(j * ROW_TILES, ROW_TILES), :],
                                  gsem.at[n]).start()

    def consume(t, n, issue_part):
        pltpu.make_async_copy(uvw_hbm.at[pl.ds(0, 2)], bufs[n], gsem.at[n]).wait()
        xrow = hbuf[pl.ds(t, 1), :]
        acc = jnp.zeros((PEER_SLOTS, LANES), F32)
        for s in range(ROW_TILES):
            issue_part(s)
            us = bufs[n][0, pl.ds(s, PEER_SLOTS, stride=ROW_TILES), :]
            acc = acc + us * xrow[:, s * LANES:(s + 1) * LANES]
        act = jnp.sum(acc, axis=1, keepdims=True)
        gcol = jnp.sum(jnp.where(lane == t, gate_ref[0], 0.0), axis=1, keepdims=True)
        coef = jnp.broadcast_to(gcol * _gelu(act), (PEER_SLOTS, LANES))
        outs = []
        for s in range(ROW_TILES):
            issue_part(ROW_TILES + s)
            vs = bufs[n][1, pl.ds(s, PEER_SLOTS, stride=ROW_TILES), :]
            outs.append(jnp.sum(vs * coef, axis=0, keepdims=True))
        orow = jnp.concatenate(outs, axis=1)
        o_ref[pl.ds(t, 1), :] = x_ref[pl.ds(t, 1), :] + g2 * orow

    part = PEER_SLOTS // (2 * ROW_TILES)

    @pl.when(i == 0)
    def _():
        for t0 in range(ahead):
            issue(0, t0, t0, 0, PEER_SLOTS)

    def group(g, carry):
        for n in range(PEER_NBUF):
            t = g * PEER_NBUF + n
            consume(t, n, lambda k, t=t, n=n: issue(cur, t + ahead, (n + ahead) % PEER_NBUF,
                                                    k * part, (k + 1) * part))
        return carry

    ngroups = tb // PEER_NBUF
    lax.fori_loop(0, ngroups - 1, group, 0)
    for n in range(PEER_NBUF):
        t = (ngroups - 1) * PEER_NBUF + n
        if n == 0:
            consume(t, n, lambda k, t=t: issue(cur, t + ahead, ahead, k * part, (k + 1) * part))
        else:
            @pl.when(i + 1 < nblk)
            def _():
                issue(nxt, n - 1, n - 1, 0, PEER_SLOTS)
            consume(t, n, lambda k: None)


def _peer_gather(eid_t, gate, xs2, nw, modsel, uv, tiles_per_batch, ctx_tiles):
    ntok, d = xs2.shape
    nblk = ntok // PEER_TB
    slab = PEER_SLOTS * ROW_TILES
    assert PEER_TB % PEER_NBUF == 0
    uvw = uv.reshape(-1, slab, LANES)
    return pl.pallas_call(
        functools.partial(_peer_gather_kernel, nblk=nblk),
        grid=(nblk,),
        in_specs=[pl.BlockSpec(memory_space=pl.ANY),
                  pl.BlockSpec((PEER_TB, d), lambda i: (i, 0)),
                  pl.BlockSpec((1, d), lambda i: (0, 0)),
                  pl.BlockSpec((1, 1, 6 * D_MODEL),
                               lambda i: (2 * (i // tiles_per_batch)
                                          + jnp.where(i % tiles_per_batch >= ctx_tiles, 1, 0), 0, 0)),
                  pl.BlockSpec((1, PEER_SLOTS, PEER_TB), lambda i: (i, 0, 0)),
                  pl.BlockSpec(memory_space=pl.ANY),
                  pl.BlockSpec(memory_space=pl.ANY)],
        out_specs=pl.BlockSpec((PEER_TB, d), lambda i: (i, 0)),
        out_shape=jax.ShapeDtypeStruct((ntok, d), F32),
        scratch_shapes=[pltpu.VMEM((PEER_TB, d), F32),
                        *[pltpu.VMEM((2, slab, LANES), F32) for _ in range(PEER_NBUF)],
                        pltpu.SMEM((3, PEER_TB, PEER_SLOTS), jnp.int32),
                        pltpu.SemaphoreType.DMA((PEER_NBUF,)),
                        pltpu.SemaphoreType.DMA((3,))],
        compiler_params=_cparams(("arbitrary",)),
        name="peer_gather",
    )(eid_t, xs2, nw, modsel, gate, uv, uvw)


def _peer(xs, nw, modsel, wq, keys, u_tab, v_tab, ctx_tiles):
    bsz, n, d = xs.shape
    eid, gate = _peer_select(xs, nw, modsel, wq, keys, ctx_tiles)
    nblk = bsz * n // PEER_TB
    split = ROWS // PEER_TB
    eid_t = eid.reshape(-1, PEER_SLOTS, split, PEER_TB).transpose(0, 2, 3, 1).reshape(nblk, PEER_TB, PEER_SLOTS)
    gate_b = gate.reshape(-1, PEER_SLOTS, split, PEER_TB).transpose(0, 2, 1, 3).reshape(nblk, PEER_SLOTS, PEER_TB)
    ne = u_tab.shape[0]
    uv = jnp.stack([u_tab.reshape(ne, ROW_TILES, LANES), v_tab.reshape(ne, ROW_TILES, LANES)], axis=1)
    out = _peer_gather(eid_t, gate_b, xs.reshape(bsz * n, d), nw, modsel, uv,
                       n // PEER_TB, ctx_tiles * ROWS // PEER_TB)
    return out.reshape(bsz, n, d)


def _final_kernel(x_ref, w_ref, o_ref):
    x = x_ref[0]
    ms = jnp.mean(x * x, axis=-1, keepdims=True)
    o_ref[0] = x * lax.rsqrt(ms + EPS) * w_ref[...]


def _final_norm(xs, w, ctx_tiles, seq):
    bsz, n, d = xs.shape
    return pl.pallas_call(
        _final_kernel,
        grid=(bsz, seq // ROWS),
        in_specs=[pl.BlockSpec((1, ROWS, d), lambda b, i: (b, i + ctx_tiles, 0)),
                  pl.BlockSpec((1, d), lambda b, i: (0, 0))],
        out_specs=pl.BlockSpec((1, ROWS, d), lambda b, i: (b, i, 0)),
        out_shape=jax.ShapeDtypeStruct((bsz, seq, d), F32),
        compiler_params=_cparams(("parallel", "arbitrary")),
        name="final_norm",
    )(xs, w)


def _block_diag(w):
    nb, d, e = w.shape
    eye = jnp.eye(nb, dtype=w.dtype)
    return (eye[:, None, :, None] * w[:, :, None, :]).reshape(nb * d, nb * e)


def _mixer0(xs, modsel, norm1, w_in, w_out, i_bias, f_bias, mlstm_norm, conv_w, conv_b, dt_bias,
            a_log, d_skip, ssd_norm, ctx_tiles, nc_ctx):
    q0, k0, v0, o0, ig0, fg0, z0, xbc0, dt0, end = 0, 256, 512, 1024, 1536, 1544, 1552, 2064, 3088, 3104
    pad = jnp.zeros((D_MODEL, LANES - 32), w_in.dtype)
    w_cat = jnp.concatenate([w_in[:, q0:ig0], w_in[:, xbc0:dt0], w_in[:, z0:xbc0],
                             w_in[:, ig0:z0], w_in[:, dt0:end], pad], axis=1).astype(BF16)
    qkvo, zx, gates = _project(xs, norm1, modsel, w_cat, (1536, 1536, LANES), ctx_tiles)
    gates_row = jnp.swapaxes(gates[:, :, :32], 1, 2)
    bias = jnp.concatenate([i_bias.reshape(-1), f_bias.reshape(-1), dt_bias.reshape(-1)])
    bias_col = jnp.pad(bias, (0, LANES - 32)).reshape(1, LANES)
    bias_row = bias.reshape(32, 1)
    alog = jnp.concatenate([jnp.zeros((16,), F32), a_log.reshape(-1)])
    alog_col = jnp.pad(alog, (0, LANES - 32)).reshape(1, LANES)
    alog_row = alog.reshape(32, 1)
    xbc_act = _conv4(zx, 1024, conv_w, conv_b.reshape(1, -1), ctx_tiles, True)
    hf = _mlstm_scan(qkvo, gates, gates_row, bias_col, bias_row, nc_ctx, 0)
    hb = _mlstm_scan(qkvo, gates, gates_row, bias_col, bias_row, nc_ctx, 1)
    yf = _ssd_scan(xbc_act, gates, gates_row, bias_col, bias_row, alog_col, alog_row, nc_ctx, 0)
    yb = _ssd_scan(xbc_act, gates, gates_row, bias_col, bias_row, alog_col, alog_row, nc_ctx, 1)
    dskip = jnp.repeat(d_skip, B_HEADDIM).reshape(1, -1)
    return _finish0(hf, hb, yf, yb, qkvo, zx, xbc_act, xs, modsel, mlstm_norm.reshape(1, -1),
                    ssd_norm.reshape(1, -1), dskip, w_out.astype(BF16), ctx_tiles)


def _mixer1(xs, modsel, norm1, w_in, w_out, mu, w0, w_up, a0, a_up, g_up, k_k, k_a, r_k, ln_w, ln_b,
            conv_w, conv_b, lam, wa, ba, wi, bi, ctx_tiles, nc_ctx):
    pr, gate_br, x_br = _project(xs, norm1, modsel, w_in.astype(BF16), (RWKV_COLS, D_WIDTH, D_WIDTH),
                                 ctx_tiles)
    r, v, kk, g, w_f, w_b, kt_f, kt_b, al_f, al_b = _rwkv_prep(
        pr, mu.reshape(1, -1), w0, w_up, a0, a_up, g_up, k_k.reshape(1, -1), k_a.reshape(1, -1), ctx_tiles)
    yf, yb = _rwkv_scan(r, v, kk, w_f, kt_f, al_f, w_b, kt_b, al_b, nc_ctx)
    xc = _conv4(x_br, D_WIDTH, conv_w, conv_b.reshape(1, -1), ctx_tiles, False)
    w_gate = jnp.concatenate([_block_diag(wa[0]), _block_diag(wi[0]),
                              _block_diag(wa[1]), _block_diag(wi[1])], axis=1)
    b_gate = jnp.concatenate([ba[0], bi[0], ba[1], bi[1]]).reshape(1, -1)
    a_f, b_f, a_b, b_b = _lru_gates(xc, w_gate, b_gate, lam)
    uf = _lru_scan(a_f, b_f, nc_ctx, False)
    ub = _lru_scan(a_b, b_b, nc_ctx, True)
    return _finish1(yf, yb, uf, ub, r, kt_f, kt_b, v, g, gate_br, xs, modsel, r_k.reshape(1, -1),
                    ln_w.reshape(1, -1), ln_b.reshape(1, -1), w_out.astype(BF16), ctx_tiles)


def kernel(x, c, ctx, c_ctx, mod_w, mod_b, norm1, norm2, peer_wq, peer_keys, peer_u, peer_v, ev_w_in, ev_w_out, ev_mlstm_i_bias, ev_mlstm_f_bias, ev_mlstm_norm, ev_ssd_conv_w, ev_ssd_conv_b, ev_ssd_dt_bias, ev_ssd_a_log, ev_ssd_d, ev_ssd_norm, od_w_in, od_w_out, od_rwkv_mu, od_rwkv_w0, od_rwkv_w_up, od_rwkv_a0, od_rwkv_a_up, od_rwkv_g_up, od_rwkv_k_k, od_rwkv_k_a, od_rwkv_r_k, od_rwkv_ln_w, od_rwkv_ln_b, od_lru_conv_w, od_lru_conv_b, od_lru_lambda, od_lru_wa, od_lru_ba, od_lru_wi, od_lru_bi, final_norm):
    bsz, seq, d = x.shape
    ctx_len = ctx.shape[1]
    depth = mod_w.shape[0]
    assert d == D_MODEL and ctx_len == ROWS and seq % ROWS == 0 and bsz < SUBLANES
    ctx_tiles = ctx_len // ROWS
    nc_ctx = ctx_len // CHUNK
    xs = jnp.concatenate([ctx, x], axis=1)
    srows = jnp.concatenate([c, c_ctx[None, :], jnp.zeros((SUBLANES - bsz - 1, d), F32)], axis=0)
    for i in range(depth):
        mod = _modulation(srows, mod_w[i], mod_b[i].reshape(1, -1))
        modsel = jnp.stack([jnp.broadcast_to(mod[bsz], (bsz, 6 * d)), mod[:bsz]], axis=1)
        modsel = modsel.reshape(2 * bsz, 1, 6 * d)
        j = i // 2
        n1 = norm1[i].reshape(1, -1)
        if i % 2 == 0:
            xs = _mixer0(xs, modsel, n1, ev_w_in[j], ev_w_out[j], ev_mlstm_i_bias[j], ev_mlstm_f_bias[j],
                         ev_mlstm_norm[j], ev_ssd_conv_w[j], ev_ssd_conv_b[j], ev_ssd_dt_bias[j],
                         ev_ssd_a_log[j], ev_ssd_d[j], ev_ssd_norm[j], ctx_tiles, nc_ctx)
        else:
            xs = _mixer1(xs, modsel, n1, od_w_in[j], od_w_out[j], od_rwkv_mu[j], od_rwkv_w0[j],
                         od_rwkv_w_up[j], od_rwkv_a0[j], od_rwkv_a_up[j], od_rwkv_g_up[j], od_rwkv_k_k[j],
                         od_rwkv_k_a[j], od_rwkv_r_k[j].reshape(-1), od_rwkv_ln_w[j], od_rwkv_ln_b[j],
                         od_lru_conv_w[j], od_lru_conv_b[j], od_lru_lambda[j], od_lru_wa[j], od_lru_ba[j],
                         od_lru_wi[j], od_lru_bi[j], ctx_tiles, nc_ctx)
        keys = peer_keys[i].reshape(2 * PEER_HEADS, N_KEYS, PEER_DK // 2)
        xs = _peer(xs, norm2[i].reshape(1, -1), modsel, peer_wq[i].astype(BF16), keys,
                   peer_u[i], peer_v[i], ctx_tiles)
    return _final_norm(xs, final_norm.reshape(1, -1), ctx_tiles, seq)
```

```python
import functools
import math

import jax
import jax.numpy as jnp
from jax import lax
from jax.experimental import pallas as pl
from jax.experimental.pallas import tpu as pltpu

F32 = jnp.float32
BF16 = jnp.bfloat16
HIGHEST = lax.Precision.HIGHEST

D_MODEL = 1024
EPS = 1e-6
CHUNK = 128
ROWS = 256
GRID_W = 64
LANES = 128
SUBLANES = 8
MIB = 1024 * 1024

A_HEADS, A_DQK, A_DV = 4, 64, 128
A_WIDTH = A_HEADS * A_DV
B_HEADS, B_HEADDIM, B_GROUPS, B_STATE = 8, 64, 2, 128
B_WIDTH = B_HEADS * B_HEADDIM
C_HEADS, C_HEADDIM = 8, 64
C_WIDTH = C_HEADS * C_HEADDIM
C_LORA_W, C_LORA_A, C_LORA_G = 64, 64, 128
RWKV_COLS = 3 * C_WIDTH + C_LORA_W + C_LORA_A + C_LORA_G
RWKV_W_SCALE = math.exp(-0.5)
RWKV_GN_EPS = 64e-5
D_WIDTH = 512
LRU_C = 8.0
PEER_HEADS, PEER_DK, N_KEYS, PEER_TOPK = 8, 256, 128, 16
PEER_SLOTS = PEER_HEADS * PEER_TOPK
PEER_TB = 128
PEER_NBUF = 4
ROW_TILES = D_MODEL // LANES


def _cparams(sem, vmem_mib=48):
    return pltpu.CompilerParams(dimension_semantics=sem, vmem_limit_bytes=vmem_mib * MIB)


def _softplus(x):
    return jnp.maximum(x, 0.0) + jnp.log1p(jnp.exp(-jnp.abs(x)))


def _sigmoid(x):
    return 1.0 / (1.0 + jnp.exp(-x))


def _silu(x):
    return x * _sigmoid(x)


def _gelu(x):
    return 0.5 * x * (1.0 + lax.erf(x * (1.0 / math.sqrt(2.0))))


def _norm_mod(x, nw, mod_ref, slot):
    ms = jnp.mean(x * x, axis=-1, keepdims=True)
    y = x * lax.rsqrt(ms + EPS) * nw
    sh = mod_ref[0, :, slot * D_MODEL:(slot + 1) * D_MODEL]
    sc = mod_ref[0, :, (slot + 1) * D_MODEL:(slot + 2) * D_MODEL]
    return y * (1.0 + sc) + sh


def _chunk_order(c, nc_ctx, nc, reverse):
    if not reverse:
        return c
    return jnp.where(c < nc_ctx, nc_ctx - 1 - c, nc + nc_ctx - 1 - c)


def _mod_kernel(s_ref, w_ref, b_ref, o_ref):
    s = _silu(s_ref[...])
    o_ref[...] = jnp.dot(s, w_ref[...], precision=HIGHEST, preferred_element_type=F32) + b_ref[...]


def _modulation(srows, w, b):
    d = srows.shape[1]
    nt = w.shape[1] // d
    return pl.pallas_call(
        _mod_kernel,
        grid=(nt,),
        in_specs=[pl.BlockSpec((SUBLANES, d), lambda j: (0, 0)),
                  pl.BlockSpec((d, d), lambda j: (0, j)),
                  pl.BlockSpec((1, d), lambda j: (0, j))],
        out_specs=pl.BlockSpec((SUBLANES, d), lambda j: (0, j)),
        out_shape=jax.ShapeDtypeStruct((SUBLANES, w.shape[1]), F32),
        compiler_params=_cparams(("arbitrary",)),
        name="modulation",
    )(srows, w, b)


def _proj_kernel(x_ref, nw_ref, mod_ref, w_ref, *o_refs, widths):
    h = _norm_mod(x_ref[0], nw_ref[...], mod_ref, 0)
    out = jnp.dot(h.astype(BF16), w_ref[...], preferred_element_type=F32)
    off = 0
    for o_ref, wd in zip(o_refs, widths):
        o_ref[0] = out[:, off:off + wd]
        off += wd


def _mod_spec(ctx_tiles):
    return pl.BlockSpec((1, 1, 6 * D_MODEL),
                        lambda b, i: (2 * b + jnp.where(i >= ctx_tiles, 1, 0), 0, 0))


def _project(xs, nw, modsel, w, widths, ctx_tiles):
    bsz, n, d = xs.shape
    nt = n // ROWS
    return pl.pallas_call(
        functools.partial(_proj_kernel, widths=widths),
        grid=(bsz, nt),
        in_specs=[pl.BlockSpec((1, ROWS, d), lambda b, i: (b, i, 0)),
                  pl.BlockSpec((1, d), lambda b, i: (0, 0)),
                  _mod_spec(ctx_tiles),
                  pl.BlockSpec(w.shape, lambda b, i: (0, 0))],
        out_specs=[pl.BlockSpec((1, ROWS, wd), lambda b, i: (b, i, 0)) for wd in widths],
        out_shape=[jax.ShapeDtypeStruct((bsz, n, wd), F32) for wd in widths],
        compiler_params=_cparams(("parallel", "arbitrary")),
        name="norm_mod_project",
    )(xs, nw, modsel, w)


def _conv_kernel(x_ref, p_ref, n_ref, w_ref, b_ref, o_ref, *, ctx_tiles, ntiles, act):
    i = pl.program_id(1)
    x = x_ref[0]
    rows = x.shape[0]
    prev_ok = jnp.logical_and(i != 0, i != ctx_tiles)
    next_ok = jnp.logical_and(i != ctx_tiles - 1, i != ntiles - 1)
    p = jnp.where(prev_ok, p_ref[0], 0.0)
    nx = jnp.where(next_ok, n_ref[0], 0.0)
    row = lax.broadcasted_iota(jnp.int32, (rows, 1), 0)
    xm1 = jnp.where(row == 0, p[7:8], pltpu.roll(x, 1, 0))
    xm2 = jnp.where(row == 0, p[6:7], jnp.where(row == 1, p[7:8], pltpu.roll(x, 2, 0)))
    xp1 = jnp.where(row == rows - 1, nx[0:1], pltpu.roll(x, rows - 1, 0))
    w = w_ref[...]
    y = b_ref[...] + xm2 * w[0:1] + xm1 * w[1:2] + x * w[2:3] + xp1 * w[3:4]
    o_ref[0] = _silu(y) if act else y


def _conv4(x, width, w, b, ctx_tiles, act):
    bsz, n, _ = x.shape
    nt = n // ROWS
    hb = ROWS // SUBLANES
    nh = n // SUBLANES
    return pl.pallas_call(
        functools.partial(_conv_kernel, ctx_tiles=ctx_tiles, ntiles=nt, act=act),
        grid=(bsz, nt),
        in_specs=[pl.BlockSpec((1, ROWS, width), lambda b_, i: (b_, i, 0)),
                  pl.BlockSpec((1, SUBLANES, width), lambda b_, i: (b_, jnp.maximum(i * hb - 1, 0), 0)),
                  pl.BlockSpec((1, SUBLANES, width),
                               lambda b_, i: (b_, jnp.minimum((i + 1) * hb, nh - 1), 0)),
                  pl.BlockSpec((4, width), lambda b_, i: (0, 0)),
                  pl.BlockSpec((1, width), lambda b_, i: (0, 0))],
        out_specs=pl.BlockSpec((1, ROWS, width), lambda b_, i: (b_, i, 0)),
        out_shape=jax.ShapeDtypeStruct((bsz, n, width), F32),
        compiler_params=_cparams(("parallel", "arbitrary")),
        name="conv4",
    )(x, x, x, w, b)


def _tri(reverse):
    t = lax.broadcasted_iota(jnp.int32, (CHUNK, CHUNK), 0)
    s = lax.broadcasted_iota(jnp.int32, (CHUNK, CHUNK), 1)
    return (s >= t) if reverse else (s <= t)


def _cumsums(col, row, mask):
    mf = mask.astype(F32)
    b_col = jnp.dot(mf, col, precision=HIGHEST, preferred_element_type=F32)
    b_row = lax.dot_general(row, mf, (((1,), (1,)), ((), ())), precision=HIGHEST,
                            preferred_element_type=F32)
    return b_col, b_row


def _mlstm_kernel(qkv_ref, gc_ref, gr_ref, bc_ref, br_ref, o_ref, ct_ref, n_ref, m_ref, *,
                  direction):
    c = pl.program_id(1)

    @pl.when(c == 0)
    def _():
        ct_ref[...] = jnp.zeros_like(ct_ref)
        n_ref[...] = jnp.zeros_like(n_ref)
        m_ref[...] = jnp.zeros_like(m_ref)

    reverse = direction == 1
    mask = _tri(reverse)
    gc = gc_ref[0] + bc_ref[...]
    gr = gr_ref[0] + br_ref[...]
    d4 = direction * A_HEADS
    li_col = gc[:, d4:d4 + A_HEADS]
    lf_col = -_softplus(-gc[:, 8 + d4:8 + d4 + A_HEADS])
    li_row = gr[d4:d4 + A_HEADS, :]
    lf_row = -_softplus(-gr[8 + d4:8 + d4 + A_HEADS, :])
    b_col, b_row = _cumsums(lf_col, lf_row, mask)
    tot = jnp.sum(lf_col, axis=0, keepdims=True)
    outs = []
    for h in range(A_HEADS):
        q = qkv_ref[0, :, h * A_DQK:(h + 1) * A_DQK] * (A_DQK ** -0.5)
        k = qkv_ref[0, :, A_HEADS * A_DQK + h * A_DQK:A_HEADS * A_DQK + (h + 1) * A_DQK]
        v = qkv_ref[0, :, 2 * A_HEADS * A_DQK + h * A_DV:2 * A_HEADS * A_DQK + (h + 1) * A_DV]
        bc, br = b_col[:, h:h + 1], b_row[h:h + 1, :]
        ic, ir = li_col[:, h:h + 1], li_row[h:h + 1, :]
        th = tot[:, h:h + 1]
        m_prev = m_ref[h]
        dmat = jnp.where(mask, bc - br + ir, -jnp.inf)
        inter = bc + m_prev
        m_t = jnp.maximum(inter, jnp.max(dmat, axis=1, keepdims=True))
        qk = lax.dot_general(q, k, (((1,), (1,)), ((), ())), preferred_element_type=F32)
        s = qk * jnp.exp(dmat - m_t)
        w_inter = jnp.exp(inter - m_t)
        num = jnp.dot(s, v, preferred_element_type=F32) \
            + w_inter * jnp.dot(q, ct_ref[h], preferred_element_type=F32)
        den = jnp.sum(s, axis=1, keepdims=True) \
            + w_inter * jnp.sum(q * n_ref[h], axis=1, keepdims=True)
        outs.append(num / jnp.maximum(jnp.abs(den), jnp.exp(-m_t)))
        g = th - bc + ic
        m_new = jnp.maximum(th + m_prev, jnp.max(g, axis=0, keepdims=True))
        wg = jnp.exp(g - m_new)
        wc = jnp.exp(th + m_prev - m_new)
        ct_ref[h] = wc * ct_ref[h] + lax.dot_general(k * wg, v, (((0,), (0,)), ((), ())),
                                                     preferred_element_type=F32)
        n_ref[h] = wc * n_ref[h] + jnp.sum(wg * k, axis=0, keepdims=True)
        m_ref[h] = m_new
    o_ref[0] = jnp.concatenate(outs, axis=1)


def _mlstm_scan(qkvo, gates, gates_row, bias_col, bias_row, nc_ctx, direction):
    bsz, n, _ = qkvo.shape
    nc = n // CHUNK
    reverse = direction == 1
    order = lambda c: _chunk_order(c, nc_ctx, nc, reverse)
    return pl.pallas_call(
        functools.partial(_mlstm_kernel, direction=direction),
        grid=(bsz, nc),
        in_specs=[pl.BlockSpec((1, CHUNK, 2 * A_HEADS * A_DQK + A_WIDTH), lambda b, c: (b, order(c), 0)),
                  pl.BlockSpec((1, CHUNK, LANES), lambda b, c: (b, order(c), 0)),
                  pl.BlockSpec((1, 32, CHUNK), lambda b, c: (b, 0, order(c))),
                  pl.BlockSpec((1, LANES), lambda b, c: (0, 0)),
                  pl.BlockSpec((32, 1), lambda b, c: (0, 0))],
        out_specs=pl.BlockSpec((1, CHUNK, A_WIDTH), lambda b, c: (b, order(c), 0)),
        out_shape=jax.ShapeDtypeStruct((bsz, n, A_WIDTH), F32),
        scratch_shapes=[pltpu.VMEM((A_HEADS, A_DQK, A_DV), F32),
                        pltpu.VMEM((A_HEADS, 1, A_DQK), F32),
                        pltpu.VMEM((A_HEADS, 1, 1), F32)],
        compiler_params=_cparams(("parallel", "arbitrary")),
        name="mlstm_scan",
    )(qkvo, gates, gates_row, bias_col, bias_row)


def _ssd_kernel(xbc_ref, gc_ref, gr_ref, bc_ref, br_ref, ac_ref, ar_ref, o_ref, st_ref, *,
                direction):
    c = pl.program_id(1)

    @pl.when(c == 0)
    def _():
        st_ref[...] = jnp.zeros_like(st_ref)

    reverse = direction == 1
    mask = _tri(reverse)
    gc = gc_ref[0] + bc_ref[...]
    gr = gr_ref[0] + br_ref[...]
    d8 = 16 + direction * B_HEADS
    dt_col = _softplus(gc[:, d8:d8 + B_HEADS])
    dt_row = _softplus(gr[d8:d8 + B_HEADS, :])
    la_col = -dt_col * jnp.exp(ac_ref[:, d8:d8 + B_HEADS])
    la_row = -dt_row * jnp.exp(ar_ref[d8:d8 + B_HEADS, :])
    b_col, b_row = _cumsums(la_col, la_row, mask)
    tot = jnp.sum(la_col, axis=0, keepdims=True)
    outs = []
    hpg = B_HEADS // B_GROUPS
    for g in range(B_GROUPS):
        bm = xbc_ref[0, :, B_WIDTH + g * B_STATE:B_WIDTH + (g + 1) * B_STATE]
        cm = xbc_ref[0, :, B_WIDTH + (B_GROUPS + g) * B_STATE:B_WIDTH + (B_GROUPS + g + 1) * B_STATE]
        cb = lax.dot_general(cm, bm, (((1,), (1,)), ((), ())), preferred_element_type=F32)
        for h in range(g * hpg, (g + 1) * hpg):
            xh = xbc_ref[0, :, h * B_HEADDIM:(h + 1) * B_HEADDIM]
            bc, br = b_col[:, h:h + 1], b_row[h:h + 1, :]
            th = tot[:, h:h + 1]
            decay = jnp.exp(jnp.where(mask, bc - br, -jnp.inf))
            s = cb * decay * dt_row[h:h + 1, :]
            y = jnp.dot(s, xh, preferred_element_type=F32) \
                + jnp.exp(bc) * jnp.dot(cm, st_ref[h], preferred_element_type=F32)
            outs.append(y)
            w_end = jnp.exp(th - bc) * dt_col[:, h:h + 1]
            st_ref[h] = jnp.exp(th) * st_ref[h] + lax.dot_general(
                bm * w_end, xh, (((0,), (0,)), ((), ())), preferred_element_type=F32)
    o_ref[0] = jnp.concatenate(outs, axis=1)


def _ssd_scan(xbc, gates, gates_row, bias_col, bias_row, alog_col, alog_row, nc_ctx, direction):
    bsz, n, width = xbc.shape
    nc = n // CHUNK
    reverse = direction == 1
    order = lambda c: _chunk_order(c, nc_ctx, nc, reverse)
    return pl.pallas_call(
        functools.partial(_ssd_kernel, direction=direction),
        grid=(bsz, nc),
        in_specs=[pl.BlockSpec((1, CHUNK, width), lambda b, c: (b, order(c), 0)),
                  pl.BlockSpec((1, CHUNK, LANES), lambda b, c: (b, order(c), 0)),
                  pl.BlockSpec((1, 32, CHUNK), lambda b, c: (b, 0, order(c))),
                  pl.BlockSpec((1, LANES), lambda b, c: (0, 0)),
                  pl.BlockSpec((32, 1), lambda b, c: (0, 0)),
                  pl.BlockSpec((1, LANES), lambda b, c: (0, 0)),
                  pl.BlockSpec((32, 1), lambda b, c: (0, 0))],
        out_specs=pl.BlockSpec((1, CHUNK, B_WIDTH), lambda b, c: (b, order(c), 0)),
        out_shape=jax.ShapeDtypeStruct((bsz, n, B_WIDTH), F32),
        scratch_shapes=[pltpu.VMEM((B_HEADS, B_STATE, B_HEADDIM), F32)],
        compiler_params=_cparams(("parallel", "arbitrary")),
        name="ssd_scan",
    )(xbc, gates, gates_row, bias_col, bias_row, alog_col, alog_row)


def _finish0_kernel(hf_ref, hb_ref, yf_ref, yb_ref, o_ref_in, z_ref, xs_ref, x_ref, mod_ref,
                    mn_ref, sn_ref, dsk_ref, w_ref, out_ref):
    h = hf_ref[0] + hb_ref[0]
    parts = []
    for hd in range(A_HEADS):
        hh = h[:, hd * A_DV:(hd + 1) * A_DV]
        ms = jnp.mean(hh * hh, axis=-1, keepdims=True)
        parts.append(hh * lax.rsqrt(ms + EPS))
    hn = jnp.concatenate(parts, axis=1) * mn_ref[...]
    ya = _sigmoid(o_ref_in[0]) * hn
    y = yf_ref[0] + yb_ref[0] + dsk_ref[...] * xs_ref[0]
    t = y * _silu(z_ref[0])
    ms = jnp.mean(t * t, axis=-1, keepdims=True)
    yb = t * lax.rsqrt(ms + EPS) * sn_ref[...]
    f = jnp.concatenate([ya, yb], axis=1)
    g1 = mod_ref[0, :, 2 * D_MODEL:3 * D_MODEL]
    out_ref[0] = x_ref[0] + g1 * jnp.dot(f.astype(BF16), w_ref[...], preferred_element_type=F32)


def _finish0(hf, hb, yf, yb, qkvo, zx, xbc_act, xs, modsel, mnorm, snorm, dskip, w_out, ctx_tiles):
    bsz, n, d = xs.shape
    nt = n // ROWS
    half = lambda j: pl.BlockSpec((1, ROWS, 512), lambda b, i, j=j: (b, i, j))
    vec = pl.BlockSpec((1, 512), lambda b, i: (0, 0))
    return pl.pallas_call(
        _finish0_kernel,
        grid=(bsz, nt),
        in_specs=[half(0), half(0), half(0), half(0), half(2), half(2), half(0),
                  pl.BlockSpec((1, ROWS, d), lambda b, i: (b, i, 0)),
                  _mod_spec(ctx_tiles), vec, vec, vec,
                  pl.BlockSpec((d, d), lambda b, i: (0, 0))],
        out_specs=pl.BlockSpec((1, ROWS, d), lambda b, i: (b, i, 0)),
        out_shape=jax.ShapeDtypeStruct((bsz, n, d), F32),
        compiler_params=_cparams(("parallel", "arbitrary")),
        name="finish_mlstm_ssd",
    )(hf, hb, yf, yb, qkvo, zx, xbc_act, xs, modsel, mnorm, snorm, dskip, w_out)


def _head_ones(n, hd):
    r = lax.broadcasted_iota(jnp.int32, (n, n), 0) // hd
    c = lax.broadcasted_iota(jnp.int32, (n, n), 1) // hd
    return (r == c).astype(F32)


def _rwkv_prep_kernel(pr_ref, p_ref, n_ref, mu_ref, w0_ref, wup_ref, a0_ref, aup_ref, gup_ref,
                      kk_ref, ka_ref, r_o, v_o, kk_o, g_o, w0_o, w1_o, kt0_o, kt1_o, al0_o, al1_o,
                      *, ctx_tiles, ntiles):
    i = pl.program_id(1)
    pr = pr_ref[0]
    rows, cols = pr.shape
    q = cols // 4
    row = lax.broadcasted_iota(jnp.int32, (rows, 1), 0)
    col = lax.broadcasted_iota(jnp.int32, (1, cols), 1)
    is_ctx = i < ctx_tiles
    l1 = pltpu.roll(pr, 1, 0)
    r1 = pltpu.roll(pr, rows - 1, 0)
    left = jnp.where(row % GRID_W == 0, 0.0, l1)
    right = jnp.where(row % GRID_W == GRID_W - 1, 0.0, r1)
    up = jnp.concatenate([p_ref[0], pr[:rows - GRID_W]], axis=0)
    up = jnp.where(jnp.logical_and(i == ctx_tiles, row < GRID_W), 0.0, up)
    down = jnp.concatenate([pr[GRID_W:], n_ref[0]], axis=0)
    down = jnp.where(jnp.logical_and(i == ntiles - 1, row >= rows - GRID_W), 0.0, down)
    grid_sh = jnp.where(col < q, left, jnp.where(col < 2 * q, right, jnp.where(col < 3 * q, up, down)))
    prev = jnp.where(row == 0, 0.0, l1)
    nxt = jnp.where(row == rows - 1, 0.0, r1)
    seq_sh = jnp.where(col < 2 * q, prev, nxt)
    shifted = jnp.where(is_ctx, seq_sh, grid_sh)
    pr = pr + mu_ref[...] * (shifted - pr)
    r = pr[:, 0:C_WIDTH]
    k = pr[:, C_WIDTH:2 * C_WIDTH]
    v = pr[:, 2 * C_WIDTH:3 * C_WIDTH]
    o = 3 * C_WIDTH
    wd = pr[:, o:o + C_LORA_W]
    ad = pr[:, o + C_LORA_W:o + C_LORA_W + C_LORA_A]
    gd = pr[:, o + C_LORA_W + C_LORA_A:]
    kk = k * kk_ref[...]
    ss = jnp.dot(kk * kk, _head_ones(C_WIDTH, C_HEADDIM), precision=HIGHEST, preferred_element_type=F32)
    kk = kk * lax.rsqrt(ss + 1e-12)
    tw = jnp.tanh(wd)
    for d, (w_o, kt_o, al_o) in enumerate(((w0_o, kt0_o, al0_o), (w1_o, kt1_o, al1_o))):
        logw = -RWKV_W_SCALE * _sigmoid(w0_ref[d:d + 1, :] + jnp.dot(tw, wup_ref[d], preferred_element_type=F32))
        a = _sigmoid(a0_ref[d:d + 1, :] + jnp.dot(ad, aup_ref[d], preferred_element_type=F32))
        w_o[0] = jnp.exp(logw)
        kt_o[0] = k * (1.0 + (a - 1.0) * ka_ref[...])
        al_o[0] = a
    r_o[0] = r
    v_o[0] = v
    kk_o[0] = kk
    g_o[0] = jnp.dot(_sigmoid(gd), gup_ref[...], preferred_element_type=F32)


def _rwkv_prep(pr, mu, w0, w_up, a0, a_up, g_up, k_k, k_a, ctx_tiles):
    bsz, n, cols = pr.shape
    nt = n // ROWS
    hb = ROWS // GRID_W
    nh = n // GRID_W
    full = lambda a: pl.BlockSpec(a.shape, lambda b, i, nd=a.ndim: (0,) * nd)
    out = pl.BlockSpec((1, ROWS, C_WIDTH), lambda b, i: (b, i, 0))
    return pl.pallas_call(
        functools.partial(_rwkv_prep_kernel, ctx_tiles=ctx_tiles, ntiles=nt),
        grid=(bsz, nt),
        in_specs=[pl.BlockSpec((1, ROWS, cols), lambda b, i: (b, i, 0)),
                  pl.BlockSpec((1, GRID_W, cols), lambda b, i: (b, jnp.maximum(i * hb - 1, 0), 0)),
                  pl.BlockSpec((1, GRID_W, cols), lambda b, i: (b, jnp.minimum((i + 1) * hb, nh - 1), 0)),
                  full(mu), full(w0), full(w_up), full(a0), full(a_up), full(g_up), full(k_k), full(k_a)],
        out_specs=[out] * 10,
        out_shape=[jax.ShapeDtypeStruct((bsz, n, C_WIDTH), F32)] * 10,
        compiler_params=_cparams(("parallel", "arbitrary")),
        name="rwkv_prep",
    )(pr, pr, pr, mu, w0, w_up, a0, a_up, g_up, k_k, k_a)


RWKV_UNROLL = 4


def _rwkv_kernel(rf, wf, kf, vf, kkf, alf, rb, wb, kb, vb, kkb, alb, of_ref, ob_ref, s_ref):
    c = pl.program_id(0)

    @pl.when(c == 0)
    def _():
        s_ref[...] = jnp.zeros_like(s_ref)

    nb = rf.shape[0]
    t_len = rf.shape[1]
    nt = C_WIDTH // LANES
    sub = lax.broadcasted_iota(jnp.int32, (C_HEADDIM, C_WIDTH), 0)
    lane = lax.broadcasted_iota(jnp.int32, (C_HEADDIM, C_WIDTH), 1)
    diag = (lane % C_HEADDIM == sub).astype(F32)
    ones = _head_ones(LANES, C_HEADDIM).astype(BF16)
    ones2 = jnp.concatenate([ones, ones], axis=0)
    chains = [(refs, b) for refs in ((rf, wf, kf, vf, kkf, alf, of_ref, False),
                                     (rb, wb, kb, vb, kkb, alb, ob_ref, True)) for b in range(nb)]

    def head_sum(xs, split):
        x = jnp.concatenate([a[:, j * LANES:(j + 1) * LANES] for a in xs for j in range(nt)], axis=0)
        if split:
            hi = x.astype(BF16)
            lo = (x - hi.astype(F32)).astype(BF16)
            out = jnp.dot(jnp.concatenate([hi, lo], axis=1), ones2, preferred_element_type=F32)
        else:
            out = jnp.dot(x.astype(BF16), ones, preferred_element_type=F32)
        res = []
        for n in range(len(xs)):
            res.append(jnp.concatenate(
                [out[(nt * n + j) * C_HEADDIM:(nt * n + j + 1) * C_HEADDIM] for j in range(nt)], axis=1))
        return res

    def step(i, states):
        rows = []
        for refs, b in chains:
            t = (t_len - 1 - i) if refs[7] else i
            rows.append([ref[b, pl.ds(t, 1), :] for ref in refs[:6]] + [t])
        sa = [head_sum([s * (-row[4])], True)[0] for s, row in zip(states, rows)]
        vcol = [head_sum([diag * row[3]], False)[0] for row in rows]
        new = [s * row[1] + a * (row[4] * row[5]) + vc * row[2]
               for s, row, a, vc in zip(states, rows, sa, vcol)]
        ys = [head_sum([s * row[0]], False)[0] for s, row in zip(new, rows)]
        for (refs, b), row, y in zip(chains, rows, ys):
            refs[6][b, pl.ds(row[6], 1), :] = jnp.sum(y * diag, axis=0, keepdims=True)
        return new

    def body(g, states):
        states = list(states)
        for u in range(RWKV_UNROLL):
            states = step(g * RWKV_UNROLL + u, states)
        return tuple(states)

    states = lax.fori_loop(0, t_len // RWKV_UNROLL, body,
                           tuple(s_ref[n] for n in range(len(chains))))
    for n in range(len(chains)):
        s_ref[n] = states[n]


def _rwkv_scan(r, v, kk, w_f, kt_f, al_f, w_b, kt_b, al_b, nc_ctx):
    bsz, n, width = r.shape
    nc = n // CHUNK
    fwd = pl.BlockSpec((bsz, CHUNK, width), lambda c: (0, c, 0))
    bwd = pl.BlockSpec((bsz, CHUNK, width), lambda c: (0, _chunk_order(c, nc_ctx, nc, True), 0))
    return pl.pallas_call(
        _rwkv_kernel,
        grid=(nc,),
        in_specs=[fwd] * 6 + [bwd] * 6,
        out_specs=[fwd, bwd],
        out_shape=[jax.ShapeDtypeStruct((bsz, n, width), F32)] * 2,
        scratch_shapes=[pltpu.VMEM((2 * bsz, C_HEADDIM, width), F32)],
        compiler_params=_cparams(("arbitrary",)),
        name="rwkv_scan",
    )(r, w_f, kt_f, v, kk, al_f, r, w_b, kt_b, v, kk, al_b)


def _lru_gate_kernel(xc_ref, w_ref, b_ref, lam_ref, a0_o, b0_o, a1_o, b1_o):
    xc = xc_ref[0]
    z = jnp.dot(xc, w_ref[...], preferred_element_type=F32) + b_ref[...]
    for d, (a_o, b_o) in enumerate(((a0_o, b0_o), (a1_o, b1_o))):
        gr = _sigmoid(z[:, 2 * d * D_WIDTH:(2 * d + 1) * D_WIDTH])
        gi = _sigmoid(z[:, (2 * d + 1) * D_WIDTH:(2 * d + 2) * D_WIDTH])
        log_a = -LRU_C * gr * _softplus(-lam_ref[d:d + 1, :])
        th = jnp.tanh(log_a)
        one_minus_a2 = -2.0 * th / (1.0 - th)
        a_o[0] = jnp.exp(log_a)
        b_o[0] = jnp.sqrt(one_minus_a2) * (gi * xc)


def _lru_gates(xc, w, b, lam):
    bsz, n, width = xc.shape
    nt = n // ROWS
    out = pl.BlockSpec((1, ROWS, width), lambda b_, i: (b_, i, 0))
    return pl.pallas_call(
        _lru_gate_kernel,
        grid=(bsz, nt),
        in_specs=[out,
                  pl.BlockSpec(w.shape, lambda b_, i: (0, 0)),
                  pl.BlockSpec(b.shape, lambda b_, i: (0, 0)),
                  pl.BlockSpec(lam.shape, lambda b_, i: (0, 0))],
        out_specs=[out] * 4,
        out_shape=[jax.ShapeDtypeStruct((bsz, n, width), F32)] * 4,
        compiler_params=_cparams(("parallel", "arbitrary")),
        name="lru_gates",
    )(xc, w, b, lam)


def _lru_kernel(a_ref, b_ref, o_ref, h_ref, *, reverse):
    c = pl.program_id(0)

    @pl.when(c == 0)
    def _():
        h_ref[...] = jnp.zeros_like(h_ref)

    nb = a_ref.shape[0]
    t_len = a_ref.shape[1]

    def step(i, hs):
        t = (t_len - 1 - i) if reverse else i
        new = []
        for b in range(nb):
            h = a_ref[b, pl.ds(t, 1), :] * hs[b] + b_ref[b, pl.ds(t, 1), :]
            o_ref[b, pl.ds(t, 1), :] = h
            new.append(h)
        return tuple(new)

    hs = lax.fori_loop(0, t_len, step, tuple(h_ref[b] for b in range(nb)))
    for b in range(nb):
        h_ref[b] = hs[b]


def _lru_scan(a, b, nc_ctx, reverse):
    bsz, n, width = a.shape
    nc = n // CHUNK
    order = lambda c: _chunk_order(c, nc_ctx, nc, reverse)
    spec = pl.BlockSpec((bsz, CHUNK, width), lambda c: (0, order(c), 0))
    return pl.pallas_call(
        functools.partial(_lru_kernel, reverse=reverse),
        grid=(nc,),
        in_specs=[spec, spec],
        out_specs=spec,
        out_shape=jax.ShapeDtypeStruct((bsz, n, width), F32),
        scratch_shapes=[pltpu.VMEM((bsz, 1, width), F32)],
        compiler_params=_cparams(("arbitrary",)),
        name="lru_scan",
    )(a, b)


def _finish1_kernel(yf_ref, yb_ref, uf_ref, ub_ref, r_ref, kt0_ref, kt1_ref, v_ref, g_ref, gb_ref,
                    x_ref, mod_ref, rk_ref, lw_ref, lb_ref, w_ref, out_ref):
    y = yf_ref[0] + yb_ref[0]
    hs = _head_ones(C_WIDTH, C_HEADDIM)
    mean = jnp.dot(y, hs, precision=HIGHEST, preferred_element_type=F32) * (1.0 / C_HEADDIM)
    yc = y - mean
    var = jnp.dot(yc * yc, hs, precision=HIGHEST, preferred_element_type=F32) * (1.0 / C_HEADDIM)
    yn = yc * lax.rsqrt(var + RWKV_GN_EPS) * lw_ref[...] + lb_ref[...]
    kb = 0.5 * (kt0_ref[0] + kt1_ref[0])
    bonus = jnp.dot(r_ref[0] * kb * rk_ref[...], hs, precision=HIGHEST, preferred_element_type=F32)
    yn = yn + bonus * v_ref[0]
    yc_ = yn * g_ref[0]
    yd = (uf_ref[0] + ub_ref[0]) * _gelu(gb_ref[0])
    f = jnp.concatenate([yc_, yd], axis=1)
    g1 = mod_ref[0, :, 2 * D_MODEL:3 * D_MODEL]
    out_ref[0] = x_ref[0] + g1 * jnp.dot(f.astype(BF16), w_ref[...], preferred_element_type=F32)


def _finish1(yf, yb, uf, ub, r, kt0, kt1, v, g, gate_br, xs, modsel, r_k, ln_w, ln_b, w_out, ctx_tiles):
    bsz, n, d = xs.shape
    nt = n // ROWS
    half = pl.BlockSpec((1, ROWS, 512), lambda b, i: (b, i, 0))
    vec = pl.BlockSpec((1, 512), lambda b, i: (0, 0))
    return pl.pallas_call(
        _finish1_kernel,
        grid=(bsz, nt),
        in_specs=[half] * 10 + [pl.BlockSpec((1, ROWS, d), lambda b, i: (b, i, 0)),
                                _mod_spec(ctx_tiles), vec, vec, vec,
                                pl.BlockSpec((d, d), lambda b, i: (0, 0))],
        out_specs=pl.BlockSpec((1, ROWS, d), lambda b, i: (b, i, 0)),
        out_shape=jax.ShapeDtypeStruct((bsz, n, d), F32),
        compiler_params=_cparams(("parallel", "arbitrary")),
        name="finish_rwkv_lru",
    )(yf, yb, uf, ub, r, kt0, kt1, v, g, gate_br, xs, modsel, r_k, ln_w, ln_b, w_out)


def _max_arg(s):
    tiles = s.shape[0] // SUBLANES
    vals = [s[i * SUBLANES:(i + 1) * SUBLANES] for i in range(tiles)]
    row = lax.broadcasted_iota(jnp.int32, (SUBLANES, s.shape[1]), 0)
    ids = [row + i * SUBLANES for i in range(tiles)]
    while len(vals) > 1:
        nv, ni = [], []
        for a in range(0, len(vals) - 1, 2):
            take = vals[a + 1] > vals[a]
            nv.append(jnp.where(take, vals[a + 1], vals[a]))
            ni.append(jnp.where(take, ids[a + 1], ids[a]))
        if len(vals) % 2:
            nv.append(vals[-1])
            ni.append(ids[-1])
        vals, ids = nv, ni
    mx = jnp.max(vals[0], axis=0, keepdims=True)
    am = jnp.min(jnp.where(vals[0] == mx, ids[0], s.shape[0]), axis=0, keepdims=True)
    return mx, am


def _topk_rows(s, payload=None):
    iota = lax.broadcasted_iota(jnp.int32, s.shape, 0)
    vals, idxs = [], []
    for _ in range(PEER_TOPK):
        mx, am = _max_arg(s)
        sel = iota == am
        vals.append(mx)
        if payload is None:
            idxs.append(am)
        else:
            idxs.append(jnp.max(jnp.where(sel, payload, -1), axis=0, keepdims=True))
        s = jnp.where(sel, -jnp.inf, s)
    return vals, idxs


def _peer_candidates(sv0, si0, sv1, si1):
    grp = SUBLANES
    v1_all, i1_all = jnp.concatenate(sv1, axis=0), jnp.concatenate(si1, axis=0)
    v1_lo, i1_lo = v1_all[:grp], i1_all[:grp]
    row = lax.broadcasted_iota(jnp.int32, v1_lo.shape, 0)
    cand, cidx = [sv0[0] + v1_all], [si0[0] * N_KEYS + i1_all]
    for a in range(1, grp):
        cand.append(jnp.where(row < PEER_TOPK // (a + 1), sv0[a] + v1_lo, -jnp.inf))
        cidx.append(si0[a] * N_KEYS + i1_lo)
    cand.append(jnp.concatenate(sv0[grp:], axis=0) + sv1[0])
    cidx.append(jnp.concatenate(si0[grp:], axis=0) * N_KEYS + si1[0])
    return jnp.concatenate(cand, axis=0), jnp.concatenate(cidx, axis=0)


def _peer_select_kernel(x_ref, nw_ref, mod_ref, wq_ref, keys_ref, eid_ref, gate_ref):
    h = _norm_mod(x_ref[0], nw_ref[...], mod_ref, 3)
    q = jnp.dot(h.astype(BF16), wq_ref[...], preferred_element_type=F32)
    half = PEER_DK // 2
    for hd in range(PEER_HEADS):
        sv, si = [], []
        for p in range(2):
            hp = 2 * hd + p
            s = lax.dot_general(keys_ref[hp], q[:, hp * half:(hp + 1) * half],
                                (((1,), (1,)), ((), ())), preferred_element_type=F32)
            v, ix = _topk_rows(s)
            sv.append(v)
            si.append(ix)
        cand, cidx = _peer_candidates(sv[0], si[0], sv[1], si[1])
        best, eid = _topk_rows(cand, cidx)
        bestm = jnp.concatenate(best, axis=0)
        e = jnp.exp(bestm - best[0])
        gate = e / jnp.sum(e, axis=0, keepdims=True)
        eid_ref[0, hd * PEER_TOPK:(hd + 1) * PEER_TOPK, :] = jnp.concatenate(eid, axis=0)
        gate_ref[0, hd * PEER_TOPK:(hd + 1) * PEER_TOPK, :] = gate


def _peer_select(xs, nw, modsel, wq, keys, ctx_tiles):
    bsz, n, d = xs.shape
    nt = n // ROWS
    out = pl.BlockSpec((1, PEER_SLOTS, ROWS), lambda b, i: (b * nt + i, 0, 0))
    return pl.pallas_call(
        _peer_select_kernel,
        grid=(bsz, nt),
        in_specs=[pl.BlockSpec((1, ROWS, d), lambda b, i: (b, i, 0)),
                  pl.BlockSpec((1, d), lambda b, i: (0, 0)),
                  _mod_spec(ctx_tiles),
                  pl.BlockSpec(wq.shape, lambda b, i: (0, 0)),
                  pl.BlockSpec(keys.shape, lambda b, i: (0, 0, 0))],
        out_specs=[out, out],
        out_shape=[jax.ShapeDtypeStruct((bsz * nt, PEER_SLOTS, ROWS), jnp.int32),
                   jax.ShapeDtypeStruct((bsz * nt, PEER_SLOTS, ROWS), F32)],
        compiler_params=_cparams(("parallel", "arbitrary")),
        name="peer_select",
    )(xs, nw, modsel, wq, keys)


def _peer_gather_kernel(eid_hbm, x_ref, nw_ref, mod_ref, gate_ref, uv_hbm, uvw_hbm, o_ref, hbuf, *scratch, nblk):
    bufs = scratch[:PEER_NBUF]
    idx_smem, gsem, isem = scratch[PEER_NBUF:]
    i = pl.program_id(0)
    tb = x_ref.shape[0]
    ahead = PEER_NBUF - 1
    nidx = idx_smem.shape[0]
    cur = i % nidx
    nxt = (i + 1) % nidx

    def idx_copy(blk):
        slot = blk % nidx
        return pltpu.make_async_copy(eid_hbm.at[blk], idx_smem.at[slot], isem.at[slot])

    @pl.when(i == 0)
    def _():
        idx_copy(0).start()
        if nblk > 1:
            idx_copy(1).start()
        idx_copy(0).wait()

    @pl.when(i + 2 < nblk)
    def _():
        idx_copy(i + 2).start()

    @pl.when(i + 1 < nblk)
    def _():
        idx_copy(i + 1).wait()

    hbuf[...] = _norm_mod(x_ref[...], nw_ref[...], mod_ref, 3)
    g2 = mod_ref[0, :, 5 * D_MODEL:6 * D_MODEL]
    lane = lax.broadcasted_iota(jnp.int32, (PEER_SLOTS, tb), 1)

    def issue(islot, t, n, lo, hi):
        for j in range(lo, hi):
            e = idx_smem[islot, t, j]
            pltpu.make_async_copy(uv_hbm.at[e], bufs[n].at[:, pl.ds(j * ROW_TILES, ROW_TILES), :],
                                  gsem.at[n]).start(priority=j % 2)

    def consume(t, n, issue_part):
        pltpu.make_async_copy(uvw_hbm.at[pl.ds(0, 2)], bufs[n], gsem.at[n]).wait()
        xrow = hbuf[pl.ds(t, 1), :]
        acc = jnp.zeros((PEER_SLOTS, LANES), F32)
        for s in range(ROW_TILES):
            issue_part(s)
            us = bufs[n][0, pl.ds(s, PEER_SLOTS, stride=ROW_TILES), :]
            acc = acc + us * xrow[:, s * LANES:(s + 1) * LANES]
        act = jnp.sum(acc, axis=1, keepdims=True)
        gcol = jnp.sum(jnp.where(lane == t, gate_ref[0], 0.0), axis=1, keepdims=True)
        coef = jnp.broadcast_to(gcol * _gelu(act), (PEER_SLOTS, LANES))
        outs = []
        for s in range(ROW_TILES):
            issue_part(ROW_TILES + s)
            vs = bufs[n][1, pl.ds(s, PEER_SLOTS, stride=ROW_TILES), :]
            outs.append(jnp.sum(vs * coef, axis=0, keepdims=True))
        orow = jnp.concatenate(outs, axis=1)
        o_ref[pl.ds(t, 1), :] = x_ref[pl.ds(t, 1), :] + g2 * orow

    part = PEER_SLOTS // (2 * ROW_TILES)

    @pl.when(i == 0)
    def _():
        for t0 in range(ahead):
            issue(0, t0, t0, 0, PEER_SLOTS)

    def group(g, carry):
        for n in range(PEER_NBUF):
            t = g * PEER_NBUF + n
            consume(t, n, lambda k, t=t, n=n: issue(cur, t + ahead, (n + ahead) % PEER_NBUF,
                                                    k * part, (k + 1) * part))
        return carry

    ngroups = tb // PEER_NBUF
    lax.fori_loop(0, ngroups - 1, group, 0)
    for n in range(PEER_NBUF):
        t = (ngroups - 1) * PEER_NBUF + n
        if n == 0:
            consume(t, n, lambda k, t=t: issue(cur, t + ahead, ahead, k * part, (k + 1) * part))
        else:
            @pl.when(i + 1 < nblk)
            def _():
                issue(nxt, n - 1, n - 1, 0, PEER_SLOTS)
            consume(t, n, lambda k: None)


def _peer_gather(eid_t, gate, xs2, nw, modsel, uv, tiles_per_batch, ctx_tiles):
    ntok, d = xs2.shape
    nblk = ntok // PEER_TB
    slab = PEER_SLOTS * ROW_TILES
    assert PEER_TB % PEER_NBUF == 0
    uvw = uv.reshape(-1, slab, LANES)
    return pl.pallas_call(
        functools.partial(_peer_gather_kernel, nblk=nblk),
        grid=(nblk,),
        in_specs=[pl.BlockSpec(memory_space=pl.ANY),
                  pl.BlockSpec((PEER_TB, d), lambda i: (i, 0)),
                  pl.BlockSpec((1, d), lambda i: (0, 0)),
                  pl.BlockSpec((1, 1, 6 * D_MODEL),
                               lambda i: (2 * (i // tiles_per_batch)
                                          + jnp.where(i % tiles_per_batch >= ctx_tiles, 1, 0), 0, 0)),
                  pl.BlockSpec((1, PEER_SLOTS, PEER_TB), lambda i: (i, 0, 0)),
                  pl.BlockSpec(memory_space=pl.ANY),
                  pl.BlockSpec(memory_space=pl.ANY)],
        out_specs=pl.BlockSpec((PEER_TB, d), lambda i: (i, 0)),
        out_shape=jax.ShapeDtypeStruct((ntok, d), F32),
        scratch_shapes=[pltpu.VMEM((PEER_TB, d), F32),
                        *[pltpu.VMEM((2, slab, LANES), F32) for _ in range(PEER_NBUF)],
                        pltpu.SMEM((3, PEER_TB, PEER_SLOTS), jnp.int32),
                        pltpu.SemaphoreType.DMA((PEER_NBUF,)),
                        pltpu.SemaphoreType.DMA((3,))],
        compiler_params=_cparams(("arbitrary",)),
        name="peer_gather",
    )(eid_t, xs2, nw, modsel, gate, uv, uvw)


def _peer(xs, nw, modsel, wq, keys, u_tab, v_tab, ctx_tiles):
    bsz, n, d = xs.shape
    eid, gate = _peer_select(xs, nw, modsel, wq, keys, ctx_tiles)
    nblk = bsz * n // PEER_TB
    split = ROWS // PEER_TB
    eid_t = eid.reshape(-1, PEER_SLOTS, split, PEER_TB).transpose(0, 2, 3, 1).reshape(nblk, PEER_TB, PEER_SLOTS)
    gate_b = gate.reshape(-1, PEER_SLOTS, split, PEER_TB).transpose(0, 2, 1, 3).reshape(nblk, PEER_SLOTS, PEER_TB)
    ne = u_tab.shape[0]
    uv = jnp.stack([u_tab.reshape(ne, ROW_TILES, LANES), v_tab.reshape(ne, ROW_TILES, LANES)], axis=1)
    out = _peer_gather(eid_t, gate_b, xs.reshape(bsz * n, d), nw, modsel, uv,
                       n // PEER_TB, ctx_tiles * ROWS // PEER_TB)
    return out.reshape(bsz, n, d)


def _final_kernel(x_ref, w_ref, o_ref):
    x = x_ref[0]
    ms = jnp.mean(x * x, axis=-1, keepdims=True)
    o_ref[0] = x * lax.rsqrt(ms + EPS) * w_ref[...]


def _final_norm(xs, w, ctx_tiles, seq):
    bsz, n, d = xs.shape
    return pl.pallas_call(
        _final_kernel,
        grid=(bsz, seq // ROWS),
        in_specs=[pl.BlockSpec((1, ROWS, d), lambda b, i: (b, i + ctx_tiles, 0)),
                  pl.BlockSpec((1, d), lambda b, i: (0, 0))],
        out_specs=pl.BlockSpec((1, ROWS, d), lambda b, i: (b, i, 0)),
        out_shape=jax.ShapeDtypeStruct((bsz, seq, d), F32),
        compiler_params=_cparams(("parallel", "arbitrary")),
        name="final_norm",
    )(xs, w)


def _block_diag(w):
    nb, d, e = w.shape
    eye = jnp.eye(nb, dtype=w.dtype)
    return (eye[:, None, :, None] * w[:, :, None, :]).reshape(nb * d, nb * e)


def _mixer0(xs, modsel, norm1, w_in, w_out, i_bias, f_bias, mlstm_norm, conv_w, conv_b, dt_bias,
            a_log, d_skip, ssd_norm, ctx_tiles, nc_ctx):
    q0, k0, v0, o0, ig0, fg0, z0, xbc0, dt0, end = 0, 256, 512, 1024, 1536, 1544, 1552, 2064, 3088, 3104
    pad = jnp.zeros((D_MODEL, LANES - 32), w_in.dtype)
    w_cat = jnp.concatenate([w_in[:, q0:ig0], w_in[:, xbc0:dt0], w_in[:, z0:xbc0],
                             w_in[:, ig0:z0], w_in[:, dt0:end], pad], axis=1).astype(BF16)
    qkvo, zx, gates = _project(xs, norm1, modsel, w_cat, (1536, 1536, LANES), ctx_tiles)
    gates_row = jnp.swapaxes(gates[:, :, :32], 1, 2)
    bias = jnp.concatenate([i_bias.reshape(-1), f_bias.reshape(-1), dt_bias.reshape(-1)])
    bias_col = jnp.pad(bias, (0, LANES - 32)).reshape(1, LANES)
    bias_row = bias.reshape(32, 1)
    alog = jnp.concatenate([jnp.zeros((16,), F32), a_log.reshape(-1)])
    alog_col = jnp.pad(alog, (0, LANES - 32)).reshape(1, LANES)
    alog_row = alog.reshape(32, 1)
    xbc_act = _conv4(zx, 1024, conv_w, conv_b.reshape(1, -1), ctx_tiles, True)
    hf = _mlstm_scan(qkvo, gates, gates_row, bias_col, bias_row, nc_ctx, 0)
    hb = _mlstm_scan(qkvo, gates, gates_row, bias_col, bias_row, nc_ctx, 1)
    yf = _ssd_scan(xbc_act, gates, gates_row, bias_col, bias_row, alog_col, alog_row, nc_ctx, 0)
    yb = _ssd_scan(xbc_act, gates, gates_row, bias_col, bias_row, alog_col, alog_row, nc_ctx, 1)
    dskip = jnp.repeat(d_skip, B_HEADDIM).reshape(1, -1)
    return _finish0(hf, hb, yf, yb, qkvo, zx, xbc_act, xs, modsel, mlstm_norm.reshape(1, -1),
                    ssd_norm.reshape(1, -1), dskip, w_out.astype(BF16), ctx_tiles)


def _mixer1(xs, modsel, norm1, w_in, w_out, mu, w0, w_up, a0, a_up, g_up, k_k, k_a, r_k, ln_w, ln_b,
            conv_w, conv_b, lam, wa, ba, wi, bi, ctx_tiles, nc_ctx):
    pr, gate_br, x_br = _project(xs, norm1, modsel, w_in.astype(BF16), (RWKV_COLS, D_WIDTH, D_WIDTH),
                                 ctx_tiles)
    r, v, kk, g, w_f, w_b, kt_f, kt_b, al_f, al_b = _rwkv_prep(
        pr, mu.reshape(1, -1), w0, w_up, a0, a_up, g_up, k_k.reshape(1, -1), k_a.reshape(1, -1), ctx_tiles)
    yf, yb = _rwkv_scan(r, v, kk, w_f, kt_f, al_f, w_b, kt_b, al_b, nc_ctx)
    xc = _conv4(x_br, D_WIDTH, conv_w, conv_b.reshape(1, -1), ctx_tiles, False)
    w_gate = jnp.concatenate([_block_diag(wa[0]), _block_diag(wi[0]),
                              _block_diag(wa[1]), _block_diag(wi[1])], axis=1)
    b_gate = jnp.concatenate([ba[0], bi[0], ba[1], bi[1]]).reshape(1, -1)
    a_f, b_f, a_b, b_b = _lru_gates(xc, w_gate, b_gate, lam)
    uf = _lru_scan(a_f, b_f, nc_ctx, False)
    ub = _lru_scan(a_b, b_b, nc_ctx, True)
    return _finish1(yf, yb, uf, ub, r, kt_f, kt_b, v, g, gate_br, xs, modsel, r_k.reshape(1, -1),
                    ln_w.reshape(1, -1), ln_b.reshape(1, -1), w_out.astype(BF16), ctx_tiles)


def kernel(x, c, ctx, c_ctx, mod_w, mod_b, norm1, norm2, peer_wq, peer_keys, peer_u, peer_v, ev_w_in, ev_w_out, ev_mlstm_i_bias, ev_mlstm_f_bias, ev_mlstm_norm, ev_ssd_conv_w, ev_ssd_conv_b, ev_ssd_dt_bias, ev_ssd_a_log, ev_ssd_d, ev_ssd_norm, od_w_in, od_w_out, od_rwkv_mu, od_rwkv_w0, od_rwkv_w_up, od_rwkv_a0, od_rwkv_a_up, od_rwkv_g_up, od_rwkv_k_k, od_rwkv_k_a, od_rwkv_r_k, od_rwkv_ln_w, od_rwkv_ln_b, od_lru_conv_w, od_lru_conv_b, od_lru_lambda, od_lru_wa, od_lru_ba, od_lru_wi, od_lru_bi, final_norm):
    bsz, seq, d = x.shape
    ctx_len = ctx.shape[1]
    depth = mod_w.shape[0]
    assert d == D_MODEL and ctx_len == ROWS and seq % ROWS == 0 and bsz < SUBLANES
    ctx_tiles = ctx_len // ROWS
    nc_ctx = ctx_len // CHUNK
    xs = jnp.concatenate([ctx, x], axis=1)
    srows = jnp.concatenate([c, c_ctx[None, :], jnp.zeros((SUBLANES - bsz - 1, d), F32)], axis=0)
    for i in range(depth):
        mod = _modulation(srows, mod_w[i], mod_b[i].reshape(1, -1))
        modsel = jnp.stack([jnp.broadcast_to(mod[bsz], (bsz, 6 * d)), mod[:bsz]], axis=1)
        modsel = modsel.reshape(2 * bsz, 1, 6 * d)
        j = i // 2
        n1 = norm1[i].reshape(1, -1)
        if i % 2 == 0:
            xs = _mixer0(xs, modsel, n1, ev_w_in[j], ev_w_out[j], ev_mlstm_i_bias[j], ev_mlstm_f_bias[j],
                         ev_mlstm_norm[j], ev_ssd_conv_w[j], ev_ssd_conv_b[j], ev_ssd_dt_bias[j],
                         ev_ssd_a_log[j], ev_ssd_d[j], ev_ssd_norm[j], ctx_tiles, nc_ctx)
        else:
            xs = _mixer1(xs, modsel, n1, od_w_in[j], od_w_out[j], od_rwkv_mu[j], od_rwkv_w0[j],
                         od_rwkv_w_up[j], od_rwkv_a0[j], od_rwkv_a_up[j], od_rwkv_g_up[j], od_rwkv_k_k[j],
                         od_rwkv_k_a[j], od_rwkv_r_k[j].reshape(-1), od_rwkv_ln_w[j], od_rwkv_ln_b[j],
                         od_lru_conv_w[j], od_lru_conv_b[j], od_lru_lambda[j], od_lru_wa[j], od_lru_ba[j],
                         od_lru_wi[j], od_lru_bi[j], ctx_tiles, nc_ctx)
        keys = peer_keys[i].reshape(2 * PEER_HEADS, N_KEYS, PEER_DK // 2)
        xs = _peer(xs, norm2[i].reshape(1, -1), modsel, peer_wq[i].astype(BF16), keys,
                   peer_u[i], peer_v[i], ctx_tiles)
    return _final_norm(xs, final_norm.reshape(1, -1), ctx_tiles, seq)
```

```python
import functools
import math

import jax
import jax.numpy as jnp
from jax import lax
from jax.experimental import pallas as pl
from jax.experimental.pallas import tpu as pltpu

F32 = jnp.float32
BF16 = jnp.bfloat16
HIGHEST = lax.Precision.HIGHEST

D_MODEL = 1024
EPS = 1e-6
CHUNK = 128
ROWS = 256
GRID_W = 64
LANES = 128
SUBLANES = 8
MIB = 1024 * 1024

A_HEADS, A_DQK, A_DV = 4, 64, 128
A_WIDTH = A_HEADS * A_DV
B_HEADS, B_HEADDIM, B_GROUPS, B_STATE = 8, 64, 2, 128
B_WIDTH = B_HEADS * B_HEADDIM
C_HEADS, C_HEADDIM = 8, 64
C_WIDTH = C_HEADS * C_HEADDIM
C_LORA_W, C_LORA_A, C_LORA_G = 64, 64, 128
RWKV_COLS = 3 * C_WIDTH + C_LORA_W + C_LORA_A + C_LORA_G
RWKV_W_SCALE = math.exp(-0.5)
RWKV_GN_EPS = 64e-5
D_WIDTH = 512
LRU_C = 8.0
PEER_HEADS, PEER_DK, N_KEYS, PEER_TOPK = 8, 256, 128, 16
PEER_SLOTS = PEER_HEADS * PEER_TOPK
PEER_TB = 128
PEER_NBUF = 8
ROW_TILES = D_MODEL // LANES


def _cparams(sem, vmem_mib=48):
    return pltpu.CompilerParams(dimension_semantics=sem, vmem_limit_bytes=vmem_mib * MIB)


def _softplus(x):
    return jnp.maximum(x, 0.0) + jnp.log1p(jnp.exp(-jnp.abs(x)))


def _sigmoid(x):
    return 1.0 / (1.0 + jnp.exp(-x))


def _silu(x):
    return x * _sigmoid(x)


def _gelu(x):
    return 0.5 * x * (1.0 + lax.erf(x * (1.0 / math.sqrt(2.0))))


def _norm_mod(x, nw, mod_ref, slot):
    ms = jnp.mean(x * x, axis=-1, keepdims=True)
    y = x * lax.rsqrt(ms + EPS) * nw
    sh = mod_ref[0, :, slot * D_MODEL:(slot + 1) * D_MODEL]
    sc = mod_ref[0, :, (slot + 1) * D_MODEL:(slot + 2) * D_MODEL]
    return y * (1.0 + sc) + sh


def _chunk_order(c, nc_ctx, nc, reverse):
    if not reverse:
        return c
    return jnp.where(c < nc_ctx, nc_ctx - 1 - c, nc + nc_ctx - 1 - c)


def _mod_kernel(s_ref, w_ref, b_ref, o_ref):
    s = _silu(s_ref[...])
    o_ref[...] = jnp.dot(s, w_ref[...], precision=HIGHEST, preferred_element_type=F32) + b_ref[...]


def _modulation(srows, w, b):
    d = srows.shape[1]
    nt = w.shape[1] // d
    return pl.pallas_call(
        _mod_kernel,
        grid=(nt,),
        in_specs=[pl.BlockSpec((SUBLANES, d), lambda j: (0, 0)),
                  pl.BlockSpec((d, d), lambda j: (0, j)),
                  pl.BlockSpec((1, d), lambda j: (0, j))],
        out_specs=pl.BlockSpec((SUBLANES, d), lambda j: (0, j)),
        out_shape=jax.ShapeDtypeStruct((SUBLANES, w.shape[1]), F32),
        compiler_params=_cparams(("arbitrary",)),
        name="modulation",
    )(srows, w, b)


def _proj_kernel(x_ref, nw_ref, mod_ref, w_ref, *o_refs, widths):
    h = _norm_mod(x_ref[0], nw_ref[...], mod_ref, 0)
    out = jnp.dot(h.astype(BF16), w_ref[...], preferred_element_type=F32)
    off = 0
    for o_ref, wd in zip(o_refs, widths):
        o_ref[0] = out[:, off:off + wd]
        off += wd


def _mod_spec(ctx_tiles):
    return pl.BlockSpec((1, 1, 6 * D_MODEL),
                        lambda b, i: (2 * b + jnp.where(i >= ctx_tiles, 1, 0), 0, 0))


def _project(xs, nw, modsel, w, widths, ctx_tiles):
    bsz, n, d = xs.shape
    nt = n // ROWS
    return pl.pallas_call(
        functools.partial(_proj_kernel, widths=widths),
        grid=(bsz, nt),
        in_specs=[pl.BlockSpec((1, ROWS, d), lambda b, i: (b, i, 0)),
                  pl.BlockSpec((1, d), lambda b, i: (0, 0)),
                  _mod_spec(ctx_tiles),
                  pl.BlockSpec(w.shape, lambda b, i: (0, 0))],
        out_specs=[pl.BlockSpec((1, ROWS, wd), lambda b, i: (b, i, 0)) for wd in widths],
        out_shape=[jax.ShapeDtypeStruct((bsz, n, wd), F32) for wd in widths],
        compiler_params=_cparams(("parallel", "arbitrary")),
        name="norm_mod_project",
    )(xs, nw, modsel, w)


def _conv_kernel(x_ref, p_ref, n_ref, w_ref, b_ref, o_ref, *, ctx_tiles, ntiles, act):
    i = pl.program_id(1)
    x = x_ref[0]
    rows = x.shape[0]
    prev_ok = jnp.logical_and(i != 0, i != ctx_tiles)
    next_ok = jnp.logical_and(i != ctx_tiles - 1, i != ntiles - 1)
    p = jnp.where(prev_ok, p_ref[0], 0.0)
    nx = jnp.where(next_ok, n_ref[0], 0.0)
    row = lax.broadcasted_iota(jnp.int32, (rows, 1), 0)
    xm1 = jnp.where(row == 0, p[7:8], pltpu.roll(x, 1, 0))
    xm2 = jnp.where(row == 0, p[6:7], jnp.where(row == 1, p[7:8], pltpu.roll(x, 2, 0)))
    xp1 = jnp.where(row == rows - 1, nx[0:1], pltpu.roll(x, rows - 1, 0))
    w = w_ref[...]
    y = b_ref[...] + xm2 * w[0:1] + xm1 * w[1:2] + x * w[2:3] + xp1 * w[3:4]
    o_ref[0] = _silu(y) if act else y


def _conv4(x, width, w, b, ctx_tiles, act):
    bsz, n, _ = x.shape
    nt = n // ROWS
    hb = ROWS // SUBLANES
    nh = n // SUBLANES
    return pl.pallas_call(
        functools.partial(_conv_kernel, ctx_tiles=ctx_tiles, ntiles=nt, act=act),
        grid=(bsz, nt),
        in_specs=[pl.BlockSpec((1, ROWS, width), lambda b_, i: (b_, i, 0)),
                  pl.BlockSpec((1, SUBLANES, width), lambda b_, i: (b_, jnp.maximum(i * hb - 1, 0), 0)),
                  pl.BlockSpec((1, SUBLANES, width),
                               lambda b_, i: (b_, jnp.minimum((i + 1) * hb, nh - 1), 0)),
                  pl.BlockSpec((4, width), lambda b_, i: (0, 0)),
                  pl.BlockSpec((1, width), lambda b_, i: (0, 0))],
        out_specs=pl.BlockSpec((1, ROWS, width), lambda b_, i: (b_, i, 0)),
        out_shape=jax.ShapeDtypeStruct((bsz, n, width), F32),
        compiler_params=_cparams(("parallel", "arbitrary")),
        name="conv4",
    )(x, x, x, w, b)


def _tri(reverse):
    t = lax.broadcasted_iota(jnp.int32, (CHUNK, CHUNK), 0)
    s = lax.broadcasted_iota(jnp.int32, (CHUNK, CHUNK), 1)
    return (s >= t) if reverse else (s <= t)


def _cumsums(col, row, mask):
    mf = mask.astype(F32)
    b_col = jnp.dot(mf, col, precision=HIGHEST, preferred_element_type=F32)
    b_row = lax.dot_general(row, mf, (((1,), (1,)), ((), ())), precision=HIGHEST,
                            preferred_element_type=F32)
    return b_col, b_row


def _mlstm_kernel(qkv_ref, gc_ref, gr_ref, bc_ref, br_ref, o_ref, ct_ref, n_ref, m_ref, *,
                  direction):
    c = pl.program_id(1)

    @pl.when(c == 0)
    def _():
        ct_ref[...] = jnp.zeros_like(ct_ref)
        n_ref[...] = jnp.zeros_like(n_ref)
        m_ref[...] = jnp.zeros_like(m_ref)

    reverse = direction == 1
    mask = _tri(reverse)
    gc = gc_ref[0] + bc_ref[...]
    gr = gr_ref[0] + br_ref[...]
    d4 = direction * A_HEADS
    li_col = gc[:, d4:d4 + A_HEADS]
    lf_col = -_softplus(-gc[:, 8 + d4:8 + d4 + A_HEADS])
    li_row = gr[d4:d4 + A_HEADS, :]
    lf_row = -_softplus(-gr[8 + d4:8 + d4 + A_HEADS, :])
    b_col, b_row = _cumsums(lf_col, lf_row, mask)
    tot = jnp.sum(lf_col, axis=0, keepdims=True)
    outs = []
    for h in range(A_HEADS):
        q = qkv_ref[0, :, h * A_DQK:(h + 1) * A_DQK] * (A_DQK ** -0.5)
        k = qkv_ref[0, :, A_HEADS * A_DQK + h * A_DQK:A_HEADS * A_DQK + (h + 1) * A_DQK]
        v = qkv_ref[0, :, 2 * A_HEADS * A_DQK + h * A_DV:2 * A_HEADS * A_DQK + (h + 1) * A_DV]
        bc, br = b_col[:, h:h + 1], b_row[h:h + 1, :]
        ic, ir = li_col[:, h:h + 1], li_row[h:h + 1, :]
        th = tot[:, h:h + 1]
        m_prev = m_ref[h]
        dmat = jnp.where(mask, bc - br + ir, -jnp.inf)
        inter = bc + m_prev
        m_t = jnp.maximum(inter, jnp.max(dmat, axis=1, keepdims=True))
        qk = lax.dot_general(q, k, (((1,), (1,)), ((), ())), preferred_element_type=F32)
        s = qk * jnp.exp(dmat - m_t)
        w_inter = jnp.exp(inter - m_t)
        num = jnp.dot(s, v, preferred_element_type=F32) \
            + w_inter * jnp.dot(q, ct_ref[h], preferred_element_type=F32)
        den = jnp.sum(s, axis=1, keepdims=True) \
            + w_inter * jnp.sum(q * n_ref[h], axis=1, keepdims=True)
        outs.append(num / jnp.maximum(jnp.abs(den), jnp.exp(-m_t)))
        g = th - bc + ic
        m_new = jnp.maximum(th + m_prev, jnp.max(g, axis=0, keepdims=True))
        wg = jnp.exp(g - m_new)
        wc = jnp.exp(th + m_prev - m_new)
        ct_ref[h] = wc * ct_ref[h] + lax.dot_general(k * wg, v, (((0,), (0,)), ((), ())),
                                                     preferred_element_type=F32)
        n_ref[h] = wc * n_ref[h] + jnp.sum(wg * k, axis=0, keepdims=True)
        m_ref[h] = m_new
    o_ref[0] = jnp.concatenate(outs, axis=1)


def _mlstm_scan(qkvo, gates, gates_row, bias_col, bias_row, nc_ctx, direction):
    bsz, n, _ = qkvo.shape
    nc = n // CHUNK
    reverse = direction == 1
    order = lambda c: _chunk_order(c, nc_ctx, nc, reverse)
    return pl.pallas_call(
        functools.partial(_mlstm_kernel, direction=direction),
        grid=(bsz, nc),
        in_specs=[pl.BlockSpec((1, CHUNK, 2 * A_HEADS * A_DQK + A_WIDTH), lambda b, c: (b, order(c), 0)),
                  pl.BlockSpec((1, CHUNK, LANES), lambda b, c: (b, order(c), 0)),
                  pl.BlockSpec((1, 32, CHUNK), lambda b, c: (b, 0, order(c))),
                  pl.BlockSpec((1, LANES), lambda b, c: (0, 0)),
                  pl.BlockSpec((32, 1), lambda b, c: (0, 0))],
        out_specs=pl.BlockSpec((1, CHUNK, A_WIDTH), lambda b, c: (b, order(c), 0)),
        out_shape=jax.ShapeDtypeStruct((bsz, n, A_WIDTH), F32),
        scratch_shapes=[pltpu.VMEM((A_HEADS, A_DQK, A_DV), F32),
                        pltpu.VMEM((A_HEADS, 1, A_DQK), F32),
                        pltpu.VMEM((A_HEADS, 1, 1), F32)],
        compiler_params=_cparams(("parallel", "arbitrary")),
        name="mlstm_scan",
    )(qkvo, gates, gates_row, bias_col, bias_row)


def _ssd_kernel(xbc_ref, gc_ref, gr_ref, bc_ref, br_ref, ac_ref, ar_ref, o_ref, st_ref, *,
                direction):
    c = pl.program_id(1)

    @pl.when(c == 0)
    def _():
        st_ref[...] = jnp.zeros_like(st_ref)

    reverse = direction == 1
    mask = _tri(reverse)
    gc = gc_ref[0] + bc_ref[...]
    gr = gr_ref[0] + br_ref[...]
    d8 = 16 + direction * B_HEADS
    dt_col = _softplus(gc[:, d8:d8 + B_HEADS])
    dt_row = _softplus(gr[d8:d8 + B_HEADS, :])
    la_col = -dt_col * jnp.exp(ac_ref[:, d8:d8 + B_HEADS])
    la_row = -dt_row * jnp.exp(ar_ref[d8:d8 + B_HEADS, :])
    b_col, b_row = _cumsums(la_col, la_row, mask)
    tot = jnp.sum(la_col, axis=0, keepdims=True)
    outs = []
    hpg = B_HEADS // B_GROUPS
    for g in range(B_GROUPS):
        bm = xbc_ref[0, :, B_WIDTH + g * B_STATE:B_WIDTH + (g + 1) * B_STATE]
        cm = xbc_ref[0, :, B_WIDTH + (B_GROUPS + g) * B_STATE:B_WIDTH + (B_GROUPS + g + 1) * B_STATE]
        cb = lax.dot_general(cm, bm, (((1,), (1,)), ((), ())), preferred_element_type=F32)
        for h in range(g * hpg, (g + 1) * hpg):
            xh = xbc_ref[0, :, h * B_HEADDIM:(h + 1) * B_HEADDIM]
            bc, br = b_col[:, h:h + 1], b_row[h:h + 1, :]
            th = tot[:, h:h + 1]
            decay = jnp.exp(jnp.where(mask, bc - br, -jnp.inf))
            s = cb * decay * dt_row[h:h + 1, :]
            y = jnp.dot(s, xh, preferred_element_type=F32) \
                + jnp.exp(bc) * jnp.dot(cm, st_ref[h], preferred_element_type=F32)
            outs.append(y)
            w_end = jnp.exp(th - bc) * dt_col[:, h:h + 1]
            st_ref[h] = jnp.exp(th) * st_ref[h] + lax.dot_general(
                bm * w_end, xh, (((0,), (0,)), ((), ())), preferred_element_type=F32)
    o_ref[0] = jnp.concatenate(outs, axis=1)


def _ssd_scan(xbc, gates, gates_row, bias_col, bias_row, alog_col, alog_row, nc_ctx, direction):
    bsz, n, width = xbc.shape
    nc = n // CHUNK
    reverse = direction == 1
    order = lambda c: _chunk_order(c, nc_ctx, nc, reverse)
    return pl.pallas_call(
        functools.partial(_ssd_kernel, direction=direction),
        grid=(bsz, nc),
        in_specs=[pl.BlockSpec((1, CHUNK, width), lambda b, c: (b, order(c), 0)),
                  pl.BlockSpec((1, CHUNK, LANES), lambda b, c: (b, order(c), 0)),
                  pl.BlockSpec((1, 32, CHUNK), lambda b, c: (b, 0, order(c))),
                  pl.BlockSpec((1, LANES), lambda b, c: (0, 0)),
                  pl.BlockSpec((32, 1), lambda b, c: (0, 0)),
                  pl.BlockSpec((1, LANES), lambda b, c: (0, 0)),
                  pl.BlockSpec((32, 1), lambda b, c: (0, 0))],
        out_specs=pl.BlockSpec((1, CHUNK, B_WIDTH), lambda b, c: (b, order(c), 0)),
        out_shape=jax.ShapeDtypeStruct((bsz, n, B_WIDTH), F32),
        scratch_shapes=[pltpu.VMEM((B_HEADS, B_STATE, B_HEADDIM), F32)],
        compiler_params=_cparams(("parallel", "arbitrary")),
        name="ssd_scan",
    )(xbc, gates, gates_row, bias_col, bias_row, alog_col, alog_row)


def _finish0_kernel(hf_ref, hb_ref, yf_ref, yb_ref, o_ref_in, z_ref, xs_ref, x_ref, mod_ref,
                    mn_ref, sn_ref, dsk_ref, w_ref, out_ref):
    h = hf_ref[0] + hb_ref[0]
    parts = []
    for hd in range(A_HEADS):
        hh = h[:, hd * A_DV:(hd + 1) * A_DV]
        ms = jnp.mean(hh * hh, axis=-1, keepdims=True)
        parts.append(hh * lax.rsqrt(ms + EPS))
    hn = jnp.concatenate(parts, axis=1) * mn_ref[...]
    ya = _sigmoid(o_ref_in[0]) * hn
    y = yf_ref[0] + yb_ref[0] + dsk_ref[...] * xs_ref[0]
    t = y * _silu(z_ref[0])
    ms = jnp.mean(t * t, axis=-1, keepdims=True)
    yb = t * lax.rsqrt(ms + EPS) * sn_ref[...]
    f = jnp.concatenate([ya, yb], axis=1)
    g1 = mod_ref[0, :, 2 * D_MODEL:3 * D_MODEL]
    out_ref[0] = x_ref[0] + g1 * jnp.dot(f.astype(BF16), w_ref[...], preferred_element_type=F32)


def _finish0(hf, hb, yf, yb, qkvo, zx, xbc_act, xs, modsel, mnorm, snorm, dskip, w_out, ctx_tiles):
    bsz, n, d = xs.shape
    nt = n // ROWS
    half = lambda j: pl.BlockSpec((1, ROWS, 512), lambda b, i, j=j: (b, i, j))
    vec = pl.BlockSpec((1, 512), lambda b, i: (0, 0))
    return pl.pallas_call(
        _finish0_kernel,
        grid=(bsz, nt),
        in_specs=[half(0), half(0), half(0), half(0), half(2), half(2), half(0),
                  pl.BlockSpec((1, ROWS, d), lambda b, i: (b, i, 0)),
                  _mod_spec(ctx_tiles), vec, vec, vec,
                  pl.BlockSpec((d, d), lambda b, i: (0, 0))],
        out_specs=pl.BlockSpec((1, ROWS, d), lambda b, i: (b, i, 0)),
        out_shape=jax.ShapeDtypeStruct((bsz, n, d), F32),
        compiler_params=_cparams(("parallel", "arbitrary")),
        name="finish_mlstm_ssd",
    )(hf, hb, yf, yb, qkvo, zx, xbc_act, xs, modsel, mnorm, snorm, dskip, w_out)


def _head_ones(n, hd):
    r = lax.broadcasted_iota(jnp.int32, (n, n), 0) // hd
    c = lax.broadcasted_iota(jnp.int32, (n, n), 1) // hd
    return (r == c).astype(F32)


def _rwkv_prep_kernel(pr_ref, p_ref, n_ref, mu_ref, w0_ref, wup_ref, a0_ref, aup_ref, gup_ref,
                      kk_ref, ka_ref, r_o, v_o, kk_o, g_o, w0_o, w1_o, kt0_o, kt1_o, al0_o, al1_o,
                      *, ctx_tiles, ntiles):
    i = pl.program_id(1)
    pr = pr_ref[0]
    rows, cols = pr.shape
    q = cols // 4
    row = lax.broadcasted_iota(jnp.int32, (rows, 1), 0)
    col = lax.broadcasted_iota(jnp.int32, (1, cols), 1)
    is_ctx = i < ctx_tiles
    l1 = pltpu.roll(pr, 1, 0)
    r1 = pltpu.roll(pr, rows - 1, 0)
    left = jnp.where(row % GRID_W == 0, 0.0, l1)
    right = jnp.where(row % GRID_W == GRID_W - 1, 0.0, r1)
    up = jnp.concatenate([p_ref[0], pr[:rows - GRID_W]], axis=0)
    up = jnp.where(jnp.logical_and(i == ctx_tiles, row < GRID_W), 0.0, up)
    down = jnp.concatenate([pr[GRID_W:], n_ref[0]], axis=0)
    down = jnp.where(jnp.logical_and(i == ntiles - 1, row >= rows - GRID_W), 0.0, down)
    grid_sh = jnp.where(col < q, left, jnp.where(col < 2 * q, right, jnp.where(col < 3 * q, up, down)))
    prev = jnp.where(row == 0, 0.0, l1)
    nxt = jnp.where(row == rows - 1, 0.0, r1)
    seq_sh = jnp.where(col < 2 * q, prev, nxt)
    shifted = jnp.where(is_ctx, seq_sh, grid_sh)
    pr = pr + mu_ref[...] * (shifted - pr)
    r = pr[:, 0:C_WIDTH]
    k = pr[:, C_WIDTH:2 * C_WIDTH]
    v = pr[:, 2 * C_WIDTH:3 * C_WIDTH]
    o = 3 * C_WIDTH
    wd = pr[:, o:o + C_LORA_W]
    ad = pr[:, o + C_LORA_W:o + C_LORA_W + C_LORA_A]
    gd = pr[:, o + C_LORA_W + C_LORA_A:]
    kk = k * kk_ref[...]
    ss = jnp.dot(kk * kk, _head_ones(C_WIDTH, C_HEADDIM), precision=HIGHEST, preferred_element_type=F32)
    kk = kk * lax.rsqrt(ss + 1e-12)
    tw = jnp.tanh(wd)
    for d, (w_o, kt_o, al_o) in enumerate(((w0_o, kt0_o, al0_o), (w1_o, kt1_o, al1_o))):
        logw = -RWKV_W_SCALE * _sigmoid(w0_ref[d:d + 1, :] + jnp.dot(tw, wup_ref[d], preferred_element_type=F32))
        a = _sigmoid(a0_ref[d:d + 1, :] + jnp.dot(ad, aup_ref[d], preferred_element_type=F32))
        w_o[0] = jnp.exp(logw)
        kt_o[0] = k * (1.0 + (a - 1.0) * ka_ref[...])
        al_o[0] = a
    r_o[0] = r
    v_o[0] = v
    kk_o[0] = kk
    g_o[0] = jnp.dot(_sigmoid(gd), gup_ref[...], preferred_element_type=F32)


def _rwkv_prep(pr, mu, w0, w_up, a0, a_up, g_up, k_k, k_a, ctx_tiles):
    bsz, n, cols = pr.shape
    nt = n // ROWS
    hb = ROWS // GRID_W
    nh = n // GRID_W
    full = lambda a: pl.BlockSpec(a.shape, lambda b, i, nd=a.ndim: (0,) * nd)
    out = pl.BlockSpec((1, ROWS, C_WIDTH), lambda b, i: (b, i, 0))
    return pl.pallas_call(
        functools.partial(_rwkv_prep_kernel, ctx_tiles=ctx_tiles, ntiles=nt),
        grid=(bsz, nt),
        in_specs=[pl.BlockSpec((1, ROWS, cols), lambda b, i: (b, i, 0)),
                  pl.BlockSpec((1, GRID_W, cols), lambda b, i: (b, jnp.maximum(i * hb - 1, 0), 0)),
                  pl.BlockSpec((1, GRID_W, cols), lambda b, i: (b, jnp.minimum((i + 1) * hb, nh - 1), 0)),
                  full(mu), full(w0), full(w_up), full(a0), full(a_up), full(g_up), full(k_k), full(k_a)],
        out_specs=[out] * 10,
        out_shape=[jax.ShapeDtypeStruct((bsz, n, C_WIDTH), F32)] * 10,
        compiler_params=_cparams(("parallel", "arbitrary")),
        name="rwkv_prep",
    )(pr, pr, pr, mu, w0, w_up, a0, a_up, g_up, k_k, k_a)


RWKV_UNROLL = 4


def _rwkv_kernel(rf, wf, kf, vf, kkf, alf, rb, wb, kb, vb, kkb, alb, of_ref, ob_ref, s_ref):
    c = pl.program_id(0)

    @pl.when(c == 0)
    def _():
        s_ref[...] = jnp.zeros_like(s_ref)

    nb = rf.shape[0]
    t_len = rf.shape[1]
    nt = C_WIDTH // LANES
    sub = lax.broadcasted_iota(jnp.int32, (C_HEADDIM, C_WIDTH), 0)
    lane = lax.broadcasted_iota(jnp.int32, (C_HEADDIM, C_WIDTH), 1)
    diag = (lane % C_HEADDIM == sub).astype(F32)
    ones = _head_ones(LANES, C_HEADDIM).astype(BF16)
    ones2 = jnp.concatenate([ones, ones], axis=0)
    chains = [(refs, b) for refs in ((rf, wf, kf, vf, kkf, alf, of_ref, False),
                                     (rb, wb, kb, vb, kkb, alb, ob_ref, True)) for b in range(nb)]

    def head_sum(xs, split):
        x = jnp.concatenate([a[:, j * LANES:(j + 1) * LANES] for a in xs for j in range(nt)], axis=0)
        if split:
            hi = x.astype(BF16)
            lo = (x - hi.astype(F32)).astype(BF16)
            out = jnp.dot(jnp.concatenate([hi, lo], axis=1), ones2, preferred_element_type=F32)
        else:
            out = jnp.dot(x.astype(BF16), ones, preferred_element_type=F32)
        res = []
        for n in range(len(xs)):
            res.append(jnp.concatenate(
                [out[(nt * n + j) * C_HEADDIM:(nt * n + j + 1) * C_HEADDIM] for j in range(nt)], axis=1))
        return res

    def step(i, states):
        rows = []
        for refs, b in chains:
            t = (t_len - 1 - i) if refs[7] else i
            rows.append([ref[b, pl.ds(t, 1), :] for ref in refs[:6]] + [t])
        sa = [head_sum([s * (-row[4])], True)[0] for s, row in zip(states, rows)]
        vcol = [head_sum([diag * row[3]], False)[0] for row in rows]
        new = [s * row[1] + a * (row[4] * row[5]) + vc * row[2]
               for s, row, a, vc in zip(states, rows, sa, vcol)]
        ys = [head_sum([s * row[0]], False)[0] for s, row in zip(new, rows)]
        for (refs, b), row, y in zip(chains, rows, ys):
            refs[6][b, pl.ds(row[6], 1), :] = jnp.sum(y * diag, axis=0, keepdims=True)
        return new

    def body(g, states):
        states = list(states)
        for u in range(RWKV_UNROLL):
            states = step(g * RWKV_UNROLL + u, states)
        return tuple(states)

    states = lax.fori_loop(0, t_len // RWKV_UNROLL, body,
                           tuple(s_ref[n] for n in range(len(chains))))
    for n in range(len(chains)):
        s_ref[n] = states[n]


def _rwkv_scan(r, v, kk, w_f, kt_f, al_f, w_b, kt_b, al_b, nc_ctx):
    bsz, n, width = r.shape
    nc = n // CHUNK
    fwd = pl.BlockSpec((bsz, CHUNK, width), lambda c: (0, c, 0))
    bwd = pl.BlockSpec((bsz, CHUNK, width), lambda c: (0, _chunk_order(c, nc_ctx, nc, True), 0))
    return pl.pallas_call(
        _rwkv_kernel,
        grid=(nc,),
        in_specs=[fwd] * 6 + [bwd] * 6,
        out_specs=[fwd, bwd],
        out_shape=[jax.ShapeDtypeStruct((bsz, n, width), F32)] * 2,
        scratch_shapes=[pltpu.VMEM((2 * bsz, C_HEADDIM, width), F32)],
        compiler_params=_cparams(("arbitrary",)),
        name="rwkv_scan",
    )(r, w_f, kt_f, v, kk, al_f, r, w_b, kt_b, v, kk, al_b)


def _lru_gate_kernel(xc_ref, w_ref, b_ref, lam_ref, a0_o, b0_o, a1_o, b1_o):
    xc = xc_ref[0]
    z = jnp.dot(xc, w_ref[...], preferred_element_type=F32) + b_ref[...]
    for d, (a_o, b_o) in enumerate(((a0_o, b0_o), (a1_o, b1_o))):
        gr = _sigmoid(z[:, 2 * d * D_WIDTH:(2 * d + 1) * D_WIDTH])
        gi = _sigmoid(z[:, (2 * d + 1) * D_WIDTH:(2 * d + 2) * D_WIDTH])
        log_a = -LRU_C * gr * _softplus(-lam_ref[d:d + 1, :])
        th = jnp.tanh(log_a)
        one_minus_a2 = -2.0 * th / (1.0 - th)
        a_o[0] = jnp.exp(log_a)
        b_o[0] = jnp.sqrt(one_minus_a2) * (gi * xc)


def _lru_gates(xc, w, b, lam):
    bsz, n, width = xc.shape
    nt = n // ROWS
    out = pl.BlockSpec((1, ROWS, width), lambda b_, i: (b_, i, 0))
    return pl.pallas_call(
        _lru_gate_kernel,
        grid=(bsz, nt),
        in_specs=[out,
                  pl.BlockSpec(w.shape, lambda b_, i: (0, 0)),
                  pl.BlockSpec(b.shape, lambda b_, i: (0, 0)),
                  pl.BlockSpec(lam.shape, lambda b_, i: (0, 0))],
        out_specs=[out] * 4,
        out_shape=[jax.ShapeDtypeStruct((bsz, n, width), F32)] * 4,
        compiler_params=_cparams(("parallel", "arbitrary")),
        name="lru_gates",
    )(xc, w, b, lam)


def _lru_kernel(a_ref, b_ref, o_ref, h_ref, *, reverse):
    c = pl.program_id(0)

    @pl.when(c == 0)
    def _():
        h_ref[...] = jnp.zeros_like(h_ref)

    nb = a_ref.shape[0]
    t_len = a_ref.shape[1]

    def step(i, hs):
        t = (t_len - 1 - i) if reverse else i
        new = []
        for b in range(nb):
            h = a_ref[b, pl.ds(t, 1), :] * hs[b] + b_ref[b, pl.ds(t, 1), :]
            o_ref[b, pl.ds(t, 1), :] = h
            new.append(h)
        return tuple(new)

    hs = lax.fori_loop(0, t_len, step, tuple(h_ref[b] for b in range(nb)))
    for b in range(nb):
        h_ref[b] = hs[b]


def _lru_scan(a, b, nc_ctx, reverse):
    bsz, n, width = a.shape
    nc = n // CHUNK
    order = lambda c: _chunk_order(c, nc_ctx, nc, reverse)
    spec = pl.BlockSpec((bsz, CHUNK, width), lambda c: (0, order(c), 0))
    return pl.pallas_call(
        functools.partial(_lru_kernel, reverse=reverse),
        grid=(nc,),
        in_specs=[spec, spec],
        out_specs=spec,
        out_shape=jax.ShapeDtypeStruct((bsz, n, width), F32),
        scratch_shapes=[pltpu.VMEM((bsz, 1, width), F32)],
        compiler_params=_cparams(("arbitrary",)),
        name="lru_scan",
    )(a, b)


def _finish1_kernel(yf_ref, yb_ref, uf_ref, ub_ref, r_ref, kt0_ref, kt1_ref, v_ref, g_ref, gb_ref,
                    x_ref, mod_ref, rk_ref, lw_ref, lb_ref, w_ref, out_ref):
    y = yf_ref[0] + yb_ref[0]
    hs = _head_ones(C_WIDTH, C_HEADDIM)
    mean = jnp.dot(y, hs, precision=HIGHEST, preferred_element_type=F32) * (1.0 / C_HEADDIM)
    yc = y - mean
    var = jnp.dot(yc * yc, hs, precision=HIGHEST, preferred_element_type=F32) * (1.0 / C_HEADDIM)
    yn = yc * lax.rsqrt(var + RWKV_GN_EPS) * lw_ref[...] + lb_ref[...]
    kb = 0.5 * (kt0_ref[0] + kt1_ref[0])
    bonus = jnp.dot(r_ref[0] * kb * rk_ref[...], hs, precision=HIGHEST, preferred_element_type=F32)
    yn = yn + bonus * v_ref[0]
    yc_ = yn * g_ref[0]
    yd = (uf_ref[0] + ub_ref[0]) * _gelu(gb_ref[0])
    f = jnp.concatenate([yc_, yd], axis=1)
    g1 = mod_ref[0, :, 2 * D_MODEL:3 * D_MODEL]
    out_ref[0] = x_ref[0] + g1 * jnp.dot(f.astype(BF16), w_ref[...], preferred_element_type=F32)


def _finish1(yf, yb, uf, ub, r, kt0, kt1, v, g, gate_br, xs, modsel, r_k, ln_w, ln_b, w_out, ctx_tiles):
    bsz, n, d = xs.shape
    nt = n // ROWS
    half = pl.BlockSpec((1, ROWS, 512), lambda b, i: (b, i, 0))
    vec = pl.BlockSpec((1, 512), lambda b, i: (0, 0))
    return pl.pallas_call(
        _finish1_kernel,
        grid=(bsz, nt),
        in_specs=[half] * 10 + [pl.BlockSpec((1, ROWS, d), lambda b, i: (b, i, 0)),
                                _mod_spec(ctx_tiles), vec, vec, vec,
                                pl.BlockSpec((d, d), lambda b, i: (0, 0))],
        out_specs=pl.BlockSpec((1, ROWS, d), lambda b, i: (b, i, 0)),
        out_shape=jax.ShapeDtypeStruct((bsz, n, d), F32),
        compiler_params=_cparams(("parallel", "arbitrary")),
        name="finish_rwkv_lru",
    )(yf, yb, uf, ub, r, kt0, kt1, v, g, gate_br, xs, modsel, r_k, ln_w, ln_b, w_out)


def _max_arg(s):
    tiles = s.shape[0] // SUBLANES
    vals = [s[i * SUBLANES:(i + 1) * SUBLANES] for i in range(tiles)]
    row = lax.broadcasted_iota(jnp.int32, (SUBLANES, s.shape[1]), 0)
    ids = [row + i * SUBLANES for i in range(tiles)]
    while len(vals) > 1:
        nv, ni = [], []
        for a in range(0, len(vals) - 1, 2):
            take = vals[a + 1] > vals[a]
            nv.append(jnp.where(take, vals[a + 1], vals[a]))
            ni.append(jnp.where(take, ids[a + 1], ids[a]))
        if len(vals) % 2:
            nv.append(vals[-1])
            ni.append(ids[-1])
        vals, ids = nv, ni
    mx = jnp.max(vals[0], axis=0, keepdims=True)
    am = jnp.min(jnp.where(vals[0] == mx, ids[0], s.shape[0]), axis=0, keepdims=True)
    return mx, am


def _topk_rows(s, payload=None):
    iota = lax.broadcasted_iota(jnp.int32, s.shape, 0)
    vals, idxs = [], []
    for _ in range(PEER_TOPK):
        mx, am = _max_arg(s)
        sel = iota == am
        vals.append(mx)
        if payload is None:
            idxs.append(am)
        else:
            idxs.append(jnp.max(jnp.where(sel, payload, -1), axis=0, keepdims=True))
        s = jnp.where(sel, -jnp.inf, s)
    return vals, idxs


def _peer_candidates(sv0, si0, sv1, si1):
    grp = SUBLANES
    v1_all, i1_all = jnp.concatenate(sv1, axis=0), jnp.concatenate(si1, axis=0)
    v1_lo, i1_lo = v1_all[:grp], i1_all[:grp]
    row = lax.broadcasted_iota(jnp.int32, v1_lo.shape, 0)
    cand, cidx = [sv0[0] + v1_all], [si0[0] * N_KEYS + i1_all]
    for a in range(1, grp):
        cand.append(jnp.where(row < PEER_TOPK // (a + 1), sv0[a] + v1_lo, -jnp.inf))
        cidx.append(si0[a] * N_KEYS + i1_lo)
    cand.append(jnp.concatenate(sv0[grp:], axis=0) + sv1[0])
    cidx.append(jnp.concatenate(si0[grp:], axis=0) * N_KEYS + si1[0])
    return jnp.concatenate(cand, axis=0), jnp.concatenate(cidx, axis=0)


def _select_head(hn, wq_ref, keys_ref, k):
    half = PEER_DK // 2
    q = jnp.dot(hn, wq_ref[k], preferred_element_type=F32)
    sv, si = [], []
    for p in range(2):
        s = lax.dot_general(keys_ref[2 * k + p], q[:, p * half:(p + 1) * half],
                            (((1,), (1,)), ((), ())), preferred_element_type=F32)
        v, ix = _topk_rows(s)
        sv.append(v)
        si.append(ix)
    cand, cidx = _peer_candidates(sv[0], si[0], sv[1], si[1])
    best, eid = _topk_rows(cand, cidx)
    e = jnp.exp(jnp.concatenate(best, axis=0) - best[0])
    return jnp.concatenate(eid, axis=0), e / jnp.sum(e, axis=0, keepdims=True)


def _peer_kernel(x_ref, xn_ref, nw_ref, mod_ref, modn_ref, wq_ref, keys_ref, uv_hbm, uvw_hbm, o_ref,
                 hbuf, hnext, eid_v, eid_t, gate_s, *scratch, nblk):
    bufs = scratch[:PEER_NBUF]
    idx_smem, gsem, isem = scratch[PEER_NBUF:]
    i = pl.program_id(0)
    tb = x_ref.shape[0]
    ahead = PEER_NBUF - 1
    cur = i % 2
    nxt = 1 - cur
    ngroups = tb // PEER_NBUF
    gpp = ngroups // PEER_HEADS
    part = PEER_SLOTS // (2 * ROW_TILES)

    def select_piece(hn, k, slot):
        eid, gate = _select_head(hn, wq_ref, keys_ref, k)
        eid_v[pl.ds(pl.multiple_of(k * PEER_TOPK, PEER_TOPK), PEER_TOPK), :] = eid
        gate_s[slot, pl.ds(pl.multiple_of(k * PEER_TOPK, PEER_TOPK), PEER_TOPK), :] = gate

    def ids_to_smem(slot):
        eid_t[...] = eid_v[...].T
        return pltpu.make_async_copy(eid_t, idx_smem.at[slot], isem.at[slot])

    def issue(islot, t, n, lo, hi):
        for j in range(lo, hi):
            e = idx_smem[islot, t, j]
            pltpu.make_async_copy(uv_hbm.at[e], bufs[n].at[:, pl.ds(j * ROW_TILES, ROW_TILES), :],
                                  gsem.at[n]).start(priority=j % 2)

    hbuf[...] = _norm_mod(x_ref[...], nw_ref[...], mod_ref, 3)

    @pl.when(i == 0)
    def _():
        h0 = hbuf[...].astype(BF16)

        def piece0(k, carry):
            select_piece(h0, k, 0)
            return carry

        lax.fori_loop(0, PEER_HEADS, piece0, 0)
        cp = ids_to_smem(0)
        cp.start()
        cp.wait()
        for t0 in range(ahead):
            issue(0, t0, t0, 0, PEER_SLOTS)

    hnext[...] = _norm_mod(xn_ref[...], nw_ref[...], modn_ref, 3).astype(BF16)
    g2 = mod_ref[0, :, 5 * D_MODEL:6 * D_MODEL]
    lane = lax.broadcasted_iota(jnp.int32, (PEER_SLOTS, tb), 1)

    def consume(t, n, issue_part):
        pltpu.make_async_copy(uvw_hbm.at[pl.ds(0, 2)], bufs[n], gsem.at[n]).wait()
        xrow = hbuf[pl.ds(t, 1), :]
        acc = jnp.zeros((PEER_SLOTS, LANES), F32)
        for s in range(ROW_TILES):
            issue_part(s)
            us = bufs[n][0, pl.ds(s, PEER_SLOTS, stride=ROW_TILES), :]
            acc = acc + us * xrow[:, s * LANES:(s + 1) * LANES]
        act = jnp.sum(acc, axis=1, keepdims=True)
        gcol = jnp.sum(jnp.where(lane == t, gate_s[cur], 0.0), axis=1, keepdims=True)
        coef = jnp.broadcast_to(gcol * _gelu(act), (PEER_SLOTS, LANES))
        outs = []
        for s in range(ROW_TILES):
            issue_part(ROW_TILES + s)
            vs = bufs[n][1, pl.ds(s, PEER_SLOTS, stride=ROW_TILES), :]
            outs.append(jnp.sum(vs * coef, axis=0, keepdims=True))
        orow = jnp.concatenate(outs, axis=1)
        o_ref[pl.ds(t, 1), :] = x_ref[pl.ds(t, 1), :] + g2 * orow

    def group(g):
        for n in range(PEER_NBUF):
            t = g * PEER_NBUF + n
            consume(t, n, lambda k, t=t, n=n: issue(cur, t + ahead, (n + ahead) % PEER_NBUF,
                                                    k * part, (k + 1) * part))

    more = i + 1 < nblk

    def piece_and_groups(k, carry):
        @pl.when(more)
        def _():
            select_piece(hnext[...], k, nxt)

        for gg in range(gpp):
            group(k * gpp + gg)
        return carry

    lax.fori_loop(0, PEER_HEADS - 1, piece_and_groups, 0)

    @pl.when(more)
    def _():
        select_piece(hnext[...], PEER_HEADS - 1, nxt)
        ids_to_smem(nxt).start()

    for gg in range(gpp - 1):
        group((PEER_HEADS - 1) * gpp + gg)

    @pl.when(more)
    def _():
        pltpu.make_async_copy(eid_t, idx_smem.at[nxt], isem.at[nxt]).wait()

    for n in range(PEER_NBUF):
        t = (ngroups - 1) * PEER_NBUF + n
        if n == 0:
            consume(t, n, lambda k, t=t: issue(cur, t + ahead, ahead, k * part, (k + 1) * part))
        else:
            @pl.when(more)
            def _():
                issue(nxt, n - 1, n - 1, 0, PEER_SLOTS)
            consume(t, n, lambda k: None)


def _peer(xs, nw, modsel, wq, keys, u_tab, v_tab, ctx_tiles):
    bsz, n, d = xs.shape
    ntok = bsz * n
    nblk = ntok // PEER_TB
    tiles_per_batch = n // PEER_TB
    ctx_blocks = ctx_tiles * ROWS // PEER_TB
    slab = PEER_SLOTS * ROW_TILES
    assert PEER_TB % (PEER_NBUF * PEER_HEADS) == 0
    ne = u_tab.shape[0]
    uv = jnp.stack([u_tab.reshape(ne, ROW_TILES, LANES), v_tab.reshape(ne, ROW_TILES, LANES)], axis=1)
    uvw = uv.reshape(-1, slab, LANES)
    wq3 = wq.reshape(d, PEER_HEADS, PEER_DK).transpose(1, 0, 2)
    nxt_blk = lambda i: jnp.minimum(i + 1, nblk - 1)
    mod_of = lambda i: (2 * (i // tiles_per_batch) + jnp.where(i % tiles_per_batch >= ctx_blocks, 1, 0), 0, 0)
    out = pl.pallas_call(
        functools.partial(_peer_kernel, nblk=nblk),
        grid=(nblk,),
        in_specs=[pl.BlockSpec((PEER_TB, d), lambda i: (i, 0)),
                  pl.BlockSpec((PEER_TB, d), lambda i: (nxt_blk(i), 0)),
                  pl.BlockSpec((1, d), lambda i: (0, 0)),
                  pl.BlockSpec((1, 1, 6 * D_MODEL), mod_of),
                  pl.BlockSpec((1, 1, 6 * D_MODEL), lambda i: mod_of(nxt_blk(i))),
                  pl.BlockSpec(wq3.shape, lambda i: (0, 0, 0)),
                  pl.BlockSpec(keys.shape, lambda i: (0, 0, 0)),
                  pl.BlockSpec(memory_space=pl.ANY),
                  pl.BlockSpec(memory_space=pl.ANY)],
        out_specs=pl.BlockSpec((PEER_TB, d), lambda i: (i, 0)),
        out_shape=jax.ShapeDtypeStruct((ntok, d), F32),
        scratch_shapes=[pltpu.VMEM((PEER_TB, d), F32),
                        pltpu.VMEM((PEER_TB, d), BF16),
                        pltpu.VMEM((PEER_SLOTS, PEER_TB), jnp.int32),
                        pltpu.VMEM((PEER_TB, PEER_SLOTS), jnp.int32),
                        pltpu.VMEM((2, PEER_SLOTS, PEER_TB), F32),
                        *[pltpu.VMEM((2, slab, LANES), F32) for _ in range(PEER_NBUF)],
                        pltpu.SMEM((2, PEER_TB, PEER_SLOTS), jnp.int32),
                        pltpu.SemaphoreType.DMA((PEER_NBUF,)),
                        pltpu.SemaphoreType.DMA((2,))],
        compiler_params=_cparams(("arbitrary",)),
        name="peer",
    )(xs.reshape(ntok, d), xs.reshape(ntok, d), nw, modsel, modsel, wq3, keys, uv, uvw)
    return out.reshape(bsz, n, d)


def _final_kernel(x_ref, w_ref, o_ref):
    x = x_ref[0]
    ms = jnp.mean(x * x, axis=-1, keepdims=True)
    o_ref[0] = x * lax.rsqrt(ms + EPS) * w_ref[...]


def _final_norm(xs, w, ctx_tiles, seq):
    bsz, n, d = xs.shape
    return pl.pallas_call(
        _final_kernel,
        grid=(bsz, seq // ROWS),
        in_specs=[pl.BlockSpec((1, ROWS, d), lambda b, i: (b, i + ctx_tiles, 0)),
                  pl.BlockSpec((1, d), lambda b, i: (0, 0))],
        out_specs=pl.BlockSpec((1, ROWS, d), lambda b, i: (b, i, 0)),
        out_shape=jax.ShapeDtypeStruct((bsz, seq, d), F32),
        compiler_params=_cparams(("parallel", "arbitrary")),
        name="final_norm",
    )(xs, w)


def _block_diag(w):
    nb, d, e = w.shape
    eye = jnp.eye(nb, dtype=w.dtype)
    return (eye[:, None, :, None] * w[:, :, None, :]).reshape(nb * d, nb * e)


def _mixer0(xs, modsel, norm1, w_in, w_out, i_bias, f_bias, mlstm_norm, conv_w, conv_b, dt_bias,
            a_log, d_skip, ssd_norm, ctx_tiles, nc_ctx):
    q0, k0, v0, o0, ig0, fg0, z0, xbc0, dt0, end = 0, 256, 512, 1024, 1536, 1544, 1552, 2064, 3088, 3104
    pad = jnp.zeros((D_MODEL, LANES - 32), w_in.dtype)
    w_cat = jnp.concatenate([w_in[:, q0:ig0], w_in[:, xbc0:dt0], w_in[:, z0:xbc0],
                             w_in[:, ig0:z0], w_in[:, dt0:end], pad], axis=1).astype(BF16)
    qkvo, zx, gates = _project(xs, norm1, modsel, w_cat, (1536, 1536, LANES), ctx_tiles)
    gates_row = jnp.swapaxes(gates[:, :, :32], 1, 2)
    bias = jnp.concatenate([i_bias.reshape(-1), f_bias.reshape(-1), dt_bias.reshape(-1)])
    bias_col = jnp.pad(bias, (0, LANES - 32)).reshape(1, LANES)
    bias_row = bias.reshape(32, 1)
    alog = jnp.concatenate([jnp.zeros((16,), F32), a_log.reshape(-1)])
    alog_col = jnp.pad(alog, (0, LANES - 32)).reshape(1, LANES)
    alog_row = alog.reshape(32, 1)
    xbc_act = _conv4(zx, 1024, conv_w, conv_b.reshape(1, -1), ctx_tiles, True)
    hf = _mlstm_scan(qkvo, gates, gates_row, bias_col, bias_row, nc_ctx, 0)
    hb = _mlstm_scan(qkvo, gates, gates_row, bias_col, bias_row, nc_ctx, 1)
    yf = _ssd_scan(xbc_act, gates, gates_row, bias_col, bias_row, alog_col, alog_row, nc_ctx, 0)
    yb = _ssd_scan(xbc_act, gates, gates_row, bias_col, bias_row, alog_col, alog_row, nc_ctx, 1)
    dskip = jnp.repeat(d_skip, B_HEADDIM).reshape(1, -1)
    return _finish0(hf, hb, yf, yb, qkvo, zx, xbc_act, xs, modsel, mlstm_norm.reshape(1, -1),
                    ssd_norm.reshape(1, -1), dskip, w_out.astype(BF16), ctx_tiles)


def _mixer1(xs, modsel, norm1, w_in, w_out, mu, w0, w_up, a0, a_up, g_up, k_k, k_a, r_k, ln_w, ln_b,
            conv_w, conv_b, lam, wa, ba, wi, bi, ctx_tiles, nc_ctx):
    pr, gate_br, x_br = _project(xs, norm1, modsel, w_in.astype(BF16), (RWKV_COLS, D_WIDTH, D_WIDTH),
                                 ctx_tiles)
    r, v, kk, g, w_f, w_b, kt_f, kt_b, al_f, al_b = _rwkv_prep(
        pr, mu.reshape(1, -1), w0, w_up, a0, a_up, g_up, k_k.reshape(1, -1), k_a.reshape(1, -1), ctx_tiles)
    yf, yb = _rwkv_scan(r, v, kk, w_f, kt_f, al_f, w_b, kt_b, al_b, nc_ctx)
    xc = _conv4(x_br, D_WIDTH, conv_w, conv_b.reshape(1, -1), ctx_tiles, False)
    w_gate = jnp.concatenate([_block_diag(wa[0]), _block_diag(wi[0]),
                              _block_diag(wa[1]), _block_diag(wi[1])], axis=1)
    b_gate = jnp.concatenate([ba[0], bi[0], ba[1], bi[1]]).reshape(1, -1)
    a_f, b_f, a_b, b_b = _lru_gates(xc, w_gate, b_gate, lam)
    uf = _lru_scan(a_f, b_f, nc_ctx, False)
    ub = _lru_scan(a_b, b_b, nc_ctx, True)
    return _finish1(yf, yb, uf, ub, r, kt_f, kt_b, v, g, gate_br, xs, modsel, r_k.reshape(1, -1),
                    ln_w.reshape(1, -1), ln_b.reshape(1, -1), w_out.astype(BF16), ctx_tiles)


def kernel(x, c, ctx, c_ctx, mod_w, mod_b, norm1, norm2, peer_wq, peer_keys, peer_u, peer_v, ev_w_in, ev_w_out, ev_mlstm_i_bias, ev_mlstm_f_bias, ev_mlstm_norm, ev_ssd_conv_w, ev_ssd_conv_b, ev_ssd_dt_bias, ev_ssd_a_log, ev_ssd_d, ev_ssd_norm, od_w_in, od_w_out, od_rwkv_mu, od_rwkv_w0, od_rwkv_w_up, od_rwkv_a0, od_rwkv_a_up, od_rwkv_g_up, od_rwkv_k_k, od_rwkv_k_a, od_rwkv_r_k, od_rwkv_ln_w, od_rwkv_ln_b, od_lru_conv_w, od_lru_conv_b, od_lru_lambda, od_lru_wa, od_lru_ba, od_lru_wi, od_lru_bi, final_norm):
    bsz, seq, d = x.shape
    ctx_len = ctx.shape[1]
    depth = mod_w.shape[0]
    assert d == D_MODEL and ctx_len == ROWS and seq % ROWS == 0 and bsz < SUBLANES
    ctx_tiles = ctx_len // ROWS
    nc_ctx = ctx_len // CHUNK
    xs = jnp.concatenate([ctx, x], axis=1)
    srows = jnp.concatenate([c, c_ctx[None, :], jnp.zeros((SUBLANES - bsz - 1, d), F32)], axis=0)
    for i in range(depth):
        mod = _modulation(srows, mod_w[i], mod_b[i].reshape(1, -1))
        modsel = jnp.stack([jnp.broadcast_to(mod[bsz], (bsz, 6 * d)), mod[:bsz]], axis=1)
        modsel = modsel.reshape(2 * bsz, 1, 6 * d)
        j = i // 2
        n1 = norm1[i].reshape(1, -1)
        if i % 2 == 0:
            xs = _mixer0(xs, modsel, n1, ev_w_in[j], ev_w_out[j], ev_mlstm_i_bias[j], ev_mlstm_f_bias[j],
                         ev_mlstm_norm[j], ev_ssd_conv_w[j], ev_ssd_conv_b[j], ev_ssd_dt_bias[j],
                         ev_ssd_a_log[j], ev_ssd_d[j], ev_ssd_norm[j], ctx_tiles, nc_ctx)
        else:
            xs = _mixer1(xs, modsel, n1, od_w_in[j], od_w_out[j], od_rwkv_mu[j], od_rwkv_w0[j],
                         od_rwkv_w_up[j], od_rwkv_a0[j], od_rwkv_a_up[j], od_rwkv_g_up[j], od_rwkv_k_k[j],
                         od_rwkv_k_a[j], od_rwkv_r_k[j].reshape(-1), od_rwkv_ln_w[j], od_rwkv_ln_b[j],
                         od_lru_conv_w[j], od_lru_conv_b[j], od_lru_lambda[j], od_lru_wa[j], od_lru_ba[j],
                         od_lru_wi[j], od_lru_bi[j], ctx_tiles, nc_ctx)
        keys = peer_keys[i].reshape(2 * PEER_HEADS, N_KEYS, PEER_DK // 2)
        xs = _peer(xs, norm2[i].reshape(1, -1), modsel, peer_wq[i].astype(BF16), keys,
                   peer_u[i], peer_v[i], ctx_tiles)
    return _final_norm(xs, final_norm.reshape(1, -1), ctx_tiles, seq)
```

```python
import functools
import math

import jax
import jax.numpy as jnp
from jax import lax
from jax.experimental import pallas as pl
from jax.experimental.pallas import tpu as pltpu

F32 = jnp.float32
BF16 = jnp.bfloat16
HIGHEST = lax.Precision.HIGHEST

D_MODEL = 1024
EPS = 1e-6
CHUNK = 128
ROWS = 256
GRID_W = 64
LANES = 128
SUBLANES = 8
MIB = 1024 * 1024

A_HEADS, A_DQK, A_DV = 4, 64, 128
A_WIDTH = A_HEADS * A_DV
B_HEADS, B_HEADDIM, B_GROUPS, B_STATE = 8, 64, 2, 128
B_WIDTH = B_HEADS * B_HEADDIM
C_HEADS, C_HEADDIM = 8, 64
C_WIDTH = C_HEADS * C_HEADDIM
C_LORA_W, C_LORA_A, C_LORA_G = 64, 64, 128
RWKV_COLS = 3 * C_WIDTH + C_LORA_W + C_LORA_A + C_LORA_G
RWKV_W_SCALE = math.exp(-0.5)
RWKV_GN_EPS = 64e-5
D_WIDTH = 512
LRU_C = 8.0
PEER_HEADS, PEER_DK, N_KEYS, PEER_TOPK = 8, 256, 128, 16
PEER_SLOTS = PEER_HEADS * PEER_TOPK
PEER_TB = 128
PEER_NBUF = 8
ROW_TILES = D_MODEL // LANES


def _cparams(sem, vmem_mib=48):
    return pltpu.CompilerParams(dimension_semantics=sem, vmem_limit_bytes=vmem_mib * MIB)


def _softplus(x):
    return jnp.maximum(x, 0.0) + jnp.log1p(jnp.exp(-jnp.abs(x)))


def _sigmoid(x):
    return 1.0 / (1.0 + jnp.exp(-x))


def _silu(x):
    return x * _sigmoid(x)


def _gelu(x):
    return 0.5 * x * (1.0 + lax.erf(x * (1.0 / math.sqrt(2.0))))


def _norm_mod(x, nw, mod_ref, slot):
    ms = jnp.mean(x * x, axis=-1, keepdims=True)
    y = x * lax.rsqrt(ms + EPS) * nw
    sh = mod_ref[0, :, slot * D_MODEL:(slot + 1) * D_MODEL]
    sc = mod_ref[0, :, (slot + 1) * D_MODEL:(slot + 2) * D_MODEL]
    return y * (1.0 + sc) + sh


def _chunk_order(c, nc_ctx, nc, reverse):
    if not reverse:
        return c
    return jnp.where(c < nc_ctx, nc_ctx - 1 - c, nc + nc_ctx - 1 - c)


def _mod_kernel(s_ref, w_ref, b_ref, o_ref):
    s = _silu(s_ref[...])
    o_ref[...] = jnp.dot(s, w_ref[...], precision=HIGHEST, preferred_element_type=F32) + b_ref[...]


def _modulation(srows, w, b):
    d = srows.shape[1]
    nt = w.shape[1] // d
    return pl.pallas_call(
        _mod_kernel,
        grid=(nt,),
        in_specs=[pl.BlockSpec((SUBLANES, d), lambda j: (0, 0)),
                  pl.BlockSpec((d, d), lambda j: (0, j)),
                  pl.BlockSpec((1, d), lambda j: (0, j))],
        out_specs=pl.BlockSpec((SUBLANES, d), lambda j: (0, j)),
        out_shape=jax.ShapeDtypeStruct((SUBLANES, w.shape[1]), F32),
        compiler_params=_cparams(("arbitrary",)),
        name="modulation",
    )(srows, w, b)


def _proj_kernel(x_ref, nw_ref, mod_ref, w_ref, *o_refs, widths):
    h = _norm_mod(x_ref[0], nw_ref[...], mod_ref, 0)
    out = jnp.dot(h.astype(BF16), w_ref[...], preferred_element_type=F32)
    off = 0
    for o_ref, wd in zip(o_refs, widths):
        o_ref[0] = out[:, off:off + wd]
        off += wd


def _mod_spec(ctx_tiles):
    return pl.BlockSpec((1, 1, 6 * D_MODEL),
                        lambda b, i: (2 * b + jnp.where(i >= ctx_tiles, 1, 0), 0, 0))


def _project(xs, nw, modsel, w, widths, ctx_tiles):
    bsz, n, d = xs.shape
    nt = n // ROWS
    return pl.pallas_call(
        functools.partial(_proj_kernel, widths=widths),
        grid=(bsz, nt),
        in_specs=[pl.BlockSpec((1, ROWS, d), lambda b, i: (b, i, 0)),
                  pl.BlockSpec((1, d), lambda b, i: (0, 0)),
                  _mod_spec(ctx_tiles),
                  pl.BlockSpec(w.shape, lambda b, i: (0, 0))],
        out_specs=[pl.BlockSpec((1, ROWS, wd), lambda b, i: (b, i, 0)) for wd in widths],
        out_shape=[jax.ShapeDtypeStruct((bsz, n, wd), F32) for wd in widths],
        compiler_params=_cparams(("parallel", "arbitrary")),
        name="norm_mod_project",
    )(xs, nw, modsel, w)


def _conv_kernel(x_ref, p_ref, n_ref, w_ref, b_ref, o_ref, *, ctx_tiles, ntiles, act):
    i = pl.program_id(1)
    x = x_ref[0]
    rows = x.shape[0]
    prev_ok = jnp.logical_and(i != 0, i != ctx_tiles)
    next_ok = jnp.logical_and(i != ctx_tiles - 1, i != ntiles - 1)
    p = jnp.where(prev_ok, p_ref[0], 0.0)
    nx = jnp.where(next_ok, n_ref[0], 0.0)
    row = lax.broadcasted_iota(jnp.int32, (rows, 1), 0)
    xm1 = jnp.where(row == 0, p[7:8], pltpu.roll(x, 1, 0))
    xm2 = jnp.where(row == 0, p[6:7], jnp.where(row == 1, p[7:8], pltpu.roll(x, 2, 0)))
    xp1 = jnp.where(row == rows - 1, nx[0:1], pltpu.roll(x, rows - 1, 0))
    w = w_ref[...]
    y = b_ref[...] + xm2 * w[0:1] + xm1 * w[1:2] + x * w[2:3] + xp1 * w[3:4]
    o_ref[0] = _silu(y) if act else y


def _conv4(x, width, w, b, ctx_tiles, act):
    bsz, n, _ = x.shape
    nt = n // ROWS
    hb = ROWS // SUBLANES
    nh = n // SUBLANES
    return pl.pallas_call(
        functools.partial(_conv_kernel, ctx_tiles=ctx_tiles, ntiles=nt, act=act),
        grid=(bsz, nt),
        in_specs=[pl.BlockSpec((1, ROWS, width), lambda b_, i: (b_, i, 0)),
                  pl.BlockSpec((1, SUBLANES, width), lambda b_, i: (b_, jnp.maximum(i * hb - 1, 0), 0)),
                  pl.BlockSpec((1, SUBLANES, width),
                               lambda b_, i: (b_, jnp.minimum((i + 1) * hb, nh - 1), 0)),
                  pl.BlockSpec((4, width), lambda b_, i: (0, 0)),
                  pl.BlockSpec((1, width), lambda b_, i: (0, 0))],
        out_specs=pl.BlockSpec((1, ROWS, width), lambda b_, i: (b_, i, 0)),
        out_shape=jax.ShapeDtypeStruct((bsz, n, width), F32),
        compiler_params=_cparams(("parallel", "arbitrary")),
        name="conv4",
    )(x, x, x, w, b)


def _tri(reverse):
    t = lax.broadcasted_iota(jnp.int32, (CHUNK, CHUNK), 0)
    s = lax.broadcasted_iota(jnp.int32, (CHUNK, CHUNK), 1)
    return (s >= t) if reverse else (s <= t)


def _cumsums(col, row, mask):
    mf = mask.astype(F32)
    b_col = jnp.dot(mf, col, precision=HIGHEST, preferred_element_type=F32)
    b_row = lax.dot_general(row, mf, (((1,), (1,)), ((), ())), precision=HIGHEST,
                            preferred_element_type=F32)
    return b_col, b_row


def _mlstm_kernel(qkv_ref, gc_ref, gr_ref, bc_ref, br_ref, o_ref, ct_ref, n_ref, m_ref, *,
                  direction):
    c = pl.program_id(1)

    @pl.when(c == 0)
    def _():
        ct_ref[...] = jnp.zeros_like(ct_ref)
        n_ref[...] = jnp.zeros_like(n_ref)
        m_ref[...] = jnp.zeros_like(m_ref)

    reverse = direction == 1
    mask = _tri(reverse)
    gc = gc_ref[0] + bc_ref[...]
    gr = gr_ref[0] + br_ref[...]
    d4 = direction * A_HEADS
    li_col = gc[:, d4:d4 + A_HEADS]
    lf_col = -_softplus(-gc[:, 8 + d4:8 + d4 + A_HEADS])
    li_row = gr[d4:d4 + A_HEADS, :]
    lf_row = -_softplus(-gr[8 + d4:8 + d4 + A_HEADS, :])
    b_col, b_row = _cumsums(lf_col, lf_row, mask)
    tot = jnp.sum(lf_col, axis=0, keepdims=True)
    outs = []
    for h in range(A_HEADS):
        q = qkv_ref[0, :, h * A_DQK:(h + 1) * A_DQK] * (A_DQK ** -0.5)
        k = qkv_ref[0, :, A_HEADS * A_DQK + h * A_DQK:A_HEADS * A_DQK + (h + 1) * A_DQK]
        v = qkv_ref[0, :, 2 * A_HEADS * A_DQK + h * A_DV:2 * A_HEADS * A_DQK + (h + 1) * A_DV]
        bc, br = b_col[:, h:h + 1], b_row[h:h + 1, :]
        ic, ir = li_col[:, h:h + 1], li_row[h:h + 1, :]
        th = tot[:, h:h + 1]
        m_prev = m_ref[h]
        dmat = jnp.where(mask, bc - br + ir, -jnp.inf)
        inter = bc + m_prev
        m_t = jnp.maximum(inter, jnp.max(dmat, axis=1, keepdims=True))
        qk = lax.dot_general(q, k, (((1,), (1,)), ((), ())), preferred_element_type=F32)
        s = qk * jnp.exp(dmat - m_t)
        w_inter = jnp.exp(inter - m_t)
        num = jnp.dot(s, v, preferred_element_type=F32) \
            + w_inter * jnp.dot(q, ct_ref[h], preferred_element_type=F32)
        den = jnp.sum(s, axis=1, keepdims=True) \
            + w_inter * jnp.sum(q * n_ref[h], axis=1, keepdims=True)
        outs.append(num / jnp.maximum(jnp.abs(den), jnp.exp(-m_t)))
        g = th - bc + ic
        m_new = jnp.maximum(th + m_prev, jnp.max(g, axis=0, keepdims=True))
        wg = jnp.exp(g - m_new)
        wc = jnp.exp(th + m_prev - m_new)
        ct_ref[h] = wc * ct_ref[h] + lax.dot_general(k * wg, v, (((0,), (0,)), ((), ())),
                                                     preferred_element_type=F32)
        n_ref[h] = wc * n_ref[h] + jnp.sum(wg * k, axis=0, keepdims=True)
        m_ref[h] = m_new
    o_ref[0] = jnp.concatenate(outs, axis=1)


def _mlstm_scan(qkvo, gates, gates_row, bias_col, bias_row, nc_ctx, direction):
    bsz, n, _ = qkvo.shape
    nc = n // CHUNK
    reverse = direction == 1
    order = lambda c: _chunk_order(c, nc_ctx, nc, reverse)
    return pl.pallas_call(
        functools.partial(_mlstm_kernel, direction=direction),
        grid=(bsz, nc),
        in_specs=[pl.BlockSpec((1, CHUNK, 2 * A_HEADS * A_DQK + A_WIDTH), lambda b, c: (b, order(c), 0)),
                  pl.BlockSpec((1, CHUNK, LANES), lambda b, c: (b, order(c), 0)),
                  pl.BlockSpec((1, 32, CHUNK), lambda b, c: (b, 0, order(c))),
                  pl.BlockSpec((1, LANES), lambda b, c: (0, 0)),
                  pl.BlockSpec((32, 1), lambda b, c: (0, 0))],
        out_specs=pl.BlockSpec((1, CHUNK, A_WIDTH), lambda b, c: (b, order(c), 0)),
        out_shape=jax.ShapeDtypeStruct((bsz, n, A_WIDTH), F32),
        scratch_shapes=[pltpu.VMEM((A_HEADS, A_DQK, A_DV), F32),
                        pltpu.VMEM((A_HEADS, 1, A_DQK), F32),
                        pltpu.VMEM((A_HEADS, 1, 1), F32)],
        compiler_params=_cparams(("parallel", "arbitrary")),
        name="mlstm_scan",
    )(qkvo, gates, gates_row, bias_col, bias_row)


def _ssd_kernel(xbc_ref, gc_ref, gr_ref, bc_ref, br_ref, ac_ref, ar_ref, o_ref, st_ref, *,
                direction):
    c = pl.program_id(1)

    @pl.when(c == 0)
    def _():
        st_ref[...] = jnp.zeros_like(st_ref)

    reverse = direction == 1
    mask = _tri(reverse)
    gc = gc_ref[0] + bc_ref[...]
    gr = gr_ref[0] + br_ref[...]
    d8 = 16 + direction * B_HEADS
    dt_col = _softplus(gc[:, d8:d8 + B_HEADS])
    dt_row = _softplus(gr[d8:d8 + B_HEADS, :])
    la_col = -dt_col * jnp.exp(ac_ref[:, d8:d8 + B_HEADS])
    la_row = -dt_row * jnp.exp(ar_ref[d8:d8 + B_HEADS, :])
    b_col, b_row = _cumsums(la_col, la_row, mask)
    tot = jnp.sum(la_col, axis=0, keepdims=True)
    outs = []
    hpg = B_HEADS // B_GROUPS
    for g in range(B_GROUPS):
        bm = xbc_ref[0, :, B_WIDTH + g * B_STATE:B_WIDTH + (g + 1) * B_STATE]
        cm = xbc_ref[0, :, B_WIDTH + (B_GROUPS + g) * B_STATE:B_WIDTH + (B_GROUPS + g + 1) * B_STATE]
        cb = lax.dot_general(cm, bm, (((1,), (1,)), ((), ())), preferred_element_type=F32)
        for h in range(g * hpg, (g + 1) * hpg):
            xh = xbc_ref[0, :, h * B_HEADDIM:(h + 1) * B_HEADDIM]
            bc, br = b_col[:, h:h + 1], b_row[h:h + 1, :]
            th = tot[:, h:h + 1]
            decay = jnp.exp(jnp.where(mask, bc - br, -jnp.inf))
            s = cb * decay * dt_row[h:h + 1, :]
            y = jnp.dot(s, xh, preferred_element_type=F32) \
                + jnp.exp(bc) * jnp.dot(cm, st_ref[h], preferred_element_type=F32)
            outs.append(y)
            w_end = jnp.exp(th - bc) * dt_col[:, h:h + 1]
            st_ref[h] = jnp.exp(th) * st_ref[h] + lax.dot_general(
                bm * w_end, xh, (((0,), (0,)), ((), ())), preferred_element_type=F32)
    o_ref[0] = jnp.concatenate(outs, axis=1)


def _ssd_scan(xbc, gates, gates_row, bias_col, bias_row, alog_col, alog_row, nc_ctx, direction):
    bsz, n, width = xbc.shape
    nc = n // CHUNK
    reverse = direction == 1
    order = lambda c: _chunk_order(c, nc_ctx, nc, reverse)
    return pl.pallas_call(
        functools.partial(_ssd_kernel, direction=direction),
        grid=(bsz, nc),
        in_specs=[pl.BlockSpec((1, CHUNK, width), lambda b, c: (b, order(c), 0)),
                  pl.BlockSpec((1, CHUNK, LANES), lambda b, c: (b, order(c), 0)),
                  pl.BlockSpec((1, 32, CHUNK), lambda b, c: (b, 0, order(c))),
                  pl.BlockSpec((1, LANES), lambda b, c: (0, 0)),
                  pl.BlockSpec((32, 1), lambda b, c: (0, 0)),
                  pl.BlockSpec((1, LANES), lambda b, c: (0, 0)),
                  pl.BlockSpec((32, 1), lambda b, c: (0, 0))],
        out_specs=pl.BlockSpec((1, CHUNK, B_WIDTH), lambda b, c: (b, order(c), 0)),
        out_shape=jax.ShapeDtypeStruct((bsz, n, B_WIDTH), F32),
        scratch_shapes=[pltpu.VMEM((B_HEADS, B_STATE, B_HEADDIM), F32)],
        compiler_params=_cparams(("parallel", "arbitrary")),
        name="ssd_scan",
    )(xbc, gates, gates_row, bias_col, bias_row, alog_col, alog_row)


def _finish0_kernel(hf_ref, hb_ref, yf_ref, yb_ref, o_ref_in, z_ref, xs_ref, x_ref, mod_ref,
                    mn_ref, sn_ref, dsk_ref, w_ref, out_ref):
    h = hf_ref[0] + hb_ref[0]
    parts = []
    for hd in range(A_HEADS):
        hh = h[:, hd * A_DV:(hd + 1) * A_DV]
        ms = jnp.mean(hh * hh, axis=-1, keepdims=True)
        parts.append(hh * lax.rsqrt(ms + EPS))
    hn = jnp.concatenate(parts, axis=1) * mn_ref[...]
    ya = _sigmoid(o_ref_in[0]) * hn
    y = yf_ref[0] + yb_ref[0] + dsk_ref[...] * xs_ref[0]
    t = y * _silu(z_ref[0])
    ms = jnp.mean(t * t, axis=-1, keepdims=True)
    yb = t * lax.rsqrt(ms + EPS) * sn_ref[...]
    f = jnp.concatenate([ya, yb], axis=1)
    g1 = mod_ref[0, :, 2 * D_MODEL:3 * D_MODEL]
    out_ref[0] = x_ref[0] + g1 * jnp.dot(f.astype(BF16), w_ref[...], preferred_element_type=F32)


def _finish0(hf, hb, yf, yb, qkvo, zx, xbc_act, xs, modsel, mnorm, snorm, dskip, w_out, ctx_tiles):
    bsz, n, d = xs.shape
    nt = n // ROWS
    half = lambda j: pl.BlockSpec((1, ROWS, 512), lambda b, i, j=j: (b, i, j))
    vec = pl.BlockSpec((1, 512), lambda b, i: (0, 0))
    return pl.pallas_call(
        _finish0_kernel,
        grid=(bsz, nt),
        in_specs=[half(0), half(0), half(0), half(0), half(2), half(2), half(0),
                  pl.BlockSpec((1, ROWS, d), lambda b, i: (b, i, 0)),
                  _mod_spec(ctx_tiles), vec, vec, vec,
                  pl.BlockSpec((d, d), lambda b, i: (0, 0))],
        out_specs=pl.BlockSpec((1, ROWS, d), lambda b, i: (b, i, 0)),
        out_shape=jax.ShapeDtypeStruct((bsz, n, d), F32),
        compiler_params=_cparams(("parallel", "arbitrary")),
        name="finish_mlstm_ssd",
    )(hf, hb, yf, yb, qkvo, zx, xbc_act, xs, modsel, mnorm, snorm, dskip, w_out)


def _head_ones(n, hd):
    r = lax.broadcasted_iota(jnp.int32, (n, n), 0) // hd
    c = lax.broadcasted_iota(jnp.int32, (n, n), 1) // hd
    return (r == c).astype(F32)


def _rwkv_prep_kernel(pr_ref, p_ref, n_ref, mu_ref, w0_ref, wup_ref, a0_ref, aup_ref, gup_ref,
                      kk_ref, ka_ref, r_o, v_o, kk_o, g_o, w0_o, w1_o, kt0_o, kt1_o, al0_o, al1_o,
                      *, ctx_tiles, ntiles):
    i = pl.program_id(1)
    pr = pr_ref[0]
    rows, cols = pr.shape
    q = cols // 4
    row = lax.broadcasted_iota(jnp.int32, (rows, 1), 0)
    col = lax.broadcasted_iota(jnp.int32, (1, cols), 1)
    is_ctx = i < ctx_tiles
    l1 = pltpu.roll(pr, 1, 0)
    r1 = pltpu.roll(pr, rows - 1, 0)
    left = jnp.where(row % GRID_W == 0, 0.0, l1)
    right = jnp.where(row % GRID_W == GRID_W - 1, 0.0, r1)
    up = jnp.concatenate([p_ref[0], pr[:rows - GRID_W]], axis=0)
    up = jnp.where(jnp.logical_and(i == ctx_tiles, row < GRID_W), 0.0, up)
    down = jnp.concatenate([pr[GRID_W:], n_ref[0]], axis=0)
    down = jnp.where(jnp.logical_and(i == ntiles - 1, row >= rows - GRID_W), 0.0, down)
    grid_sh = jnp.where(col < q, left, jnp.where(col < 2 * q, right, jnp.where(col < 3 * q, up, down)))
    prev = jnp.where(row == 0, 0.0, l1)
    nxt = jnp.where(row == rows - 1, 0.0, r1)
    seq_sh = jnp.where(col < 2 * q, prev, nxt)
    shifted = jnp.where(is_ctx, seq_sh, grid_sh)
    pr = pr + mu_ref[...] * (shifted - pr)
    r = pr[:, 0:C_WIDTH]
    k = pr[:, C_WIDTH:2 * C_WIDTH]
    v = pr[:, 2 * C_WIDTH:3 * C_WIDTH]
    o = 3 * C_WIDTH
    wd = pr[:, o:o + C_LORA_W]
    ad = pr[:, o + C_LORA_W:o + C_LORA_W + C_LORA_A]
    gd = pr[:, o + C_LORA_W + C_LORA_A:]
    kk = k * kk_ref[...]
    ss = jnp.dot(kk * kk, _head_ones(C_WIDTH, C_HEADDIM), precision=HIGHEST, preferred_element_type=F32)
    kk = kk * lax.rsqrt(ss + 1e-12)
    tw = jnp.tanh(wd)
    for d, (w_o, kt_o, al_o) in enumerate(((w0_o, kt0_o, al0_o), (w1_o, kt1_o, al1_o))):
        logw = -RWKV_W_SCALE * _sigmoid(w0_ref[d:d + 1, :] + jnp.dot(tw, wup_ref[d], preferred_element_type=F32))
        a = _sigmoid(a0_ref[d:d + 1, :] + jnp.dot(ad, aup_ref[d], preferred_element_type=F32))
        w_o[0] = jnp.exp(logw)
        kt_o[0] = k * (1.0 + (a - 1.0) * ka_ref[...])
        al_o[0] = a
    r_o[0] = r
    v_o[0] = v
    kk_o[0] = kk
    g_o[0] = jnp.dot(_sigmoid(gd), gup_ref[...], preferred_element_type=F32)


def _rwkv_prep(pr, mu, w0, w_up, a0, a_up, g_up, k_k, k_a, ctx_tiles):
    bsz, n, cols = pr.shape
    nt = n // ROWS
    hb = ROWS // GRID_W
    nh = n // GRID_W
    full = lambda a: pl.BlockSpec(a.shape, lambda b, i, nd=a.ndim: (0,) * nd)
    out = pl.BlockSpec((1, ROWS, C_WIDTH), lambda b, i: (b, i, 0))
    return pl.pallas_call(
        functools.partial(_rwkv_prep_kernel, ctx_tiles=ctx_tiles, ntiles=nt),
        grid=(bsz, nt),
        in_specs=[pl.BlockSpec((1, ROWS, cols), lambda b, i: (b, i, 0)),
                  pl.BlockSpec((1, GRID_W, cols), lambda b, i: (b, jnp.maximum(i * hb - 1, 0), 0)),
                  pl.BlockSpec((1, GRID_W, cols), lambda b, i: (b, jnp.minimum((i + 1) * hb, nh - 1), 0)),
                  full(mu), full(w0), full(w_up), full(a0), full(a_up), full(g_up), full(k_k), full(k_a)],
        out_specs=[out] * 10,
        out_shape=[jax.ShapeDtypeStruct((bsz, n, C_WIDTH), F32)] * 10,
        compiler_params=_cparams(("parallel", "arbitrary")),
        name="rwkv_prep",
    )(pr, pr, pr, mu, w0, w_up, a0, a_up, g_up, k_k, k_a)


RWKV_UNROLL = 8


def _rwkv_kernel(rf, wf, kf, vf, kkf, alf, rb, wb, kb, vb, kkb, alb, of_ref, ob_ref, s_ref, y_ref):
    c = pl.program_id(0)

    @pl.when(c == 0)
    def _():
        s_ref[...] = jnp.zeros_like(s_ref)

    nb = rf.shape[0]
    t_len = rf.shape[1]
    nt = C_WIDTH // LANES
    sub = lax.broadcasted_iota(jnp.int32, (C_HEADDIM, C_WIDTH), 0)
    lane = lax.broadcasted_iota(jnp.int32, (C_HEADDIM, C_WIDTH), 1)
    diag = (lane % C_HEADDIM == sub).astype(F32)
    ones = _head_ones(LANES, C_HEADDIM).astype(BF16)
    ones2 = jnp.concatenate([ones, ones], axis=0)
    chains = [(refs, b) for refs in ((rf, wf, kf, vf, kkf, alf, of_ref, False),
                                     (rb, wb, kb, vb, kkb, alb, ob_ref, True)) for b in range(nb)]

    def head_sum(xs, split):
        x = jnp.concatenate([a[:, j * LANES:(j + 1) * LANES] for a in xs for j in range(nt)], axis=0)
        if split:
            hi = x.astype(BF16)
            lo = (x - hi.astype(F32)).astype(BF16)
            out = jnp.dot(jnp.concatenate([hi, lo], axis=1), ones2, preferred_element_type=F32)
        else:
            out = jnp.dot(x.astype(BF16), ones, preferred_element_type=F32)
        res = []
        for n in range(len(xs)):
            res.append(jnp.concatenate(
                [out[(nt * n + j) * C_HEADDIM:(nt * n + j + 1) * C_HEADDIM] for j in range(nt)], axis=1))
        return res

    def step(i, states):
        rows = []
        for refs, b in chains:
            t = (t_len - 1 - i) if refs[7] else i
            rows.append([ref[b, pl.ds(t, 1), :] for ref in refs[:6]] + [t])
        sa = [head_sum([s * (-row[4])], True)[0] for s, row in zip(states, rows)]
        vcol = [head_sum([diag * row[3]], False)[0] for row in rows]
        new = [s * row[1] + a * (row[4] * row[5]) + vc * row[2]
               for s, row, a, vc in zip(states, rows, sa, vcol)]
        ys = [head_sum([s * row[0]], False)[0] for s, row in zip(new, rows)]
        for n, (row, y) in enumerate(zip(rows, ys)):
            pltpu.store(y_ref.at[n], y, mask=lane % C_HEADDIM == row[6] % C_HEADDIM)
        return new

    nblk = t_len // C_HEADDIM

    def flush(walk):
        for n, (refs, b) in enumerate(chains):
            blk = (nblk - 1 - walk) if refs[7] else walk
            parts = []
            for j in range(nt):
                tr = y_ref[n, :, j * LANES:(j + 1) * LANES].T
                parts += [tr[:C_HEADDIM], tr[C_HEADDIM:]]
            refs[6][b, blk * C_HEADDIM:(blk + 1) * C_HEADDIM, :] = jnp.concatenate(parts, axis=1)

    def body(g, states):
        states = list(states)
        for u in range(RWKV_UNROLL):
            states = step(g * RWKV_UNROLL + u, states)
        return tuple(states)

    per_blk = C_HEADDIM // RWKV_UNROLL
    states = tuple(s_ref[n] for n in range(len(chains)))
    for walk in range(nblk):
        states = lax.fori_loop(walk * per_blk, (walk + 1) * per_blk, body, states)
        flush(walk)
    for n in range(len(chains)):
        s_ref[n] = states[n]


def _rwkv_scan(r, v, kk, w_f, kt_f, al_f, w_b, kt_b, al_b, nc_ctx):
    bsz, n, width = r.shape
    nc = n // CHUNK
    fwd = pl.BlockSpec((bsz, CHUNK, width), lambda c: (0, c, 0))
    bwd = pl.BlockSpec((bsz, CHUNK, width), lambda c: (0, _chunk_order(c, nc_ctx, nc, True), 0))
    return pl.pallas_call(
        _rwkv_kernel,
        grid=(nc,),
        in_specs=[fwd] * 6 + [bwd] * 6,
        out_specs=[fwd, bwd],
        out_shape=[jax.ShapeDtypeStruct((bsz, n, width), F32)] * 2,
        scratch_shapes=[pltpu.VMEM((2 * bsz, C_HEADDIM, width), F32),
                        pltpu.VMEM((2 * bsz, C_HEADDIM, width), F32)],
        compiler_params=_cparams(("arbitrary",)),
        name="rwkv_scan",
    )(r, w_f, kt_f, v, kk, al_f, r, w_b, kt_b, v, kk, al_b)


def _lru_gate_kernel(xc_ref, w_ref, b_ref, lam_ref, a0_o, b0_o, a1_o, b1_o):
    xc = xc_ref[0]
    z = jnp.dot(xc, w_ref[...], preferred_element_type=F32) + b_ref[...]
    for d, (a_o, b_o) in enumerate(((a0_o, b0_o), (a1_o, b1_o))):
        gr = _sigmoid(z[:, 2 * d * D_WIDTH:(2 * d + 1) * D_WIDTH])
        gi = _sigmoid(z[:, (2 * d + 1) * D_WIDTH:(2 * d + 2) * D_WIDTH])
        log_a = -LRU_C * gr * _softplus(-lam_ref[d:d + 1, :])
        th = jnp.tanh(log_a)
        one_minus_a2 = -2.0 * th / (1.0 - th)
        a_o[0] = jnp.exp(log_a)
        b_o[0] = jnp.sqrt(one_minus_a2) * (gi * xc)


def _lru_gates(xc, w, b, lam):
    bsz, n, width = xc.shape
    nt = n // ROWS
    out = pl.BlockSpec((1, ROWS, width), lambda b_, i: (b_, i, 0))
    return pl.pallas_call(
        _lru_gate_kernel,
        grid=(bsz, nt),
        in_specs=[out,
                  pl.BlockSpec(w.shape, lambda b_, i: (0, 0)),
                  pl.BlockSpec(b.shape, lambda b_, i: (0, 0)),
                  pl.BlockSpec(lam.shape, lambda b_, i: (0, 0))],
        out_specs=[out] * 4,
        out_shape=[jax.ShapeDtypeStruct((bsz, n, width), F32)] * 4,
        compiler_params=_cparams(("parallel", "arbitrary")),
        name="lru_gates",
    )(xc, w, b, lam)


def _lru_kernel(a_ref, b_ref, o_ref, h_ref, *, reverse):
    c = pl.program_id(0)

    @pl.when(c == 0)
    def _():
        h_ref[...] = jnp.zeros_like(h_ref)

    nb = a_ref.shape[0]
    t_len = a_ref.shape[1]

    def step(i, hs):
        t = (t_len - 1 - i) if reverse else i
        new = []
        for b in range(nb):
            h = a_ref[b, pl.ds(t, 1), :] * hs[b] + b_ref[b, pl.ds(t, 1), :]
            o_ref[b, pl.ds(t, 1), :] = h
            new.append(h)
        return tuple(new)

    hs = lax.fori_loop(0, t_len, step, tuple(h_ref[b] for b in range(nb)))
    for b in range(nb):
        h_ref[b] = hs[b]


def _lru_scan(a, b, nc_ctx, reverse):
    bsz, n, width = a.shape
    nc = n // CHUNK
    order = lambda c: _chunk_order(c, nc_ctx, nc, reverse)
    spec = pl.BlockSpec((bsz, CHUNK, width), lambda c: (0, order(c), 0))
    return pl.pallas_call(
        functools.partial(_lru_kernel, reverse=reverse),
        grid=(nc,),
        in_specs=[spec, spec],
        out_specs=spec,
        out_shape=jax.ShapeDtypeStruct((bsz, n, width), F32),
        scratch_shapes=[pltpu.VMEM((bsz, 1, width), F32)],
        compiler_params=_cparams(("arbitrary",)),
        name="lru_scan",
    )(a, b)


def _finish1_kernel(yf_ref, yb_ref, uf_ref, ub_ref, r_ref, kt0_ref, kt1_ref, v_ref, g_ref, gb_ref,
                    x_ref, mod_ref, rk_ref, lw_ref, lb_ref, w_ref, out_ref):
    y = yf_ref[0] + yb_ref[0]
    hs = _head_ones(C_WIDTH, C_HEADDIM)
    mean = jnp.dot(y, hs, precision=HIGHEST, preferred_element_type=F32) * (1.0 / C_HEADDIM)
    yc = y - mean
    var = jnp.dot(yc * yc, hs, precision=HIGHEST, preferred_element_type=F32) * (1.0 / C_HEADDIM)
    yn = yc * lax.rsqrt(var + RWKV_GN_EPS) * lw_ref[...] + lb_ref[...]
    kb = 0.5 * (kt0_ref[0] + kt1_ref[0])
    bonus = jnp.dot(r_ref[0] * kb * rk_ref[...], hs, precision=HIGHEST, preferred_element_type=F32)
    yn = yn + bonus * v_ref[0]
    yc_ = yn * g_ref[0]
    yd = (uf_ref[0] + ub_ref[0]) * _gelu(gb_ref[0])
    f = jnp.concatenate([yc_, yd], axis=1)
    g1 = mod_ref[0, :, 2 * D_MODEL:3 * D_MODEL]
    out_ref[0] = x_ref[0] + g1 * jnp.dot(f.astype(BF16), w_ref[...], preferred_element_type=F32)


def _finish1(yf, yb, uf, ub, r, kt0, kt1, v, g, gate_br, xs, modsel, r_k, ln_w, ln_b, w_out, ctx_tiles):
    bsz, n, d = xs.shape
    nt = n // ROWS
    half = pl.BlockSpec((1, ROWS, 512), lambda b, i: (b, i, 0))
    vec = pl.BlockSpec((1, 512), lambda b, i: (0, 0))
    return pl.pallas_call(
        _finish1_kernel,
        grid=(bsz, nt),
        in_specs=[half] * 10 + [pl.BlockSpec((1, ROWS, d), lambda b, i: (b, i, 0)),
                                _mod_spec(ctx_tiles), vec, vec, vec,
                                pl.BlockSpec((d, d), lambda b, i: (0, 0))],
        out_specs=pl.BlockSpec((1, ROWS, d), lambda b, i: (b, i, 0)),
        out_shape=jax.ShapeDtypeStruct((bsz, n, d), F32),
        compiler_params=_cparams(("parallel", "arbitrary")),
        name="finish_rwkv_lru",
    )(yf, yb, uf, ub, r, kt0, kt1, v, g, gate_br, xs, modsel, r_k, ln_w, ln_b, w_out)


def _max_arg(s):
    tiles = s.shape[0] // SUBLANES
    vals = [s[i * SUBLANES:(i + 1) * SUBLANES] for i in range(tiles)]
    row = lax.broadcasted_iota(jnp.int32, (SUBLANES, s.shape[1]), 0)
    ids = [row + i * SUBLANES for i in range(tiles)]
    while len(vals) > 1:
        nv, ni = [], []
        for a in range(0, len(vals) - 1, 2):
            take = vals[a + 1] > vals[a]
            nv.append(jnp.where(take, vals[a + 1], vals[a]))
            ni.append(jnp.where(take, ids[a + 1], ids[a]))
        if len(vals) % 2:
            nv.append(vals[-1])
            ni.append(ids[-1])
        vals, ids = nv, ni
    mx = jnp.max(vals[0], axis=0, keepdims=True)
    am = jnp.min(jnp.where(vals[0] == mx, ids[0], s.shape[0]), axis=0, keepdims=True)
    return mx, am


def _topk_rows(s, payload=None):
    iota = lax.broadcasted_iota(jnp.int32, s.shape, 0)
    vals, idxs = [], []
    for _ in range(PEER_TOPK):
        mx, am = _max_arg(s)
        sel = iota == am
        vals.append(mx)
        if payload is None:
            idxs.append(am)
        else:
            idxs.append(jnp.max(jnp.where(sel, payload, -1), axis=0, keepdims=True))
        s = jnp.where(sel, -jnp.inf, s)
    return vals, idxs


def _peer_candidates(sv0, si0, sv1, si1):
    grp = SUBLANES
    v1_all, i1_all = jnp.concatenate(sv1, axis=0), jnp.concatenate(si1, axis=0)
    v1_lo, i1_lo = v1_all[:grp], i1_all[:grp]
    row = lax.broadcasted_iota(jnp.int32, v1_lo.shape, 0)
    cand, cidx = [sv0[0] + v1_all], [si0[0] * N_KEYS + i1_all]
    for a in range(1, grp):
        cand.append(jnp.where(row < PEER_TOPK // (a + 1), sv0[a] + v1_lo, -jnp.inf))
        cidx.append(si0[a] * N_KEYS + i1_lo)
    cand.append(jnp.concatenate(sv0[grp:], axis=0) + sv1[0])
    cidx.append(jnp.concatenate(si0[grp:], axis=0) * N_KEYS + si1[0])
    return jnp.concatenate(cand, axis=0), jnp.concatenate(cidx, axis=0)


def _select_head(hn, wq_ref, keys_ref, k):
    half = PEER_DK // 2
    q = jnp.dot(hn, wq_ref[k], preferred_element_type=F32)
    sv, si = [], []
    for p in range(2):
        s = lax.dot_general(keys_ref[2 * k + p], q[:, p * half:(p + 1) * half],
                            (((1,), (1,)), ((), ())), preferred_element_type=F32)
        v, ix = _topk_rows(s)
        sv.append(v)
        si.append(ix)
    cand, cidx = _peer_candidates(sv[0], si[0], sv[1], si[1])
    best, eid = _topk_rows(cand, cidx)
    e = jnp.exp(jnp.concatenate(best, axis=0) - best[0])
    return jnp.concatenate(eid, axis=0), e / jnp.sum(e, axis=0, keepdims=True)


def _peer_kernel(x_ref, xn_ref, nw_ref, mod_ref, modn_ref, wq_ref, keys_ref, fw_ref, uv_hbm, uvw_hbm, o_ref,
                 hbuf, hnext, eid_v, eid_t, gate_s, *scratch, nblk, layer, final):
    bufs = scratch[:PEER_NBUF]
    idx_smem, gsem, isem = scratch[PEER_NBUF:]
    i = pl.program_id(0)
    tb = x_ref.shape[0]
    ahead = PEER_NBUF - 1
    cur = i % 2
    nxt = 1 - cur
    ngroups = tb // PEER_NBUF
    gpp = ngroups // PEER_HEADS
    part = PEER_SLOTS // (2 * ROW_TILES)

    def select_piece(hn, k, slot):
        eid, gate = _select_head(hn, wq_ref, keys_ref, k)
        eid_v[pl.ds(pl.multiple_of(k * PEER_TOPK, PEER_TOPK), PEER_TOPK), :] = eid
        gate_s[slot, pl.ds(pl.multiple_of(k * PEER_TOPK, PEER_TOPK), PEER_TOPK), :] = gate

    def ids_to_smem(slot):
        eid_t[...] = eid_v[...].T
        return pltpu.make_async_copy(eid_t, idx_smem.at[slot], isem.at[slot])

    def issue(islot, t, n, lo, hi):
        for j in range(lo, hi):
            e = idx_smem[islot, t, j]
            pltpu.make_async_copy(uv_hbm.at[layer, e], bufs[n].at[:, pl.ds(j * ROW_TILES, ROW_TILES), :],
                                  gsem.at[n]).start(priority=j % 2)

    hbuf[...] = _norm_mod(x_ref[...], nw_ref[...], mod_ref, 3)

    @pl.when(i == 0)
    def _():
        h0 = hbuf[...].astype(BF16)

        def piece0(k, carry):
            select_piece(h0, k, 0)
            return carry

        lax.fori_loop(0, PEER_HEADS, piece0, 0)
        cp = ids_to_smem(0)
        cp.start()
        cp.wait()
        for t0 in range(ahead):
            issue(0, t0, t0, 0, PEER_SLOTS)

    hnext[...] = _norm_mod(xn_ref[...], nw_ref[...], modn_ref, 3).astype(BF16)
    g2 = mod_ref[0, :, 5 * D_MODEL:6 * D_MODEL]
    lane = lax.broadcasted_iota(jnp.int32, (PEER_SLOTS, tb), 1)

    def consume(t, n, issue_part):
        pltpu.make_async_copy(uvw_hbm.at[layer, pl.ds(0, 2)], bufs[n], gsem.at[n]).wait()
        xrow = hbuf[pl.ds(t, 1), :]
        acc = jnp.zeros((PEER_SLOTS, LANES), F32)
        for s in range(ROW_TILES):
            issue_part(s)
            us = bufs[n][0, pl.ds(s, PEER_SLOTS, stride=ROW_TILES), :]
            acc = acc + us * xrow[:, s * LANES:(s + 1) * LANES]
        act = jnp.sum(acc, axis=1, keepdims=True)
        gcol = jnp.sum(jnp.where(lane == t, gate_s[cur], 0.0), axis=1, keepdims=True)
        coef = jnp.broadcast_to(gcol * _gelu(act), (PEER_SLOTS, LANES))
        outs = []
        for s in range(ROW_TILES):
            issue_part(ROW_TILES + s)
            vs = bufs[n][1, pl.ds(s, PEER_SLOTS, stride=ROW_TILES), :]
            outs.append(jnp.sum(vs * coef, axis=0, keepdims=True))
        orow = x_ref[pl.ds(t, 1), :] + g2 * jnp.concatenate(outs, axis=1)
        if final:
            orow = orow * lax.rsqrt(jnp.mean(orow * orow, axis=-1, keepdims=True) + EPS) * fw_ref[...]
        o_ref[pl.ds(t, 1), :] = orow

    def group(g):
        for n in range(PEER_NBUF):
            t = g * PEER_NBUF + n
            consume(t, n, lambda k, t=t, n=n: issue(cur, t + ahead, (n + ahead) % PEER_NBUF,
                                                    k * part, (k + 1) * part))

    more = i + 1 < nblk

    def piece_and_groups(k, carry):
        @pl.when(more)
        def _():
            select_piece(hnext[...], k, nxt)

        for gg in range(gpp):
            group(k * gpp + gg)
        return carry

    lax.fori_loop(0, PEER_HEADS - 1, piece_and_groups, 0)

    @pl.when(more)
    def _():
        select_piece(hnext[...], PEER_HEADS - 1, nxt)
        ids_to_smem(nxt).start()

    for gg in range(gpp - 1):
        group((PEER_HEADS - 1) * gpp + gg)

    @pl.when(more)
    def _():
        pltpu.make_async_copy(eid_t, idx_smem.at[nxt], isem.at[nxt]).wait()

    for n in range(PEER_NBUF):
        t = (ngroups - 1) * PEER_NBUF + n
        if n == 0:
            consume(t, n, lambda k, t=t: issue(cur, t + ahead, ahead, k * part, (k + 1) * part))
        else:
            @pl.when(more)
            def _():
                issue(nxt, n - 1, n - 1, 0, PEER_SLOTS)
            consume(t, n, lambda k: None)


def _peer(xs, nw, modsel, wq, keys, uv, layer, ctx_tiles, final_w=None):
    bsz, n, d = xs.shape
    ntok = bsz * n
    nblk = ntok // PEER_TB
    tiles_per_batch = n // PEER_TB
    ctx_blocks = ctx_tiles * ROWS // PEER_TB
    slab = PEER_SLOTS * ROW_TILES
    assert PEER_TB % (PEER_NBUF * PEER_HEADS) == 0
    uvw = uv.reshape(uv.shape[0], -1, slab, LANES)
    final = final_w is not None
    lat_blocks = tiles_per_batch - ctx_blocks
    if final:
        out_map = lambda i: ((i // tiles_per_batch) * lat_blocks
                             + jnp.maximum(i % tiles_per_batch - ctx_blocks, 0), 0)
        out_rows = bsz * lat_blocks * PEER_TB
    else:
        out_map = lambda i: (i, 0)
        out_rows = ntok
        final_w = jnp.ones((1, d), F32)
    wq3 = wq.reshape(d, PEER_HEADS, PEER_DK).transpose(1, 0, 2)
    nxt_blk = lambda i: jnp.minimum(i + 1, nblk - 1)
    mod_of = lambda i: (2 * (i // tiles_per_batch) + jnp.where(i % tiles_per_batch >= ctx_blocks, 1, 0), 0, 0)
    out = pl.pallas_call(
        functools.partial(_peer_kernel, nblk=nblk, layer=layer, final=final),
        grid=(nblk,),
        in_specs=[pl.BlockSpec((PEER_TB, d), lambda i: (i, 0)),
                  pl.BlockSpec((PEER_TB, d), lambda i: (nxt_blk(i), 0)),
                  pl.BlockSpec((1, d), lambda i: (0, 0)),
                  pl.BlockSpec((1, 1, 6 * D_MODEL), mod_of),
                  pl.BlockSpec((1, 1, 6 * D_MODEL), lambda i: mod_of(nxt_blk(i))),
                  pl.BlockSpec(wq3.shape, lambda i: (0, 0, 0)),
                  pl.BlockSpec(keys.shape, lambda i: (0, 0, 0)),
                  pl.BlockSpec((1, d), lambda i: (0, 0)),
                  pl.BlockSpec(memory_space=pl.ANY),
                  pl.BlockSpec(memory_space=pl.ANY)],
        out_specs=pl.BlockSpec((PEER_TB, d), out_map),
        out_shape=jax.ShapeDtypeStruct((out_rows, d), F32),
        scratch_shapes=[pltpu.VMEM((PEER_TB, d), F32),
                        pltpu.VMEM((PEER_TB, d), BF16),
                        pltpu.VMEM((PEER_SLOTS, PEER_TB), jnp.int32),
                        pltpu.VMEM((PEER_TB, PEER_SLOTS), jnp.int32),
                        pltpu.VMEM((2, PEER_SLOTS, PEER_TB), F32),
                        *[pltpu.VMEM((2, slab, LANES), F32) for _ in range(PEER_NBUF)],
                        pltpu.SMEM((2, PEER_TB, PEER_SLOTS), jnp.int32),
                        pltpu.SemaphoreType.DMA((PEER_NBUF,)),
                        pltpu.SemaphoreType.DMA((2,))],
        compiler_params=_cparams(("arbitrary",)),
        name="peer",
    )(xs.reshape(ntok, d), xs.reshape(ntok, d), nw, modsel, modsel, wq3, keys, final_w, uv, uvw)
    return out.reshape(bsz, -1, d)


def _block_diag(w):
    nb, d, e = w.shape
    eye = jnp.eye(nb, dtype=w.dtype)
    return (eye[:, None, :, None] * w[:, :, None, :]).reshape(nb * d, nb * e)


def _mixer0(xs, modsel, norm1, w_in, w_out, i_bias, f_bias, mlstm_norm, conv_w, conv_b, dt_bias,
            a_log, d_skip, ssd_norm, ctx_tiles, nc_ctx):
    q0, k0, v0, o0, ig0, fg0, z0, xbc0, dt0, end = 0, 256, 512, 1024, 1536, 1544, 1552, 2064, 3088, 3104
    pad = jnp.zeros((D_MODEL, LANES - 32), w_in.dtype)
    w_cat = jnp.concatenate([w_in[:, q0:ig0], w_in[:, xbc0:dt0], w_in[:, z0:xbc0],
                             w_in[:, ig0:z0], w_in[:, dt0:end], pad], axis=1).astype(BF16)
    qkvo, zx, gates = _project(xs, norm1, modsel, w_cat, (1536, 1536, LANES), ctx_tiles)
    gates_row = jnp.swapaxes(gates[:, :, :32], 1, 2)
    bias = jnp.concatenate([i_bias.reshape(-1), f_bias.reshape(-1), dt_bias.reshape(-1)])
    bias_col = jnp.pad(bias, (0, LANES - 32)).reshape(1, LANES)
    bias_row = bias.reshape(32, 1)
    alog = jnp.concatenate([jnp.zeros((16,), F32), a_log.reshape(-1)])
    alog_col = jnp.pad(alog, (0, LANES - 32)).reshape(1, LANES)
    alog_row = alog.reshape(32, 1)
    xbc_act = _conv4(zx, 1024, conv_w, conv_b.reshape(1, -1), ctx_tiles, True)
    hf = _mlstm_scan(qkvo, gates, gates_row, bias_col, bias_row, nc_ctx, 0)
    hb = _mlstm_scan(qkvo, gates, gates_row, bias_col, bias_row, nc_ctx, 1)
    yf = _ssd_scan(xbc_act, gates, gates_row, bias_col, bias_row, alog_col, alog_row, nc_ctx, 0)
    yb = _ssd_scan(xbc_act, gates, gates_row, bias_col, bias_row, alog_col, alog_row, nc_ctx, 1)
    dskip = jnp.repeat(d_skip, B_HEADDIM).reshape(1, -1)
    return _finish0(hf, hb, yf, yb, qkvo, zx, xbc_act, xs, modsel, mlstm_norm.reshape(1, -1),
                    ssd_norm.reshape(1, -1), dskip, w_out.astype(BF16), ctx_tiles)


def _mixer1(xs, modsel, norm1, w_in, w_out, mu, w0, w_up, a0, a_up, g_up, k_k, k_a, r_k, ln_w, ln_b,
            conv_w, conv_b, lam, wa, ba, wi, bi, ctx_tiles, nc_ctx):
    pr, gate_br, x_br = _project(xs, norm1, modsel, w_in.astype(BF16), (RWKV_COLS, D_WIDTH, D_WIDTH),
                                 ctx_tiles)
    r, v, kk, g, w_f, w_b, kt_f, kt_b, al_f, al_b = _rwkv_prep(
        pr, mu.reshape(1, -1), w0, w_up, a0, a_up, g_up, k_k.reshape(1, -1), k_a.reshape(1, -1), ctx_tiles)
    yf, yb = _rwkv_scan(r, v, kk, w_f, kt_f, al_f, w_b, kt_b, al_b, nc_ctx)
    xc = _conv4(x_br, D_WIDTH, conv_w, conv_b.reshape(1, -1), ctx_tiles, False)
    w_gate = jnp.concatenate([_block_diag(wa[0]), _block_diag(wi[0]),
                              _block_diag(wa[1]), _block_diag(wi[1])], axis=1)
    b_gate = jnp.concatenate([ba[0], bi[0], ba[1], bi[1]]).reshape(1, -1)
    a_f, b_f, a_b, b_b = _lru_gates(xc, w_gate, b_gate, lam)
    uf = _lru_scan(a_f, b_f, nc_ctx, False)
    ub = _lru_scan(a_b, b_b, nc_ctx, True)
    return _finish1(yf, yb, uf, ub, r, kt_f, kt_b, v, g, gate_br, xs, modsel, r_k.reshape(1, -1),
                    ln_w.reshape(1, -1), ln_b.reshape(1, -1), w_out.astype(BF16), ctx_tiles)


def kernel(x, c, ctx, c_ctx, mod_w, mod_b, norm1, norm2, peer_wq, peer_keys, peer_u, peer_v, ev_w_in, ev_w_out, ev_mlstm_i_bias, ev_mlstm_f_bias, ev_mlstm_norm, ev_ssd_conv_w, ev_ssd_conv_b, ev_ssd_dt_bias, ev_ssd_a_log, ev_ssd_d, ev_ssd_norm, od_w_in, od_w_out, od_rwkv_mu, od_rwkv_w0, od_rwkv_w_up, od_rwkv_a0, od_rwkv_a_up, od_rwkv_g_up, od_rwkv_k_k, od_rwkv_k_a, od_rwkv_r_k, od_rwkv_ln_w, od_rwkv_ln_b, od_lru_conv_w, od_lru_conv_b, od_lru_lambda, od_lru_wa, od_lru_ba, od_lru_wi, od_lru_bi, final_norm):
    bsz, seq, d = x.shape
    ctx_len = ctx.shape[1]
    depth = mod_w.shape[0]
    assert d == D_MODEL and ctx_len == ROWS and seq % ROWS == 0 and bsz < SUBLANES
    ctx_tiles = ctx_len // ROWS
    nc_ctx = ctx_len // CHUNK
    xs = jnp.concatenate([ctx, x], axis=1)
    srows = jnp.concatenate([c, c_ctx[None, :], jnp.zeros((SUBLANES - bsz - 1, d), F32)], axis=0)
    ne = peer_u.shape[1]
    uv = jnp.stack([peer_u.reshape(depth, ne, ROW_TILES, LANES), peer_v.reshape(depth, ne, ROW_TILES, LANES)],
                   axis=2)
    for i in range(depth):
        mod = _modulation(srows, mod_w[i], mod_b[i].reshape(1, -1))
        modsel = jnp.stack([jnp.broadcast_to(mod[bsz], (bsz, 6 * d)), mod[:bsz]], axis=1)
        modsel = modsel.reshape(2 * bsz, 1, 6 * d)
        j = i // 2
        n1 = norm1[i].reshape(1, -1)
        if i % 2 == 0:
            xs = _mixer0(xs, modsel, n1, ev_w_in[j], ev_w_out[j], ev_mlstm_i_bias[j], ev_mlstm_f_bias[j],
                         ev_mlstm_norm[j], ev_ssd_conv_w[j], ev_ssd_conv_b[j], ev_ssd_dt_bias[j],
                         ev_ssd_a_log[j], ev_ssd_d[j], ev_ssd_norm[j], ctx_tiles, nc_ctx)
        else:
            xs = _mixer1(xs, modsel, n1, od_w_in[j], od_w_out[j], od_rwkv_mu[j], od_rwkv_w0[j],
                         od_rwkv_w_up[j], od_rwkv_a0[j], od_rwkv_a_up[j], od_rwkv_g_up[j], od_rwkv_k_k[j],
                         od_rwkv_k_a[j], od_rwkv_r_k[j].reshape(-1), od_rwkv_ln_w[j], od_rwkv_ln_b[j],
                         od_lru_conv_w[j], od_lru_conv_b[j], od_lru_lambda[j], od_lru_wa[j], od_lru_ba[j],
                         od_lru_wi[j], od_lru_bi[j], ctx_tiles, nc_ctx)
        keys = peer_keys[i].reshape(2 * PEER_HEADS, N_KEYS, PEER_DK // 2)
        xs = _peer(xs, norm2[i].reshape(1, -1), modsel, peer_wq[i].astype(BF16), keys, uv, i, ctx_tiles,
                   final_norm.reshape(1, -1) if i == depth - 1 else None)
    return xs
```

```python
import functools
import math

import jax
import jax.numpy as jnp
from jax import lax
from jax.experimental import pallas as pl
from jax.experimental.pallas import tpu as pltpu

F32 = jnp.float32
BF16 = jnp.bfloat16
HIGHEST = lax.Precision.HIGHEST

D_MODEL = 1024
EPS = 1e-6
CHUNK = 128
ROWS = 256
GRID_W = 64
LANES = 128
SUBLANES = 8
MIB = 1024 * 1024

A_HEADS, A_DQK, A_DV = 4, 64, 128
A_WIDTH = A_HEADS * A_DV
B_HEADS, B_HEADDIM, B_GROUPS, B_STATE = 8, 64, 2, 128
B_WIDTH = B_HEADS * B_HEADDIM
C_HEADS, C_HEADDIM = 8, 64
C_WIDTH = C_HEADS * C_HEADDIM
C_LORA_W, C_LORA_A, C_LORA_G = 64, 64, 128
RWKV_COLS = 3 * C_WIDTH + C_LORA_W + C_LORA_A + C_LORA_G
RWKV_W_SCALE = math.exp(-0.5)
RWKV_GN_EPS = 64e-5
D_WIDTH = 512
LRU_C = 8.0
PEER_HEADS, PEER_DK, N_KEYS, PEER_TOPK = 8, 256, 128, 16
PEER_SLOTS = PEER_HEADS * PEER_TOPK
PEER_TB = 128
PEER_NBUF = 8
ROW_TILES = D_MODEL // LANES


def _cparams(sem, vmem_mib=48):
    return pltpu.CompilerParams(dimension_semantics=sem, vmem_limit_bytes=vmem_mib * MIB)


def _softplus(x):
    return jnp.maximum(x, 0.0) + jnp.log1p(jnp.exp(-jnp.abs(x)))


def _sigmoid(x):
    return 1.0 / (1.0 + jnp.exp(-x))


def _silu(x):
    return x * _sigmoid(x)


def _gelu(x):
    return 0.5 * x * (1.0 + lax.erf(x * (1.0 / math.sqrt(2.0))))


def _norm_mod(x, nw, mod_ref, slot):
    ms = jnp.mean(x * x, axis=-1, keepdims=True)
    y = x * lax.rsqrt(ms + EPS) * nw
    sh = mod_ref[0, :, slot * D_MODEL:(slot + 1) * D_MODEL]
    sc = mod_ref[0, :, (slot + 1) * D_MODEL:(slot + 2) * D_MODEL]
    return y * (1.0 + sc) + sh


def _chunk_order(c, nc_ctx, nc, reverse):
    if not reverse:
        return c
    return jnp.where(c < nc_ctx, nc_ctx - 1 - c, nc + nc_ctx - 1 - c)


def _mod_kernel(s_ref, w_ref, b_ref, o_ref):
    s = _silu(s_ref[...])
    o_ref[...] = jnp.dot(s, w_ref[...], precision=HIGHEST, preferred_element_type=F32) + b_ref[...]


def _modulation(srows, w, b):
    d = srows.shape[1]
    nt = w.shape[1] // d
    return pl.pallas_call(
        _mod_kernel,
        grid=(nt,),
        in_specs=[pl.BlockSpec((SUBLANES, d), lambda j: (0, 0)),
                  pl.BlockSpec((d, d), lambda j: (0, j)),
                  pl.BlockSpec((1, d), lambda j: (0, j))],
        out_specs=pl.BlockSpec((SUBLANES, d), lambda j: (0, j)),
        out_shape=jax.ShapeDtypeStruct((SUBLANES, w.shape[1]), F32),
        compiler_params=_cparams(("arbitrary",)),
        name="modulation",
    )(srows, w, b)


def _proj_kernel(x_ref, nw_ref, mod_ref, w_ref, *o_refs, widths):
    h = _norm_mod(x_ref[0], nw_ref[...], mod_ref, 0)
    out = jnp.dot(h.astype(BF16), w_ref[...], preferred_element_type=F32)
    off = 0
    for o_ref, wd in zip(o_refs, widths):
        o_ref[0] = out[:, off:off + wd]
        off += wd


def _mod_spec(ctx_tiles):
    return pl.BlockSpec((1, 1, 6 * D_MODEL),
                        lambda b, i: (2 * b + jnp.where(i >= ctx_tiles, 1, 0), 0, 0))


def _project(xs, nw, modsel, w, widths, ctx_tiles):
    bsz, n, d = xs.shape
    nt = n // ROWS
    return pl.pallas_call(
        functools.partial(_proj_kernel, widths=widths),
        grid=(bsz, nt),
        in_specs=[pl.BlockSpec((1, ROWS, d), lambda b, i: (b, i, 0)),
                  pl.BlockSpec((1, d), lambda b, i: (0, 0)),
                  _mod_spec(ctx_tiles),
                  pl.BlockSpec(w.shape, lambda b, i: (0, 0))],
        out_specs=[pl.BlockSpec((1, ROWS, wd), lambda b, i: (b, i, 0)) for wd in widths],
        out_shape=[jax.ShapeDtypeStruct((bsz, n, wd), F32) for wd in widths],
        compiler_params=_cparams(("parallel", "arbitrary")),
        name="norm_mod_project",
    )(xs, nw, modsel, w)


def _conv_kernel(x_ref, p_ref, n_ref, w_ref, b_ref, o_ref, *, ctx_tiles, ntiles, act):
    i = pl.program_id(1)
    x = x_ref[0]
    rows = x.shape[0]
    prev_ok = jnp.logical_and(i != 0, i != ctx_tiles)
    next_ok = jnp.logical_and(i != ctx_tiles - 1, i != ntiles - 1)
    p = jnp.where(prev_ok, p_ref[0], 0.0)
    nx = jnp.where(next_ok, n_ref[0], 0.0)
    row = lax.broadcasted_iota(jnp.int32, (rows, 1), 0)
    xm1 = jnp.where(row == 0, p[7:8], pltpu.roll(x, 1, 0))
    xm2 = jnp.where(row == 0, p[6:7], jnp.where(row == 1, p[7:8], pltpu.roll(x, 2, 0)))
    xp1 = jnp.where(row == rows - 1, nx[0:1], pltpu.roll(x, rows - 1, 0))
    w = w_ref[...]
    y = b_ref[...] + xm2 * w[0:1] + xm1 * w[1:2] + x * w[2:3] + xp1 * w[3:4]
    o_ref[0] = _silu(y) if act else y


def _conv4(x, width, w, b, ctx_tiles, act):
    bsz, n, _ = x.shape
    nt = n // ROWS
    hb = ROWS // SUBLANES
    nh = n // SUBLANES
    return pl.pallas_call(
        functools.partial(_conv_kernel, ctx_tiles=ctx_tiles, ntiles=nt, act=act),
        grid=(bsz, nt),
        in_specs=[pl.BlockSpec((1, ROWS, width), lambda b_, i: (b_, i, 0)),
                  pl.BlockSpec((1, SUBLANES, width), lambda b_, i: (b_, jnp.maximum(i * hb - 1, 0), 0)),
                  pl.BlockSpec((1, SUBLANES, width),
                               lambda b_, i: (b_, jnp.minimum((i + 1) * hb, nh - 1), 0)),
                  pl.BlockSpec((4, width), lambda b_, i: (0, 0)),
                  pl.BlockSpec((1, width), lambda b_, i: (0, 0))],
        out_specs=pl.BlockSpec((1, ROWS, width), lambda b_, i: (b_, i, 0)),
        out_shape=jax.ShapeDtypeStruct((bsz, n, width), F32),
        compiler_params=_cparams(("parallel", "arbitrary")),
        name="conv4",
    )(x, x, x, w, b)


def _tri(reverse):
    t = lax.broadcasted_iota(jnp.int32, (CHUNK, CHUNK), 0)
    s = lax.broadcasted_iota(jnp.int32, (CHUNK, CHUNK), 1)
    return (s >= t) if reverse else (s <= t)


def _cumsums(col, row, mask):
    mf = mask.astype(F32)
    b_col = jnp.dot(mf, col, precision=HIGHEST, preferred_element_type=F32)
    b_row = lax.dot_general(row, mf, (((1,), (1,)), ((), ())), precision=HIGHEST,
                            preferred_element_type=F32)
    return b_col, b_row


def _mlstm_kernel(qkv_ref, gc_ref, gr_ref, bc_ref, br_ref, o_ref, ct_ref, n_ref, m_ref, *,
                  direction):
    c = pl.program_id(1)

    @pl.when(c == 0)
    def _():
        ct_ref[...] = jnp.zeros_like(ct_ref)
        n_ref[...] = jnp.zeros_like(n_ref)
        m_ref[...] = jnp.zeros_like(m_ref)

    reverse = direction == 1
    mask = _tri(reverse)
    gc = gc_ref[0] + bc_ref[...]
    gr = gr_ref[0] + br_ref[...]
    d4 = direction * A_HEADS
    li_col = gc[:, d4:d4 + A_HEADS]
    lf_col = -_softplus(-gc[:, 8 + d4:8 + d4 + A_HEADS])
    li_row = gr[d4:d4 + A_HEADS, :]
    lf_row = -_softplus(-gr[8 + d4:8 + d4 + A_HEADS, :])
    b_col, b_row = _cumsums(lf_col, lf_row, mask)
    tot = jnp.sum(lf_col, axis=0, keepdims=True)
    hs = range(A_HEADS)
    nt_dims, tn_dims = (((1,), (1,)), ((), ())), (((0,), (0,)), ((), ()))
    q = [qkv_ref[0, :, h * A_DQK:(h + 1) * A_DQK] * (A_DQK ** -0.5) for h in hs]
    k = [qkv_ref[0, :, A_HEADS * A_DQK + h * A_DQK:A_HEADS * A_DQK + (h + 1) * A_DQK] for h in hs]
    v = [qkv_ref[0, :, 2 * A_HEADS * A_DQK + h * A_DV:2 * A_HEADS * A_DQK + (h + 1) * A_DV] for h in hs]
    ct_in = [ct_ref[h] for h in hs]
    n_in = [n_ref[h] for h in hs]
    m_prev = [m_ref[h] for h in hs]
    bc = [b_col[:, h:h + 1] for h in hs]
    th = [tot[:, h:h + 1] for h in hs]
    dmat = [jnp.where(mask, bc[h] - b_row[h:h + 1, :] + li_row[h:h + 1, :], -jnp.inf) for h in hs]
    inter = [bc[h] + m_prev[h] for h in hs]
    m_t = [jnp.maximum(inter[h], jnp.max(dmat[h], axis=1, keepdims=True)) for h in hs]
    qk = [lax.dot_general(q[h], k[h], nt_dims, preferred_element_type=F32) for h in hs]
    s = [qk[h] * jnp.exp(dmat[h] - m_t[h]) for h in hs]
    w_inter = [jnp.exp(inter[h] - m_t[h]) for h in hs]
    num = [jnp.dot(s[h], v[h], preferred_element_type=F32)
           + w_inter[h] * jnp.dot(q[h], ct_in[h], preferred_element_type=F32) for h in hs]
    den = [jnp.sum(s[h], axis=1, keepdims=True)
           + w_inter[h] * jnp.sum(q[h] * n_in[h], axis=1, keepdims=True) for h in hs]
    o_ref[0] = jnp.concatenate([num[h] / jnp.maximum(jnp.abs(den[h]), jnp.exp(-m_t[h])) for h in hs], axis=1)
    g = [th[h] - bc[h] + li_col[:, h:h + 1] for h in hs]
    m_new = [jnp.maximum(th[h] + m_prev[h], jnp.max(g[h], axis=0, keepdims=True)) for h in hs]
    wg = [jnp.exp(g[h] - m_new[h]) for h in hs]
    wc = [jnp.exp(th[h] + m_prev[h] - m_new[h]) for h in hs]
    ct_out = [wc[h] * ct_in[h] + lax.dot_general(k[h] * wg[h], v[h], tn_dims, preferred_element_type=F32)
              for h in hs]
    n_out = [wc[h] * n_in[h] + jnp.sum(wg[h] * k[h], axis=0, keepdims=True) for h in hs]
    for h in hs:
        ct_ref[h] = ct_out[h]
        n_ref[h] = n_out[h]
        m_ref[h] = m_new[h]


def _mlstm_scan(qkvo, gates, gates_row, bias_col, bias_row, nc_ctx, direction):
    bsz, n, _ = qkvo.shape
    nc = n // CHUNK
    reverse = direction == 1
    order = lambda c: _chunk_order(c, nc_ctx, nc, reverse)
    return pl.pallas_call(
        functools.partial(_mlstm_kernel, direction=direction),
        grid=(bsz, nc),
        in_specs=[pl.BlockSpec((1, CHUNK, 2 * A_HEADS * A_DQK + A_WIDTH), lambda b, c: (b, order(c), 0)),
                  pl.BlockSpec((1, CHUNK, LANES), lambda b, c: (b, order(c), 0)),
                  pl.BlockSpec((1, 32, CHUNK), lambda b, c: (b, 0, order(c))),
                  pl.BlockSpec((1, LANES), lambda b, c: (0, 0)),
                  pl.BlockSpec((32, 1), lambda b, c: (0, 0))],
        out_specs=pl.BlockSpec((1, CHUNK, A_WIDTH), lambda b, c: (b, order(c), 0)),
        out_shape=jax.ShapeDtypeStruct((bsz, n, A_WIDTH), F32),
        scratch_shapes=[pltpu.VMEM((A_HEADS, A_DQK, A_DV), F32),
                        pltpu.VMEM((A_HEADS, 1, A_DQK), F32),
                        pltpu.VMEM((A_HEADS, 1, 1), F32)],
        compiler_params=_cparams(("parallel", "arbitrary")),
        name="mlstm_scan",
    )(qkvo, gates, gates_row, bias_col, bias_row)


def _ssd_kernel(xbc_ref, gc_ref, gr_ref, bc_ref, br_ref, ac_ref, ar_ref, o_ref, st_ref, *,
                direction):
    c = pl.program_id(1)

    @pl.when(c == 0)
    def _():
        st_ref[...] = jnp.zeros_like(st_ref)

    reverse = direction == 1
    mask = _tri(reverse)
    gc = gc_ref[0] + bc_ref[...]
    gr = gr_ref[0] + br_ref[...]
    d8 = 16 + direction * B_HEADS
    dt_col = _softplus(gc[:, d8:d8 + B_HEADS])
    dt_row = _softplus(gr[d8:d8 + B_HEADS, :])
    la_col = -dt_col * jnp.exp(ac_ref[:, d8:d8 + B_HEADS])
    la_row = -dt_row * jnp.exp(ar_ref[d8:d8 + B_HEADS, :])
    b_col, b_row = _cumsums(la_col, la_row, mask)
    tot = jnp.sum(la_col, axis=0, keepdims=True)
    hs = range(B_HEADS)
    hpg = B_HEADS // B_GROUPS
    nt_dims, tn_dims = (((1,), (1,)), ((), ())), (((0,), (0,)), ((), ()))
    bm = [xbc_ref[0, :, B_WIDTH + g * B_STATE:B_WIDTH + (g + 1) * B_STATE] for g in range(B_GROUPS)]
    cm = [xbc_ref[0, :, B_WIDTH + (B_GROUPS + g) * B_STATE:B_WIDTH + (B_GROUPS + g + 1) * B_STATE]
          for g in range(B_GROUPS)]
    cb = [lax.dot_general(cm[g], bm[g], nt_dims, preferred_element_type=F32) for g in range(B_GROUPS)]
    xh = [xbc_ref[0, :, h * B_HEADDIM:(h + 1) * B_HEADDIM] for h in hs]
    st_in = [st_ref[h] for h in hs]
    bc = [b_col[:, h:h + 1] for h in hs]
    th = [tot[:, h:h + 1] for h in hs]
    decay = [jnp.exp(jnp.where(mask, bc[h] - b_row[h:h + 1, :], -jnp.inf)) for h in hs]
    s = [cb[h // hpg] * decay[h] * dt_row[h:h + 1, :] for h in hs]
    y = [jnp.dot(s[h], xh[h], preferred_element_type=F32)
         + jnp.exp(bc[h]) * jnp.dot(cm[h // hpg], st_in[h], preferred_element_type=F32) for h in hs]
    o_ref[0] = jnp.concatenate(y, axis=1)
    w_end = [jnp.exp(th[h] - bc[h]) * dt_col[:, h:h + 1] for h in hs]
    st_out = [jnp.exp(th[h]) * st_in[h]
              + lax.dot_general(bm[h // hpg] * w_end[h], xh[h], tn_dims, preferred_element_type=F32) for h in hs]
    for h in hs:
        st_ref[h] = st_out[h]


def _ssd_scan(xbc, gates, gates_row, bias_col, bias_row, alog_col, alog_row, nc_ctx, direction):
    bsz, n, width = xbc.shape
    nc = n // CHUNK
    reverse = direction == 1
    order = lambda c: _chunk_order(c, nc_ctx, nc, reverse)
    return pl.pallas_call(
        functools.partial(_ssd_kernel, direction=direction),
        grid=(bsz, nc),
        in_specs=[pl.BlockSpec((1, CHUNK, width), lambda b, c: (b, order(c), 0)),
                  pl.BlockSpec((1, CHUNK, LANES), lambda b, c: (b, order(c), 0)),
                  pl.BlockSpec((1, 32, CHUNK), lambda b, c: (b, 0, order(c))),
                  pl.BlockSpec((1, LANES), lambda b, c: (0, 0)),
                  pl.BlockSpec((32, 1), lambda b, c: (0, 0)),
                  pl.BlockSpec((1, LANES), lambda b, c: (0, 0)),
                  pl.BlockSpec((32, 1), lambda b, c: (0, 0))],
        out_specs=pl.BlockSpec((1, CHUNK, B_WIDTH), lambda b, c: (b, order(c), 0)),
        out_shape=jax.ShapeDtypeStruct((bsz, n, B_WIDTH), F32),
        scratch_shapes=[pltpu.VMEM((B_HEADS, B_STATE, B_HEADDIM), F32)],
        compiler_params=_cparams(("parallel", "arbitrary")),
        name="ssd_scan",
    )(xbc, gates, gates_row, bias_col, bias_row, alog_col, alog_row)


def _finish0_kernel(hf_ref, hb_ref, yf_ref, yb_ref, o_ref_in, z_ref, xs_ref, x_ref, mod_ref,
                    mn_ref, sn_ref, dsk_ref, w_ref, out_ref):
    h = hf_ref[0] + hb_ref[0]
    parts = []
    for hd in range(A_HEADS):
        hh = h[:, hd * A_DV:(hd + 1) * A_DV]
        ms = jnp.mean(hh * hh, axis=-1, keepdims=True)
        parts.append(hh * lax.rsqrt(ms + EPS))
    hn = jnp.concatenate(parts, axis=1) * mn_ref[...]
    ya = _sigmoid(o_ref_in[0]) * hn
    y = yf_ref[0] + yb_ref[0] + dsk_ref[...] * xs_ref[0]
    t = y * _silu(z_ref[0])
    ms = jnp.mean(t * t, axis=-1, keepdims=True)
    yb = t * lax.rsqrt(ms + EPS) * sn_ref[...]
    f = jnp.concatenate([ya, yb], axis=1)
    g1 = mod_ref[0, :, 2 * D_MODEL:3 * D_MODEL]
    out_ref[0] = x_ref[0] + g1 * jnp.dot(f.astype(BF16), w_ref[...], preferred_element_type=F32)


def _finish0(hf, hb, yf, yb, qkvo, zx, xbc_act, xs, modsel, mnorm, snorm, dskip, w_out, ctx_tiles):
    bsz, n, d = xs.shape
    nt = n // ROWS
    half = lambda j: pl.BlockSpec((1, ROWS, 512), lambda b, i, j=j: (b, i, j))
    vec = pl.BlockSpec((1, 512), lambda b, i: (0, 0))
    return pl.pallas_call(
        _finish0_kernel,
        grid=(bsz, nt),
        in_specs=[half(0), half(0), half(0), half(0), half(2), half(2), half(0),
                  pl.BlockSpec((1, ROWS, d), lambda b, i: (b, i, 0)),
                  _mod_spec(ctx_tiles), vec, vec, vec,
                  pl.BlockSpec((d, d), lambda b, i: (0, 0))],
        out_specs=pl.BlockSpec((1, ROWS, d), lambda b, i: (b, i, 0)),
        out_shape=jax.ShapeDtypeStruct((bsz, n, d), F32),
        compiler_params=_cparams(("parallel", "arbitrary")),
        name="finish_mlstm_ssd",
    )(hf, hb, yf, yb, qkvo, zx, xbc_act, xs, modsel, mnorm, snorm, dskip, w_out)


def _head_ones(n, hd):
    r = lax.broadcasted_iota(jnp.int32, (n, n), 0) // hd
    c = lax.broadcasted_iota(jnp.int32, (n, n), 1) // hd
    return (r == c).astype(F32)


def _rwkv_prep_kernel(pr_ref, p_ref, n_ref, mu_ref, w0_ref, wup_ref, a0_ref, aup_ref, gup_ref,
                      kk_ref, ka_ref, r_o, v_o, kk_o, g_o, w0_o, w1_o, kt0_o, kt1_o, al0_o, al1_o,
                      *, ctx_tiles, ntiles):
    i = pl.program_id(1)
    pr = pr_ref[0]
    rows, cols = pr.shape
    q = cols // 4
    row = lax.broadcasted_iota(jnp.int32, (rows, 1), 0)
    col = lax.broadcasted_iota(jnp.int32, (1, cols), 1)
    is_ctx = i < ctx_tiles
    l1 = pltpu.roll(pr, 1, 0)
    r1 = pltpu.roll(pr, rows - 1, 0)
    left = jnp.where(row % GRID_W == 0, 0.0, l1)
    right = jnp.where(row % GRID_W == GRID_W - 1, 0.0, r1)
    up = jnp.concatenate([p_ref[0], pr[:rows - GRID_W]], axis=0)
    up = jnp.where(jnp.logical_and(i == ctx_tiles, row < GRID_W), 0.0, up)
    down = jnp.concatenate([pr[GRID_W:], n_ref[0]], axis=0)
    down = jnp.where(jnp.logical_and(i == ntiles - 1, row >= rows - GRID_W), 0.0, down)
    grid_sh = jnp.where(col < q, left, jnp.where(col < 2 * q, right, jnp.where(col < 3 * q, up, down)))
    prev = jnp.where(row == 0, 0.0, l1)
    nxt = jnp.where(row == rows - 1, 0.0, r1)
    seq_sh = jnp.where(col < 2 * q, prev, nxt)
    shifted = jnp.where(is_ctx, seq_sh, grid_sh)
    pr = pr + mu_ref[...] * (shifted - pr)
    r = pr[:, 0:C_WIDTH]
    k = pr[:, C_WIDTH:2 * C_WIDTH]
    v = pr[:, 2 * C_WIDTH:3 * C_WIDTH]
    o = 3 * C_WIDTH
    wd = pr[:, o:o + C_LORA_W]
    ad = pr[:, o + C_LORA_W:o + C_LORA_W + C_LORA_A]
    gd = pr[:, o + C_LORA_W + C_LORA_A:]
    kk = k * kk_ref[...]
    ss = jnp.dot(kk * kk, _head_ones(C_WIDTH, C_HEADDIM), precision=HIGHEST, preferred_element_type=F32)
    kk = kk * lax.rsqrt(ss + 1e-12)
    tw = jnp.tanh(wd)
    for d, (w_o, kt_o, al_o) in enumerate(((w0_o, kt0_o, al0_o), (w1_o, kt1_o, al1_o))):
        logw = -RWKV_W_SCALE * _sigmoid(w0_ref[d:d + 1, :] + jnp.dot(tw, wup_ref[d], preferred_element_type=F32))
        a = _sigmoid(a0_ref[d:d + 1, :] + jnp.dot(ad, aup_ref[d], preferred_element_type=F32))
        w_o[0] = jnp.exp(logw)
        kt_o[0] = k * (1.0 + (a - 1.0) * ka_ref[...])
        al_o[0] = a
    r_o[0] = r
    v_o[0] = v
    kk_o[0] = kk
    g_o[0] = jnp.dot(_sigmoid(gd), gup_ref[...], preferred_element_type=F32)


def _rwkv_prep(pr, mu, w0, w_up, a0, a_up, g_up, k_k, k_a, ctx_tiles):
    bsz, n, cols = pr.shape
    nt = n // ROWS
    hb = ROWS // GRID_W
    nh = n // GRID_W
    full = lambda a: pl.BlockSpec(a.shape, lambda b, i, nd=a.ndim: (0,) * nd)
    out = pl.BlockSpec((1, ROWS, C_WIDTH), lambda b, i: (b, i, 0))
    return pl.pallas_call(
        functools.partial(_rwkv_prep_kernel, ctx_tiles=ctx_tiles, ntiles=nt),
        grid=(bsz, nt),
        in_specs=[pl.BlockSpec((1, ROWS, cols), lambda b, i: (b, i, 0)),
                  pl.BlockSpec((1, GRID_W, cols), lambda b, i: (b, jnp.maximum(i * hb - 1, 0), 0)),
                  pl.BlockSpec((1, GRID_W, cols), lambda b, i: (b, jnp.minimum((i + 1) * hb, nh - 1), 0)),
                  full(mu), full(w0), full(w_up), full(a0), full(a_up), full(g_up), full(k_k), full(k_a)],
        out_specs=[out] * 10,
        out_shape=[jax.ShapeDtypeStruct((bsz, n, C_WIDTH), F32)] * 10,
        compiler_params=_cparams(("parallel", "arbitrary")),
        name="rwkv_prep",
    )(pr, pr, pr, mu, w0, w_up, a0, a_up, g_up, k_k, k_a)


RWKV_UNROLL = 8


def _rwkv_kernel(rf, wf, kf, vf, kkf, alf, rb, wb, kb, vb, kkb, alb, of_ref, ob_ref, s_ref, y_ref):
    c = pl.program_id(0)

    @pl.when(c == 0)
    def _():
        s_ref[...] = jnp.zeros_like(s_ref)

    nb = rf.shape[0]
    t_len = rf.shape[1]
    nt = C_WIDTH // LANES
    sub = lax.broadcasted_iota(jnp.int32, (C_HEADDIM, C_WIDTH), 0)
    lane = lax.broadcasted_iota(jnp.int32, (C_HEADDIM, C_WIDTH), 1)
    diag = (lane % C_HEADDIM == sub).astype(F32)
    ones = _head_ones(LANES, C_HEADDIM).astype(BF16)
    ones2 = jnp.concatenate([ones, ones], axis=0)
    chains = [(refs, b) for refs in ((rf, wf, kf, vf, kkf, alf, of_ref, False),
                                     (rb, wb, kb, vb, kkb, alb, ob_ref, True)) for b in range(nb)]

    def head_sum(xs, split):
        x = jnp.concatenate([a[:, j * LANES:(j + 1) * LANES] for a in xs for j in range(nt)], axis=0)
        if split:
            hi = x.astype(BF16)
            lo = (x - hi.astype(F32)).astype(BF16)
            out = jnp.dot(jnp.concatenate([hi, lo], axis=1), ones2, preferred_element_type=F32)
        else:
            out = jnp.dot(x.astype(BF16), ones, preferred_element_type=F32)
        res = []
        for n in range(len(xs)):
            res.append(jnp.concatenate(
                [out[(nt * n + j) * C_HEADDIM:(nt * n + j + 1) * C_HEADDIM] for j in range(nt)], axis=1))
        return res

    def step(i, states):
        rows = []
        for refs, b in chains:
            t = (t_len - 1 - i) if refs[7] else i
            rows.append([ref[b, pl.ds(t, 1), :] for ref in refs[:6]] + [t])
        sa = [head_sum([s * (-row[4])], True)[0] for s, row in zip(states, rows)]
        vcol = [head_sum([diag * row[3]], False)[0] for row in rows]
        new = [s * row[1] + a * (row[4] * row[5]) + vc * row[2]
               for s, row, a, vc in zip(states, rows, sa, vcol)]
        ys = [head_sum([s * row[0]], False)[0] for s, row in zip(new, rows)]
        for n, (row, y) in enumerate(zip(rows, ys)):
            pltpu.store(y_ref.at[n], y, mask=lane % C_HEADDIM == row[6] % C_HEADDIM)
        return new

    nblk = t_len // C_HEADDIM

    def flush(walk):
        for n, (refs, b) in enumerate(chains):
            blk = (nblk - 1 - walk) if refs[7] else walk
            parts = []
            for j in range(nt):
                tr = y_ref[n, :, j * LANES:(j + 1) * LANES].T
                parts += [tr[:C_HEADDIM], tr[C_HEADDIM:]]
            refs[6][b, blk * C_HEADDIM:(blk + 1) * C_HEADDIM, :] = jnp.concatenate(parts, axis=1)

    def body(g, states):
        states = list(states)
        for u in range(RWKV_UNROLL):
            states = step(g * RWKV_UNROLL + u, states)
        return tuple(states)

    per_blk = C_HEADDIM // RWKV_UNROLL
    states = tuple(s_ref[n] for n in range(len(chains)))
    for walk in range(nblk):
        states = lax.fori_loop(walk * per_blk, (walk + 1) * per_blk, body, states)
        flush(walk)
    for n in range(len(chains)):
        s_ref[n] = states[n]


def _rwkv_scan(r, v, kk, w_f, kt_f, al_f, w_b, kt_b, al_b, nc_ctx):
    bsz, n, width = r.shape
    nc = n // CHUNK
    fwd = pl.BlockSpec((bsz, CHUNK, width), lambda c: (0, c, 0))
    bwd = pl.BlockSpec((bsz, CHUNK, width), lambda c: (0, _chunk_order(c, nc_ctx, nc, True), 0))
    return pl.pallas_call(
        _rwkv_kernel,
        grid=(nc,),
        in_specs=[fwd] * 6 + [bwd] * 6,
        out_specs=[fwd, bwd],
        out_shape=[jax.ShapeDtypeStruct((bsz, n, width), F32)] * 2,
        scratch_shapes=[pltpu.VMEM((2 * bsz, C_HEADDIM, width), F32),
                        pltpu.VMEM((2 * bsz, C_HEADDIM, width), F32)],
        compiler_params=_cparams(("arbitrary",)),
        name="rwkv_scan",
    )(r, w_f, kt_f, v, kk, al_f, r, w_b, kt_b, v, kk, al_b)


def _lru_gate_kernel(xc_ref, w_ref, b_ref, lam_ref, a0_o, b0_o, a1_o, b1_o):
    xc = xc_ref[0]
    z = jnp.dot(xc, w_ref[...], preferred_element_type=F32) + b_ref[...]
    for d, (a_o, b_o) in enumerate(((a0_o, b0_o), (a1_o, b1_o))):
        gr = _sigmoid(z[:, 2 * d * D_WIDTH:(2 * d + 1) * D_WIDTH])
        gi = _sigmoid(z[:, (2 * d + 1) * D_WIDTH:(2 * d + 2) * D_WIDTH])
        log_a = -LRU_C * gr * _softplus(-lam_ref[d:d + 1, :])
        th = jnp.tanh(log_a)
        one_minus_a2 = -2.0 * th / (1.0 - th)
        a_o[0] = jnp.exp(log_a)
        b_o[0] = jnp.sqrt(one_minus_a2) * (gi * xc)


def _lru_gates(xc, w, b, lam):
    bsz, n, width = xc.shape
    nt = n // ROWS
    out = pl.BlockSpec((1, ROWS, width), lambda b_, i: (b_, i, 0))
    return pl.pallas_call(
        _lru_gate_kernel,
        grid=(bsz, nt),
        in_specs=[out,
                  pl.BlockSpec(w.shape, lambda b_, i: (0, 0)),
                  pl.BlockSpec(b.shape, lambda b_, i: (0, 0)),
                  pl.BlockSpec(lam.shape, lambda b_, i: (0, 0))],
        out_specs=[out] * 4,
        out_shape=[jax.ShapeDtypeStruct((bsz, n, width), F32)] * 4,
        compiler_params=_cparams(("parallel", "arbitrary")),
        name="lru_gates",
    )(xc, w, b, lam)


def _lru_kernel(a_ref, b_ref, o_ref, h_ref, *, reverse):
    c = pl.program_id(0)

    @pl.when(c == 0)
    def _():
        h_ref[...] = jnp.zeros_like(h_ref)

    nb = a_ref.shape[0]
    t_len = a_ref.shape[1]

    def step(i, hs):
        t = (t_len - 1 - i) if reverse else i
        new = []
        for b in range(nb):
            h = a_ref[b, pl.ds(t, 1), :] * hs[b] + b_ref[b, pl.ds(t, 1), :]
            o_ref[b, pl.ds(t, 1), :] = h
            new.append(h)
        return tuple(new)

    hs = lax.fori_loop(0, t_len, step, tuple(h_ref[b] for b in range(nb)))
    for b in range(nb):
        h_ref[b] = hs[b]


def _lru_scan(a, b, nc_ctx, reverse):
    bsz, n, width = a.shape
    nc = n // CHUNK
    order = lambda c: _chunk_order(c, nc_ctx, nc, reverse)
    spec = pl.BlockSpec((bsz, CHUNK, width), lambda c: (0, order(c), 0))
    return pl.pallas_call(
        functools.partial(_lru_kernel, reverse=reverse),
        grid=(nc,),
        in_specs=[spec, spec],
        out_specs=spec,
        out_shape=jax.ShapeDtypeStruct((bsz, n, width), F32),
        scratch_shapes=[pltpu.VMEM((bsz, 1, width), F32)],
        compiler_params=_cparams(("arbitrary",)),
        name="lru_scan",
    )(a, b)


def _finish1_kernel(yf_ref, yb_ref, uf_ref, ub_ref, r_ref, kt0_ref, kt1_ref, v_ref, g_ref, gb_ref,
                    x_ref, mod_ref, rk_ref, lw_ref, lb_ref, w_ref, out_ref):
    y = yf_ref[0] + yb_ref[0]
    hs = _head_ones(C_WIDTH, C_HEADDIM)
    mean = jnp.dot(y, hs, precision=HIGHEST, preferred_element_type=F32) * (1.0 / C_HEADDIM)
    yc = y - mean
    var = jnp.dot(yc * yc, hs, precision=HIGHEST, preferred_element_type=F32) * (1.0 / C_HEADDIM)
    yn = yc * lax.rsqrt(var + RWKV_GN_EPS) * lw_ref[...] + lb_ref[...]
    kb = 0.5 * (kt0_ref[0] + kt1_ref[0])
    bonus = jnp.dot(r_ref[0] * kb * rk_ref[...], hs, precision=HIGHEST, preferred_element_type=F32)
    yn = yn + bonus * v_ref[0]
    yc_ = yn * g_ref[0]
    yd = (uf_ref[0] + ub_ref[0]) * _gelu(gb_ref[0])
    f = jnp.concatenate([yc_, yd], axis=1)
    g1 = mod_ref[0, :, 2 * D_MODEL:3 * D_MODEL]
    out_ref[0] = x_ref[0] + g1 * jnp.dot(f.astype(BF16), w_ref[...], preferred_element_type=F32)


def _finish1(yf, yb, uf, ub, r, kt0, kt1, v, g, gate_br, xs, modsel, r_k, ln_w, ln_b, w_out, ctx_tiles):
    bsz, n, d = xs.shape
    nt = n // ROWS
    half = pl.BlockSpec((1, ROWS, 512), lambda b, i: (b, i, 0))
    vec = pl.BlockSpec((1, 512), lambda b, i: (0, 0))
    return pl.pallas_call(
        _finish1_kernel,
        grid=(bsz, nt),
        in_specs=[half] * 10 + [pl.BlockSpec((1, ROWS, d), lambda b, i: (b, i, 0)),
                                _mod_spec(ctx_tiles), vec, vec, vec,
                                pl.BlockSpec((d, d), lambda b, i: (0, 0))],
        out_specs=pl.BlockSpec((1, ROWS, d), lambda b, i: (b, i, 0)),
        out_shape=jax.ShapeDtypeStruct((bsz, n, d), F32),
        compiler_params=_cparams(("parallel", "arbitrary")),
        name="finish_rwkv_lru",
    )(yf, yb, uf, ub, r, kt0, kt1, v, g, gate_br, xs, modsel, r_k, ln_w, ln_b, w_out)


def _max_arg(s):
    tiles = s.shape[0] // SUBLANES
    vals = [s[i * SUBLANES:(i + 1) * SUBLANES] for i in range(tiles)]
    row = lax.broadcasted_iota(jnp.int32, (SUBLANES, s.shape[1]), 0)
    ids = [row + i * SUBLANES for i in range(tiles)]
    while len(vals) > 1:
        nv, ni = [], []
        for a in range(0, len(vals) - 1, 2):
            take = vals[a + 1] > vals[a]
            nv.append(jnp.where(take, vals[a + 1], vals[a]))
            ni.append(jnp.where(take, ids[a + 1], ids[a]))
        if len(vals) % 2:
            nv.append(vals[-1])
            ni.append(ids[-1])
        vals, ids = nv, ni
    mx = jnp.max(vals[0], axis=0, keepdims=True)
    am = jnp.min(jnp.where(vals[0] == mx, ids[0], s.shape[0]), axis=0, keepdims=True)
    return mx, am


def _topk_rows(s, payload=None):
    iota = lax.broadcasted_iota(jnp.int32, s.shape, 0)
    vals, idxs = [], []
    for _ in range(PEER_TOPK):
        mx, am = _max_arg(s)
        sel = iota == am
        vals.append(mx)
        if payload is None:
            idxs.append(am)
        else:
            idxs.append(jnp.max(jnp.where(sel, payload, -1), axis=0, keepdims=True))
        s = jnp.where(sel, -jnp.inf, s)
    return vals, idxs


def _peer_candidates(sv0, si0, sv1, si1):
    grp = SUBLANES
    v1_all, i1_all = jnp.concatenate(sv1, axis=0), jnp.concatenate(si1, axis=0)
    v1_lo, i1_lo = v1_all[:grp], i1_all[:grp]
    row = lax.broadcasted_iota(jnp.int32, v1_lo.shape, 0)
    cand, cidx = [sv0[0] + v1_all], [si0[0] * N_KEYS + i1_all]
    for a in range(1, grp):
        cand.append(jnp.where(row < PEER_TOPK // (a + 1), sv0[a] + v1_lo, -jnp.inf))
        cidx.append(si0[a] * N_KEYS + i1_lo)
    cand.append(jnp.concatenate(sv0[grp:], axis=0) + sv1[0])
    cidx.append(jnp.concatenate(si0[grp:], axis=0) * N_KEYS + si1[0])
    return jnp.concatenate(cand, axis=0), jnp.concatenate(cidx, axis=0)


def _select_head(hn, wq_ref, keys_ref, k):
    half = PEER_DK // 2
    q = jnp.dot(hn, wq_ref[k], preferred_element_type=F32)
    sv, si = [], []
    for p in range(2):
        s = lax.dot_general(keys_ref[2 * k + p], q[:, p * half:(p + 1) * half],
                            (((1,), (1,)), ((), ())), preferred_element_type=F32)
        v, ix = _topk_rows(s)
        sv.append(v)
        si.append(ix)
    cand, cidx = _peer_candidates(sv[0], si[0], sv[1], si[1])
    best, eid = _topk_rows(cand, cidx)
    e = jnp.exp(jnp.concatenate(best, axis=0) - best[0])
    return jnp.concatenate(eid, axis=0), e / jnp.sum(e, axis=0, keepdims=True)


def _peer_kernel(x_ref, xn_ref, nw_ref, mod_ref, modn_ref, wq_ref, keys_ref, uv_hbm, uvw_hbm, o_ref,
                 hbuf, hnext, eid_v, eid_t, gate_s, *scratch, nblk):
    bufs = scratch[:PEER_NBUF]
    idx_smem, gsem, isem = scratch[PEER_NBUF:]
    i = pl.program_id(0)
    tb = x_ref.shape[0]
    ahead = PEER_NBUF - 1
    cur = i % 2
    nxt = 1 - cur
    ngroups = tb // PEER_NBUF
    gpp = ngroups // PEER_HEADS
    part = PEER_SLOTS // (2 * ROW_TILES)

    def select_piece(hn, k, slot):
        eid, gate = _select_head(hn, wq_ref, keys_ref, k)
        eid_v[pl.ds(pl.multiple_of(k * PEER_TOPK, PEER_TOPK), PEER_TOPK), :] = eid
        gate_s[slot, pl.ds(pl.multiple_of(k * PEER_TOPK, PEER_TOPK), PEER_TOPK), :] = gate

    def ids_to_smem(slot):
        eid_t[...] = eid_v[...].T
        return pltpu.make_async_copy(eid_t, idx_smem.at[slot], isem.at[slot])

    def issue(islot, t, n, lo, hi):
        for j in range(lo, hi):
            e = idx_smem[islot, t, j]
            pltpu.make_async_copy(uv_hbm.at[e], bufs[n].at[:, pl.ds(j * ROW_TILES, ROW_TILES), :],
                                  gsem.at[n]).start(priority=j % 2)

    hbuf[...] = _norm_mod(x_ref[...], nw_ref[...], mod_ref, 3)

    @pl.when(i == 0)
    def _():
        h0 = hbuf[...].astype(BF16)

        def piece0(k, carry):
            select_piece(h0, k, 0)
            return carry

        lax.fori_loop(0, PEER_HEADS, piece0, 0)
        cp = ids_to_smem(0)
        cp.start()
        cp.wait()
        for t0 in range(ahead):
            issue(0, t0, t0, 0, PEER_SLOTS)

    hnext[...] = _norm_mod(xn_ref[...], nw_ref[...], modn_ref, 3).astype(BF16)
    g2 = mod_ref[0, :, 5 * D_MODEL:6 * D_MODEL]
    lane = lax.broadcasted_iota(jnp.int32, (PEER_SLOTS, tb), 1)

    def consume(t, n, issue_part):
        pltpu.make_async_copy(uvw_hbm.at[pl.ds(0, 2)], bufs[n], gsem.at[n]).wait()
        xrow = hbuf[pl.ds(t, 1), :]
        acc = jnp.zeros((PEER_SLOTS, LANES), F32)
        for s in range(ROW_TILES):
            issue_part(s)
            us = bufs[n][0, pl.ds(s, PEER_SLOTS, stride=ROW_TILES), :]
            acc = acc + us * xrow[:, s * LANES:(s + 1) * LANES]
        act = jnp.sum(acc, axis=1, keepdims=True)
        gcol = jnp.sum(jnp.where(lane == t, gate_s[cur], 0.0), axis=1, keepdims=True)
        coef = jnp.broadcast_to(gcol * _gelu(act), (PEER_SLOTS, LANES))
        outs = []
        for s in range(ROW_TILES):
            issue_part(ROW_TILES + s)
            vs = bufs[n][1, pl.ds(s, PEER_SLOTS, stride=ROW_TILES), :]
            outs.append(jnp.sum(vs * coef, axis=0, keepdims=True))
        orow = jnp.concatenate(outs, axis=1)
        o_ref[pl.ds(t, 1), :] = x_ref[pl.ds(t, 1), :] + g2 * orow

    def group(g):
        for n in range(PEER_NBUF):
            t = g * PEER_NBUF + n
            consume(t, n, lambda k, t=t, n=n: issue(cur, t + ahead, (n + ahead) % PEER_NBUF,
                                                    k * part, (k + 1) * part))

    more = i + 1 < nblk

    def piece_and_groups(k, carry):
        @pl.when(more)
        def _():
            select_piece(hnext[...], k, nxt)

        for gg in range(gpp):
            group(k * gpp + gg)
        return carry

    lax.fori_loop(0, PEER_HEADS - 1, piece_and_groups, 0)

    @pl.when(more)
    def _():
        select_piece(hnext[...], PEER_HEADS - 1, nxt)
        ids_to_smem(nxt).start()

    for gg in range(gpp - 1):
        group((PEER_HEADS - 1) * gpp + gg)

    @pl.when(more)
    def _():
        pltpu.make_async_copy(eid_t, idx_smem.at[nxt], isem.at[nxt]).wait()

    for n in range(PEER_NBUF):
        t = (ngroups - 1) * PEER_NBUF + n
        if n == 0:
            consume(t, n, lambda k, t=t: issue(cur, t + ahead, ahead, k * part, (k + 1) * part))
        else:
            @pl.when(more)
            def _():
                issue(nxt, n - 1, n - 1, 0, PEER_SLOTS)
            consume(t, n, lambda k: None)


def _peer(xs, nw, modsel, wq, keys, u_tab, v_tab, ctx_tiles):
    bsz, n, d = xs.shape
    ntok = bsz * n
    nblk = ntok // PEER_TB
    tiles_per_batch = n // PEER_TB
    ctx_blocks = ctx_tiles * ROWS // PEER_TB
    slab = PEER_SLOTS * ROW_TILES
    assert PEER_TB % (PEER_NBUF * PEER_HEADS) == 0
    ne = u_tab.shape[0]
    uv = jnp.stack([u_tab.reshape(ne, ROW_TILES, LANES), v_tab.reshape(ne, ROW_TILES, LANES)], axis=1)
    uvw = uv.reshape(-1, slab, LANES)
    wq3 = wq.reshape(d, PEER_HEADS, PEER_DK).transpose(1, 0, 2)
    nxt_blk = lambda i: jnp.minimum(i + 1, nblk - 1)
    mod_of = lambda i: (2 * (i // tiles_per_batch) + jnp.where(i % tiles_per_batch >= ctx_blocks, 1, 0), 0, 0)
    out = pl.pallas_call(
        functools.partial(_peer_kernel, nblk=nblk),
        grid=(nblk,),
        in_specs=[pl.BlockSpec((PEER_TB, d), lambda i: (i, 0)),
                  pl.BlockSpec((PEER_TB, d), lambda i: (nxt_blk(i), 0)),
                  pl.BlockSpec((1, d), lambda i: (0, 0)),
                  pl.BlockSpec((1, 1, 6 * D_MODEL), mod_of),
                  pl.BlockSpec((1, 1, 6 * D_MODEL), lambda i: mod_of(nxt_blk(i))),
                  pl.BlockSpec(wq3.shape, lambda i: (0, 0, 0)),
                  pl.BlockSpec(keys.shape, lambda i: (0, 0, 0)),
                  pl.BlockSpec(memory_space=pl.ANY),
                  pl.BlockSpec(memory_space=pl.ANY)],
        out_specs=pl.BlockSpec((PEER_TB, d), lambda i: (i, 0)),
        out_shape=jax.ShapeDtypeStruct((ntok, d), F32),
        scratch_shapes=[pltpu.VMEM((PEER_TB, d), F32),
                        pltpu.VMEM((PEER_TB, d), BF16),
                        pltpu.VMEM((PEER_SLOTS, PEER_TB), jnp.int32),
                        pltpu.VMEM((PEER_TB, PEER_SLOTS), jnp.int32),
                        pltpu.VMEM((2, PEER_SLOTS, PEER_TB), F32),
                        *[pltpu.VMEM((2, slab, LANES), F32) for _ in range(PEER_NBUF)],
                        pltpu.SMEM((2, PEER_TB, PEER_SLOTS), jnp.int32),
                        pltpu.SemaphoreType.DMA((PEER_NBUF,)),
                        pltpu.SemaphoreType.DMA((2,))],
        compiler_params=_cparams(("arbitrary",)),
        name="peer",
    )(xs.reshape(ntok, d), xs.reshape(ntok, d), nw, modsel, modsel, wq3, keys, uv, uvw)
    return out.reshape(bsz, n, d)


def _final_kernel(x_ref, w_ref, o_ref):
    x = x_ref[0]
    ms = jnp.mean(x * x, axis=-1, keepdims=True)
    o_ref[0] = x * lax.rsqrt(ms + EPS) * w_ref[...]


def _final_norm(xs, w, ctx_tiles, seq):
    bsz, n, d = xs.shape
    return pl.pallas_call(
        _final_kernel,
        grid=(bsz, seq // ROWS),
        in_specs=[pl.BlockSpec((1, ROWS, d), lambda b, i: (b, i + ctx_tiles, 0)),
                  pl.BlockSpec((1, d), lambda b, i: (0, 0))],
        out_specs=pl.BlockSpec((1, ROWS, d), lambda b, i: (b, i, 0)),
        out_shape=jax.ShapeDtypeStruct((bsz, seq, d), F32),
        compiler_params=_cparams(("parallel", "arbitrary")),
        name="final_norm",
    )(xs, w)


def _block_diag(w):
    nb, d, e = w.shape
    eye = jnp.eye(nb, dtype=w.dtype)
    return (eye[:, None, :, None] * w[:, :, None, :]).reshape(nb * d, nb * e)


def _mixer0(xs, modsel, norm1, w_in, w_out, i_bias, f_bias, mlstm_norm, conv_w, conv_b, dt_bias,
            a_log, d_skip, ssd_norm, ctx_tiles, nc_ctx):
    q0, k0, v0, o0, ig0, fg0, z0, xbc0, dt0, end = 0, 256, 512, 1024, 1536, 1544, 1552, 2064, 3088, 3104
    pad = jnp.zeros((D_MODEL, LANES - 32), w_in.dtype)
    w_cat = jnp.concatenate([w_in[:, q0:ig0], w_in[:, xbc0:dt0], w_in[:, z0:xbc0],
                             w_in[:, ig0:z0], w_in[:, dt0:end], pad], axis=1).astype(BF16)
    qkvo, zx, gates = _project(xs, norm1, modsel, w_cat, (1536, 1536, LANES), ctx_tiles)
    gates_row = jnp.swapaxes(gates[:, :, :32], 1, 2)
    bias = jnp.concatenate([i_bias.reshape(-1), f_bias.reshape(-1), dt_bias.reshape(-1)])
    bias_col = jnp.pad(bias, (0, LANES - 32)).reshape(1, LANES)
    bias_row = bias.reshape(32, 1)
    alog = jnp.concatenate([jnp.zeros((16,), F32), a_log.reshape(-1)])
    alog_col = jnp.pad(alog, (0, LANES - 32)).reshape(1, LANES)
    alog_row = alog.reshape(32, 1)
    xbc_act = _conv4(zx, 1024, conv_w, conv_b.reshape(1, -1), ctx_tiles, True)
    hf = _mlstm_scan(qkvo, gates, gates_row, bias_col, bias_row, nc_ctx, 0)
    hb = _mlstm_scan(qkvo, gates, gates_row, bias_col, bias_row, nc_ctx, 1)
    yf = _ssd_scan(xbc_act, gates, gates_row, bias_col, bias_row, alog_col, alog_row, nc_ctx, 0)
    yb = _ssd_scan(xbc_act, gates, gates_row, bias_col, bias_row, alog_col, alog_row, nc_ctx, 1)
    dskip = jnp.repeat(d_skip, B_HEADDIM).reshape(1, -1)
    return _finish0(hf, hb, yf, yb, qkvo, zx, xbc_act, xs, modsel, mlstm_norm.reshape(1, -1),
                    ssd_norm.reshape(1, -1), dskip, w_out.astype(BF16), ctx_tiles)


def _mixer1(xs, modsel, norm1, w_in, w_out, mu, w0, w_up, a0, a_up, g_up, k_k, k_a, r_k, ln_w, ln_b,
            conv_w, conv_b, lam, wa, ba, wi, bi, ctx_tiles, nc_ctx):
    pr, gate_br, x_br = _project(xs, norm1, modsel, w_in.astype(BF16), (RWKV_COLS, D_WIDTH, D_WIDTH),
                                 ctx_tiles)
    r, v, kk, g, w_f, w_b, kt_f, kt_b, al_f, al_b = _rwkv_prep(
        pr, mu.reshape(1, -1), w0, w_up, a0, a_up, g_up, k_k.reshape(1, -1), k_a.reshape(1, -1), ctx_tiles)
    yf, yb = _rwkv_scan(r, v, kk, w_f, kt_f, al_f, w_b, kt_b, al_b, nc_ctx)
    xc = _conv4(x_br, D_WIDTH, conv_w, conv_b.reshape(1, -1), ctx_tiles, False)
    w_gate = jnp.concatenate([_block_diag(wa[0]), _block_diag(wi[0]),
                              _block_diag(wa[1]), _block_diag(wi[1])], axis=1)
    b_gate = jnp.concatenate([ba[0], bi[0], ba[1], bi[1]]).reshape(1, -1)
    a_f, b_f, a_b, b_b = _lru_gates(xc, w_gate, b_gate, lam)
    uf = _lru_scan(a_f, b_f, nc_ctx, False)
    ub = _lru_scan(a_b, b_b, nc_ctx, True)
    return _finish1(yf, yb, uf, ub, r, kt_f, kt_b, v, g, gate_br, xs, modsel, r_k.reshape(1, -1),
                    ln_w.reshape(1, -1), ln_b.reshape(1, -1), w_out.astype(BF16), ctx_tiles)


def kernel(x, c, ctx, c_ctx, mod_w, mod_b, norm1, norm2, peer_wq, peer_keys, peer_u, peer_v, ev_w_in, ev_w_out, ev_mlstm_i_bias, ev_mlstm_f_bias, ev_mlstm_norm, ev_ssd_conv_w, ev_ssd_conv_b, ev_ssd_dt_bias, ev_ssd_a_log, ev_ssd_d, ev_ssd_norm, od_w_in, od_w_out, od_rwkv_mu, od_rwkv_w0, od_rwkv_w_up, od_rwkv_a0, od_rwkv_a_up, od_rwkv_g_up, od_rwkv_k_k, od_rwkv_k_a, od_rwkv_r_k, od_rwkv_ln_w, od_rwkv_ln_b, od_lru_conv_w, od_lru_conv_b, od_lru_lambda, od_lru_wa, od_lru_ba, od_lru_wi, od_lru_bi, final_norm):
    bsz, seq, d = x.shape
    ctx_len = ctx.shape[1]
    depth = mod_w.shape[0]
    assert d == D_MODEL and ctx_len == ROWS and seq % ROWS == 0 and bsz < SUBLANES
    ctx_tiles = ctx_len // ROWS
    nc_ctx = ctx_len // CHUNK
    xs = jnp.concatenate([ctx, x], axis=1)
    srows = jnp.concatenate([c, c_ctx[None, :], jnp.zeros((SUBLANES - bsz - 1, d), F32)], axis=0)
    for i in range(depth):
        mod = _modulation(srows, mod_w[i], mod_b[i].reshape(1, -1))
        modsel = jnp.stack([jnp.broadcast_to(mod[bsz], (bsz, 6 * d)), mod[:bsz]], axis=1)
        modsel = modsel.reshape(2 * bsz, 1, 6 * d)
        j = i // 2
        n1 = norm1[i].reshape(1, -1)
        if i % 2 == 0:
            xs = _mixer0(xs, modsel, n1, ev_w_in[j], ev_w_out[j], ev_mlstm_i_bias[j], ev_mlstm_f_bias[j],
                         ev_mlstm_norm[j], ev_ssd_conv_w[j], ev_ssd_conv_b[j], ev_ssd_dt_bias[j],
                         ev_ssd_a_log[j], ev_ssd_d[j], ev_ssd_norm[j], ctx_tiles, nc_ctx)
        else:
            xs = _mixer1(xs, modsel, n1, od_w_in[j], od_w_out[j], od_rwkv_mu[j], od_rwkv_w0[j],
                         od_rwkv_w_up[j], od_rwkv_a0[j], od_rwkv_a_up[j], od_rwkv_g_up[j], od_rwkv_k_k[j],
                         od_rwkv_k_a[j], od_rwkv_r_k[j].reshape(-1), od_rwkv_ln_w[j], od_rwkv_ln_b[j],
                         od_lru_conv_w[j], od_lru_conv_b[j], od_lru_lambda[j], od_lru_wa[j], od_lru_ba[j],
                         od_lru_wi[j], od_lru_bi[j], ctx_tiles, nc_ctx)
        keys = peer_keys[i].reshape(2 * PEER_HEADS, N_KEYS, PEER_DK // 2)
        xs = _peer(xs, norm2[i].reshape(1, -1), modsel, peer_wq[i].astype(BF16), keys,
                   peer_u[i], peer_v[i], ctx_tiles)
    return _final_norm(xs, final_norm.reshape(1, -1), ctx_tiles, seq)
```

```python
import functools
import math

import jax
import jax.numpy as jnp
from jax import lax
from jax.experimental import pallas as pl
from jax.experimental.pallas import tpu as pltpu

F32 = jnp.float32
BF16 = jnp.bfloat16
HIGHEST = lax.Precision.HIGHEST

D_MODEL = 1024
EPS = 1e-6
CHUNK = 128
ROWS = 256
GRID_W = 64
LANES = 128
SUBLANES = 8
MIB = 1024 * 1024

A_HEADS, A_DQK, A_DV = 4, 64, 128
A_WIDTH = A_HEADS * A_DV
B_HEADS, B_HEADDIM, B_GROUPS, B_STATE = 8, 64, 2, 128
B_WIDTH = B_HEADS * B_HEADDIM
C_HEADS, C_HEADDIM = 8, 64
C_WIDTH = C_HEADS * C_HEADDIM
C_LORA_W, C_LORA_A, C_LORA_G = 64, 64, 128
RWKV_COLS = 3 * C_WIDTH + C_LORA_W + C_LORA_A + C_LORA_G
RWKV_W_SCALE = math.exp(-0.5)
RWKV_GN_EPS = 64e-5
D_WIDTH = 512
LRU_C = 8.0
PEER_HEADS, PEER_DK, N_KEYS, PEER_TOPK = 8, 256, 128, 16
PEER_SLOTS = PEER_HEADS * PEER_TOPK
PEER_TB = 128
PEER_NBUF = 8
ROW_TILES = D_MODEL // LANES


def _cparams(sem, vmem_mib=48):
    return pltpu.CompilerParams(dimension_semantics=sem, vmem_limit_bytes=vmem_mib * MIB)


def _softplus(x):
    return jnp.maximum(x, 0.0) + jnp.log1p(jnp.exp(-jnp.abs(x)))


def _sigmoid(x):
    return 1.0 / (1.0 + jnp.exp(-x))


def _silu(x):
    return x * _sigmoid(x)


def _gelu(x):
    return 0.5 * x * (1.0 + lax.erf(x * (1.0 / math.sqrt(2.0))))


def _norm_mod(x, nw, mod_ref, slot):
    ms = jnp.mean(x * x, axis=-1, keepdims=True)
    y = x * lax.rsqrt(ms + EPS) * nw
    sh = mod_ref[0, :, slot * D_MODEL:(slot + 1) * D_MODEL]
    sc = mod_ref[0, :, (slot + 1) * D_MODEL:(slot + 2) * D_MODEL]
    return y * (1.0 + sc) + sh


def _chunk_order(c, nc_ctx, nc, reverse):
    if not reverse:
        return c
    return jnp.where(c < nc_ctx, nc_ctx - 1 - c, nc + nc_ctx - 1 - c)


def _mod_kernel(s_ref, w_ref, b_ref, o_ref):
    s = _silu(s_ref[...])
    o_ref[...] = jnp.dot(s, w_ref[...], precision=HIGHEST, preferred_element_type=F32) + b_ref[...]


def _modulation(srows, w, b):
    d = srows.shape[1]
    nt = w.shape[1] // d
    return pl.pallas_call(
        _mod_kernel,
        grid=(nt,),
        in_specs=[pl.BlockSpec((SUBLANES, d), lambda j: (0, 0)),
                  pl.BlockSpec((d, d), lambda j: (0, j)),
                  pl.BlockSpec((1, d), lambda j: (0, j))],
        out_specs=pl.BlockSpec((SUBLANES, d), lambda j: (0, j)),
        out_shape=jax.ShapeDtypeStruct((SUBLANES, w.shape[1]), F32),
        compiler_params=_cparams(("arbitrary",)),
        name="modulation",
    )(srows, w, b)


def _proj_kernel(x_ref, nw_ref, mod_ref, w_ref, *o_refs, widths):
    h = _norm_mod(x_ref[0], nw_ref[...], mod_ref, 0)
    out = jnp.dot(h.astype(BF16), w_ref[...], preferred_element_type=F32)
    off = 0
    for o_ref, wd in zip(o_refs, widths):
        o_ref[0] = out[:, off:off + wd]
        off += wd


def _mod_spec(ctx_tiles):
    return pl.BlockSpec((1, 1, 6 * D_MODEL),
                        lambda b, i: (2 * b + jnp.where(i >= ctx_tiles, 1, 0), 0, 0))


def _project(xs, nw, modsel, w, widths, ctx_tiles):
    bsz, n, d = xs.shape
    nt = n // ROWS
    return pl.pallas_call(
        functools.partial(_proj_kernel, widths=widths),
        grid=(bsz, nt),
        in_specs=[pl.BlockSpec((1, ROWS, d), lambda b, i: (b, i, 0)),
                  pl.BlockSpec((1, d), lambda b, i: (0, 0)),
                  _mod_spec(ctx_tiles),
                  pl.BlockSpec(w.shape, lambda b, i: (0, 0))],
        out_specs=[pl.BlockSpec((1, ROWS, wd), lambda b, i: (b, i, 0)) for wd in widths],
        out_shape=[jax.ShapeDtypeStruct((bsz, n, wd), F32) for wd in widths],
        compiler_params=_cparams(("parallel", "arbitrary")),
        name="norm_mod_project",
    )(xs, nw, modsel, w)


def _conv_kernel(x_ref, p_ref, n_ref, w_ref, b_ref, o_ref, *, ctx_tiles, ntiles, act):
    i = pl.program_id(1)
    x = x_ref[0]
    rows = x.shape[0]
    prev_ok = jnp.logical_and(i != 0, i != ctx_tiles)
    next_ok = jnp.logical_and(i != ctx_tiles - 1, i != ntiles - 1)
    p = jnp.where(prev_ok, p_ref[0], 0.0)
    nx = jnp.where(next_ok, n_ref[0], 0.0)
    row = lax.broadcasted_iota(jnp.int32, (rows, 1), 0)
    xm1 = jnp.where(row == 0, p[7:8], pltpu.roll(x, 1, 0))
    xm2 = jnp.where(row == 0, p[6:7], jnp.where(row == 1, p[7:8], pltpu.roll(x, 2, 0)))
    xp1 = jnp.where(row == rows - 1, nx[0:1], pltpu.roll(x, rows - 1, 0))
    w = w_ref[...]
    y = b_ref[...] + xm2 * w[0:1] + xm1 * w[1:2] + x * w[2:3] + xp1 * w[3:4]
    o_ref[0] = _silu(y) if act else y


def _conv4(x, width, w, b, ctx_tiles, act):
    bsz, n, _ = x.shape
    nt = n // ROWS
    hb = ROWS // SUBLANES
    nh = n // SUBLANES
    return pl.pallas_call(
        functools.partial(_conv_kernel, ctx_tiles=ctx_tiles, ntiles=nt, act=act),
        grid=(bsz, nt),
        in_specs=[pl.BlockSpec((1, ROWS, width), lambda b_, i: (b_, i, 0)),
                  pl.BlockSpec((1, SUBLANES, width), lambda b_, i: (b_, jnp.maximum(i * hb - 1, 0), 0)),
                  pl.BlockSpec((1, SUBLANES, width),
                               lambda b_, i: (b_, jnp.minimum((i + 1) * hb, nh - 1), 0)),
                  pl.BlockSpec((4, width), lambda b_, i: (0, 0)),
                  pl.BlockSpec((1, width), lambda b_, i: (0, 0))],
        out_specs=pl.BlockSpec((1, ROWS, width), lambda b_, i: (b_, i, 0)),
        out_shape=jax.ShapeDtypeStruct((bsz, n, width), F32),
        compiler_params=_cparams(("parallel", "arbitrary")),
        name="conv4",
    )(x, x, x, w, b)


def _tri(reverse):
    t = lax.broadcasted_iota(jnp.int32, (CHUNK, CHUNK), 0)
    s = lax.broadcasted_iota(jnp.int32, (CHUNK, CHUNK), 1)
    return (s >= t) if reverse else (s <= t)


def _cumsums(col, row, mask):
    mf = mask.astype(F32)
    b_col = jnp.dot(mf, col, precision=HIGHEST, preferred_element_type=F32)
    b_row = lax.dot_general(row, mf, (((1,), (1,)), ((), ())), precision=HIGHEST,
                            preferred_element_type=F32)
    return b_col, b_row


def _mlstm_kernel(qkv_ref, gc_ref, gr_ref, bc_ref, br_ref, o_ref, ct_ref, n_ref, m_ref, *,
                  direction):
    c = pl.program_id(1)

    @pl.when(c == 0)
    def _():
        ct_ref[...] = jnp.zeros_like(ct_ref)
        n_ref[...] = jnp.zeros_like(n_ref)
        m_ref[...] = jnp.zeros_like(m_ref)

    reverse = direction == 1
    mask = _tri(reverse)
    gc = gc_ref[0] + bc_ref[...]
    gr = gr_ref[0] + br_ref[...]
    d4 = direction * A_HEADS
    li_col = gc[:, d4:d4 + A_HEADS]
    lf_col = -_softplus(-gc[:, 8 + d4:8 + d4 + A_HEADS])
    li_row = gr[d4:d4 + A_HEADS, :]
    lf_row = -_softplus(-gr[8 + d4:8 + d4 + A_HEADS, :])
    b_col, b_row = _cumsums(lf_col, lf_row, mask)
    tot = jnp.sum(lf_col, axis=0, keepdims=True)
    hs = range(A_HEADS)
    nt_dims, tn_dims = (((1,), (1,)), ((), ())), (((0,), (0,)), ((), ()))
    q = [qkv_ref[0, :, h * A_DQK:(h + 1) * A_DQK] * (A_DQK ** -0.5) for h in hs]
    k = [qkv_ref[0, :, A_HEADS * A_DQK + h * A_DQK:A_HEADS * A_DQK + (h + 1) * A_DQK] for h in hs]
    v = [qkv_ref[0, :, 2 * A_HEADS * A_DQK + h * A_DV:2 * A_HEADS * A_DQK + (h + 1) * A_DV] for h in hs]
    ct_in = [ct_ref[h] for h in hs]
    n_in = [n_ref[h] for h in hs]
    m_prev = [m_ref[h] for h in hs]
    bc = [b_col[:, h:h + 1] for h in hs]
    th = [tot[:, h:h + 1] for h in hs]
    dmat = [jnp.where(mask, bc[h] - b_row[h:h + 1, :] + li_row[h:h + 1, :], -jnp.inf) for h in hs]
    inter = [bc[h] + m_prev[h] for h in hs]
    m_t = [jnp.maximum(inter[h], jnp.max(dmat[h], axis=1, keepdims=True)) for h in hs]
    qk = [lax.dot_general(q[h], k[h], nt_dims, preferred_element_type=F32) for h in hs]
    s = [qk[h] * jnp.exp(dmat[h] - m_t[h]) for h in hs]
    w_inter = [jnp.exp(inter[h] - m_t[h]) for h in hs]
    num = [jnp.dot(s[h], v[h], preferred_element_type=F32)
           + w_inter[h] * jnp.dot(q[h], ct_in[h], preferred_element_type=F32) for h in hs]
    den = [jnp.sum(s[h], axis=1, keepdims=True)
           + w_inter[h] * jnp.sum(q[h] * n_in[h], axis=1, keepdims=True) for h in hs]
    o_ref[0] = jnp.concatenate([num[h] / jnp.maximum(jnp.abs(den[h]), jnp.exp(-m_t[h])) for h in hs], axis=1)
    g = [th[h] - bc[h] + li_col[:, h:h + 1] for h in hs]
    m_new = [jnp.maximum(th[h] + m_prev[h], jnp.max(g[h], axis=0, keepdims=True)) for h in hs]
    wg = [jnp.exp(g[h] - m_new[h]) for h in hs]
    wc = [jnp.exp(th[h] + m_prev[h] - m_new[h]) for h in hs]
    ct_out = [wc[h] * ct_in[h] + lax.dot_general(k[h] * wg[h], v[h], tn_dims, preferred_element_type=F32)
              for h in hs]
    n_out = [wc[h] * n_in[h] + jnp.sum(wg[h] * k[h], axis=0, keepdims=True) for h in hs]
    for h in hs:
        ct_ref[h] = ct_out[h]
        n_ref[h] = n_out[h]
        m_ref[h] = m_new[h]


def _mlstm_scan(qkvo, gates, gates_row, bias_col, bias_row, nc_ctx, direction):
    bsz, n, _ = qkvo.shape
    nc = n // CHUNK
    reverse = direction == 1
    order = lambda c: _chunk_order(c, nc_ctx, nc, reverse)
    return pl.pallas_call(
        functools.partial(_mlstm_kernel, direction=direction),
        grid=(bsz, nc),
        in_specs=[pl.BlockSpec((1, CHUNK, 2 * A_HEADS * A_DQK + A_WIDTH), lambda b, c: (b, order(c), 0)),
                  pl.BlockSpec((1, CHUNK, LANES), lambda b, c: (b, order(c), 0)),
                  pl.BlockSpec((1, 32, CHUNK), lambda b, c: (b, 0, order(c))),
                  pl.BlockSpec((1, LANES), lambda b, c: (0, 0)),
                  pl.BlockSpec((32, 1), lambda b, c: (0, 0))],
        out_specs=pl.BlockSpec((1, CHUNK, A_WIDTH), lambda b, c: (b, order(c), 0)),
        out_shape=jax.ShapeDtypeStruct((bsz, n, A_WIDTH), F32),
        scratch_shapes=[pltpu.VMEM((A_HEADS, A_DQK, A_DV), F32),
                        pltpu.VMEM((A_HEADS, 1, A_DQK), F32),
                        pltpu.VMEM((A_HEADS, 1, 1), F32)],
        compiler_params=_cparams(("parallel", "arbitrary")),
        name="mlstm_scan",
    )(qkvo, gates, gates_row, bias_col, bias_row)


def _ssd_kernel(xbc_ref, gc_ref, gr_ref, bc_ref, br_ref, ac_ref, ar_ref, o_ref, st_ref, *,
                direction):
    c = pl.program_id(1)

    @pl.when(c == 0)
    def _():
        st_ref[...] = jnp.zeros_like(st_ref)

    reverse = direction == 1
    mask = _tri(reverse)
    gc = gc_ref[0] + bc_ref[...]
    gr = gr_ref[0] + br_ref[...]
    d8 = 16 + direction * B_HEADS
    dt_col = _softplus(gc[:, d8:d8 + B_HEADS])
    dt_row = _softplus(gr[d8:d8 + B_HEADS, :])
    la_col = -dt_col * jnp.exp(ac_ref[:, d8:d8 + B_HEADS])
    la_row = -dt_row * jnp.exp(ar_ref[d8:d8 + B_HEADS, :])
    b_col, b_row = _cumsums(la_col, la_row, mask)
    tot = jnp.sum(la_col, axis=0, keepdims=True)
    hs = range(B_HEADS)
    hpg = B_HEADS // B_GROUPS
    nt_dims, tn_dims = (((1,), (1,)), ((), ())), (((0,), (0,)), ((), ()))
    bm = [xbc_ref[0, :, B_WIDTH + g * B_STATE:B_WIDTH + (g + 1) * B_STATE] for g in range(B_GROUPS)]
    cm = [xbc_ref[0, :, B_WIDTH + (B_GROUPS + g) * B_STATE:B_WIDTH + (B_GROUPS + g + 1) * B_STATE]
          for g in range(B_GROUPS)]
    cb = [lax.dot_general(cm[g], bm[g], nt_dims, preferred_element_type=F32) for g in range(B_GROUPS)]
    xh = [xbc_ref[0, :, h * B_HEADDIM:(h + 1) * B_HEADDIM] for h in hs]
    st_in = [st_ref[h] for h in hs]
    bc = [b_col[:, h:h + 1] for h in hs]
    th = [tot[:, h:h + 1] for h in hs]
    decay = [jnp.exp(jnp.where(mask, bc[h] - b_row[h:h + 1, :], -jnp.inf)) for h in hs]
    s = [cb[h // hpg] * decay[h] * dt_row[h:h + 1, :] for h in hs]
    y = [jnp.dot(s[h], xh[h], preferred_element_type=F32)
         + jnp.exp(bc[h]) * jnp.dot(cm[h // hpg], st_in[h], preferred_element_type=F32) for h in hs]
    o_ref[0] = jnp.concatenate(y, axis=1)
    w_end = [jnp.exp(th[h] - bc[h]) * dt_col[:, h:h + 1] for h in hs]
    st_out = [jnp.exp(th[h]) * st_in[h]
              + lax.dot_general(bm[h // hpg] * w_end[h], xh[h], tn_dims, preferred_element_type=F32) for h in hs]
    for h in hs:
        st_ref[h] = st_out[h]


def _ssd_scan(xbc, gates, gates_row, bias_col, bias_row, alog_col, alog_row, nc_ctx, direction):
    bsz, n, width = xbc.shape
    nc = n // CHUNK
    reverse = direction == 1
    order = lambda c: _chunk_order(c, nc_ctx, nc, reverse)
    return pl.pallas_call(
        functools.partial(_ssd_kernel, direction=direction),
        grid=(bsz, nc),
        in_specs=[pl.BlockSpec((1, CHUNK, width), lambda b, c: (b, order(c), 0)),
                  pl.BlockSpec((1, CHUNK, LANES), lambda b, c: (b, order(c), 0)),
                  pl.BlockSpec((1, 32, CHUNK), lambda b, c: (b, 0, order(c))),
                  pl.BlockSpec((1, LANES), lambda b, c: (0, 0)),
                  pl.BlockSpec((32, 1), lambda b, c: (0, 0)),
                  pl.BlockSpec((1, LANES), lambda b, c: (0, 0)),
                  pl.BlockSpec((32, 1), lambda b, c: (0, 0))],
        out_specs=pl.BlockSpec((1, CHUNK, B_WIDTH), lambda b, c: (b, order(c), 0)),
        out_shape=jax.ShapeDtypeStruct((bsz, n, B_WIDTH), F32),
        scratch_shapes=[pltpu.VMEM((B_HEADS, B_STATE, B_HEADDIM), F32)],
        compiler_params=_cparams(("parallel", "arbitrary")),
        name="ssd_scan",
    )(xbc, gates, gates_row, bias_col, bias_row, alog_col, alog_row)


def _finish0_kernel(hf_ref, hb_ref, yf_ref, yb_ref, o_ref_in, z_ref, xs_ref, x_ref, mod_ref,
                    mn_ref, sn_ref, dsk_ref, w_ref, out_ref):
    h = hf_ref[0] + hb_ref[0]
    parts = []
    for hd in range(A_HEADS):
        hh = h[:, hd * A_DV:(hd + 1) * A_DV]
        ms = jnp.mean(hh * hh, axis=-1, keepdims=True)
        parts.append(hh * lax.rsqrt(ms + EPS))
    hn = jnp.concatenate(parts, axis=1) * mn_ref[...]
    ya = _sigmoid(o_ref_in[0]) * hn
    y = yf_ref[0] + yb_ref[0] + dsk_ref[...] * xs_ref[0]
    t = y * _silu(z_ref[0])
    ms = jnp.mean(t * t, axis=-1, keepdims=True)
    yb = t * lax.rsqrt(ms + EPS) * sn_ref[...]
    f = jnp.concatenate([ya, yb], axis=1)
    g1 = mod_ref[0, :, 2 * D_MODEL:3 * D_MODEL]
    out_ref[0] = x_ref[0] + g1 * jnp.dot(f.astype(BF16), w_ref[...], preferred_element_type=F32)


def _finish0(hf, hb, yf, yb, qkvo, zx, xbc_act, xs, modsel, mnorm, snorm, dskip, w_out, ctx_tiles):
    bsz, n, d = xs.shape
    nt = n // ROWS
    half = lambda j: pl.BlockSpec((1, ROWS, 512), lambda b, i, j=j: (b, i, j))
    vec = pl.BlockSpec((1, 512), lambda b, i: (0, 0))
    return pl.pallas_call(
        _finish0_kernel,
        grid=(bsz, nt),
        in_specs=[half(0), half(0), half(0), half(0), half(2), half(2), half(0),
                  pl.BlockSpec((1, ROWS, d), lambda b, i: (b, i, 0)),
                  _mod_spec(ctx_tiles), vec, vec, vec,
                  pl.BlockSpec((d, d), lambda b, i: (0, 0))],
        out_specs=pl.BlockSpec((1, ROWS, d), lambda b, i: (b, i, 0)),
        out_shape=jax.ShapeDtypeStruct((bsz, n, d), F32),
        compiler_params=_cparams(("parallel", "arbitrary")),
        name="finish_mlstm_ssd",
    )(hf, hb, yf, yb, qkvo, zx, xbc_act, xs, modsel, mnorm, snorm, dskip, w_out)


def _head_ones(n, hd):
    r = lax.broadcasted_iota(jnp.int32, (n, n), 0) // hd
    c = lax.broadcasted_iota(jnp.int32, (n, n), 1) // hd
    return (r == c).astype(F32)


def _rwkv_prep_kernel(pr_ref, p_ref, n_ref, mu_ref, w0_ref, wup_ref, a0_ref, aup_ref, gup_ref,
                      kk_ref, ka_ref, r_o, v_o, kk_o, g_o, w0_o, w1_o, kt0_o, kt1_o, al0_o, al1_o,
                      *, ctx_tiles, ntiles):
    i = pl.program_id(1)
    pr = pr_ref[0]
    rows, cols = pr.shape
    q = cols // 4
    row = lax.broadcasted_iota(jnp.int32, (rows, 1), 0)
    col = lax.broadcasted_iota(jnp.int32, (1, cols), 1)
    is_ctx = i < ctx_tiles
    l1 = pltpu.roll(pr, 1, 0)
    r1 = pltpu.roll(pr, rows - 1, 0)
    left = jnp.where(row % GRID_W == 0, 0.0, l1)
    right = jnp.where(row % GRID_W == GRID_W - 1, 0.0, r1)
    up = jnp.concatenate([p_ref[0], pr[:rows - GRID_W]], axis=0)
    up = jnp.where(jnp.logical_and(i == ctx_tiles, row < GRID_W), 0.0, up)
    down = jnp.concatenate([pr[GRID_W:], n_ref[0]], axis=0)
    down = jnp.where(jnp.logical_and(i == ntiles - 1, row >= rows - GRID_W), 0.0, down)
    grid_sh = jnp.where(col < q, left, jnp.where(col < 2 * q, right, jnp.where(col < 3 * q, up, down)))
    prev = jnp.where(row == 0, 0.0, l1)
    nxt = jnp.where(row == rows - 1, 0.0, r1)
    seq_sh = jnp.where(col < 2 * q, prev, nxt)
    shifted = jnp.where(is_ctx, seq_sh, grid_sh)
    pr = pr + mu_ref[...] * (shifted - pr)
    r = pr[:, 0:C_WIDTH]
    k = pr[:, C_WIDTH:2 * C_WIDTH]
    v = pr[:, 2 * C_WIDTH:3 * C_WIDTH]
    o = 3 * C_WIDTH
    wd = pr[:, o:o + C_LORA_W]
    ad = pr[:, o + C_LORA_W:o + C_LORA_W + C_LORA_A]
    gd = pr[:, o + C_LORA_W + C_LORA_A:]
    kk = k * kk_ref[...]
    ss = jnp.dot(kk * kk, _head_ones(C_WIDTH, C_HEADDIM), precision=HIGHEST, preferred_element_type=F32)
    kk = kk * lax.rsqrt(ss + 1e-12)
    tw = jnp.tanh(wd)
    for d, (w_o, kt_o, al_o) in enumerate(((w0_o, kt0_o, al0_o), (w1_o, kt1_o, al1_o))):
        logw = -RWKV_W_SCALE * _sigmoid(w0_ref[d:d + 1, :] + jnp.dot(tw, wup_ref[d], preferred_element_type=F32))
        a = _sigmoid(a0_ref[d:d + 1, :] + jnp.dot(ad, aup_ref[d], preferred_element_type=F32))
        w_o[0] = jnp.exp(logw)
        kt_o[0] = k * (1.0 + (a - 1.0) * ka_ref[...])
        al_o[0] = a
    r_o[0] = r
    v_o[0] = v
    kk_o[0] = kk
    g_o[0] = jnp.dot(_sigmoid(gd), gup_ref[...], preferred_element_type=F32)


def _rwkv_prep(pr, mu, w0, w_up, a0, a_up, g_up, k_k, k_a, ctx_tiles):
    bsz, n, cols = pr.shape
    nt = n // ROWS
    hb = ROWS // GRID_W
    nh = n // GRID_W
    full = lambda a: pl.BlockSpec(a.shape, lambda b, i, nd=a.ndim: (0,) * nd)
    out = pl.BlockSpec((1, ROWS, C_WIDTH), lambda b, i: (b, i, 0))
    return pl.pallas_call(
        functools.partial(_rwkv_prep_kernel, ctx_tiles=ctx_tiles, ntiles=nt),
        grid=(bsz, nt),
        in_specs=[pl.BlockSpec((1, ROWS, cols), lambda b, i: (b, i, 0)),
                  pl.BlockSpec((1, GRID_W, cols), lambda b, i: (b, jnp.maximum(i * hb - 1, 0), 0)),
                  pl.BlockSpec((1, GRID_W, cols), lambda b, i: (b, jnp.minimum((i + 1) * hb, nh - 1), 0)),
                  full(mu), full(w0), full(w_up), full(a0), full(a_up), full(g_up), full(k_k), full(k_a)],
        out_specs=[out] * 10,
        out_shape=[jax.ShapeDtypeStruct((bsz, n, C_WIDTH), F32)] * 10,
        compiler_params=_cparams(("parallel", "arbitrary")),
        name="rwkv_prep",
    )(pr, pr, pr, mu, w0, w_up, a0, a_up, g_up, k_k, k_a)


RWKV_UNROLL = 8


def _rwkv_kernel(rf, wf, kf, vf, kkf, alf, rb, wb, kb, vb, kkb, alb, of_ref, ob_ref, s_ref, y_ref):
    c = pl.program_id(0)

    @pl.when(c == 0)
    def _():
        s_ref[...] = jnp.zeros_like(s_ref)

    nb = rf.shape[0]
    t_len = rf.shape[1]
    nt = C_WIDTH // LANES
    sub = lax.broadcasted_iota(jnp.int32, (C_HEADDIM, C_WIDTH), 0)
    lane = lax.broadcasted_iota(jnp.int32, (C_HEADDIM, C_WIDTH), 1)
    diag = (lane % C_HEADDIM == sub).astype(F32)
    ones = _head_ones(LANES, C_HEADDIM).astype(BF16)
    chains = [(refs, b) for refs in ((rf, wf, kf, vf, kkf, alf, of_ref, False),
                                     (rb, wb, kb, vb, kkb, alb, ob_ref, True)) for b in range(nb)]

    def head_sum(xs):
        x = jnp.concatenate([a[:, j * LANES:(j + 1) * LANES] for a in xs for j in range(nt)], axis=0)
        out = jnp.dot(x.astype(BF16), ones, preferred_element_type=F32)
        res = []
        for n in range(len(xs)):
            res.append(jnp.concatenate(
                [out[(nt * n + j) * C_HEADDIM:(nt * n + j + 1) * C_HEADDIM] for j in range(nt)], axis=1))
        return res

    def step(i, states):
        rows = []
        for refs, b in chains:
            t = (t_len - 1 - i) if refs[7] else i
            rows.append([ref[b, pl.ds(t, 1), :] for ref in refs[:6]] + [t])
        sa = head_sum([s * (-row[4]) for s, row in zip(states, rows)])
        vcol = head_sum([diag * row[3] for row in rows])
        new = [s * row[1] + a * (row[4] * row[5]) + vc * row[2]
               for s, row, a, vc in zip(states, rows, sa, vcol)]
        ys = head_sum([s * row[0] for s, row in zip(new, rows)])
        for n, (row, y) in enumerate(zip(rows, ys)):
            pltpu.store(y_ref.at[n], y, mask=lane % C_HEADDIM == row[6] % C_HEADDIM)
        return new

    nblk = t_len // C_HEADDIM

    def flush(walk):
        for n, (refs, b) in enumerate(chains):
            blk = (nblk - 1 - walk) if refs[7] else walk
            parts = []
            for j in range(nt):
                tr = y_ref[n, :, j * LANES:(j + 1) * LANES].T
                parts += [tr[:C_HEADDIM], tr[C_HEADDIM:]]
            refs[6][b, blk * C_HEADDIM:(blk + 1) * C_HEADDIM, :] = jnp.concatenate(parts, axis=1)

    def body(g, states):
        states = list(states)
        for u in range(RWKV_UNROLL):
            states = step(g * RWKV_UNROLL + u, states)
        return tuple(states)

    per_blk = C_HEADDIM // RWKV_UNROLL
    states = tuple(s_ref[n] for n in range(len(chains)))
    for walk in range(nblk):
        states = lax.fori_loop(walk * per_blk, (walk + 1) * per_blk, body, states)
        flush(walk)
    for n in range(len(chains)):
        s_ref[n] = states[n]


def _rwkv_scan(r, v, kk, w_f, kt_f, al_f, w_b, kt_b, al_b, nc_ctx):
    bsz, n, width = r.shape
    nc = n // CHUNK
    fwd = pl.BlockSpec((bsz, CHUNK, width), lambda c: (0, c, 0))
    bwd = pl.BlockSpec((bsz, CHUNK, width), lambda c: (0, _chunk_order(c, nc_ctx, nc, True), 0))
    return pl.pallas_call(
        _rwkv_kernel,
        grid=(nc,),
        in_specs=[fwd] * 6 + [bwd] * 6,
        out_specs=[fwd, bwd],
        out_shape=[jax.ShapeDtypeStruct((bsz, n, width), F32)] * 2,
        scratch_shapes=[pltpu.VMEM((2 * bsz, C_HEADDIM, width), F32),
                        pltpu.VMEM((2 * bsz, C_HEADDIM, width), F32)],
        compiler_params=_cparams(("arbitrary",)),
        name="rwkv_scan",
    )(r, w_f, kt_f, v, kk, al_f, r, w_b, kt_b, v, kk, al_b)


def _lru_gate_kernel(xc_ref, w_ref, b_ref, lam_ref, a0_o, b0_o, a1_o, b1_o):
    xc = xc_ref[0]
    z = jnp.dot(xc, w_ref[...], preferred_element_type=F32) + b_ref[...]
    for d, (a_o, b_o) in enumerate(((a0_o, b0_o), (a1_o, b1_o))):
        gr = _sigmoid(z[:, 2 * d * D_WIDTH:(2 * d + 1) * D_WIDTH])
        gi = _sigmoid(z[:, (2 * d + 1) * D_WIDTH:(2 * d + 2) * D_WIDTH])
        log_a = -LRU_C * gr * _softplus(-lam_ref[d:d + 1, :])
        th = jnp.tanh(log_a)
        one_minus_a2 = -2.0 * th / (1.0 - th)
        a_o[0] = jnp.exp(log_a)
        b_o[0] = jnp.sqrt(one_minus_a2) * (gi * xc)


def _lru_gates(xc, w, b, lam):
    bsz, n, width = xc.shape
    nt = n // ROWS
    out = pl.BlockSpec((1, ROWS, width), lambda b_, i: (b_, i, 0))
    return pl.pallas_call(
        _lru_gate_kernel,
        grid=(bsz, nt),
        in_specs=[out,
                  pl.BlockSpec(w.shape, lambda b_, i: (0, 0)),
                  pl.BlockSpec(b.shape, lambda b_, i: (0, 0)),
                  pl.BlockSpec(lam.shape, lambda b_, i: (0, 0))],
        out_specs=[out] * 4,
        out_shape=[jax.ShapeDtypeStruct((bsz, n, width), F32)] * 4,
        compiler_params=_cparams(("parallel", "arbitrary")),
        name="lru_gates",
    )(xc, w, b, lam)


def _lru_kernel(a_ref, b_ref, o_ref, h_ref, *, reverse):
    c = pl.program_id(0)

    @pl.when(c == 0)
    def _():
        h_ref[...] = jnp.zeros_like(h_ref)

    nb = a_ref.shape[0]
    ntile = a_ref.shape[1] // SUBLANES
    row = lax.broadcasted_iota(jnp.int32, (SUBLANES, a_ref.shape[2]), 0)

    def tile_scan(a, b):
        for d in (1, 2, 4):
            if reverse:
                keep = row < SUBLANES - d
                a_sh = jnp.where(keep, pltpu.roll(a, SUBLANES - d, 0), 1.0)
                b_sh = jnp.where(keep, pltpu.roll(b, SUBLANES - d, 0), 0.0)
            else:
                keep = row >= d
                a_sh = jnp.where(keep, pltpu.roll(a, d, 0), 1.0)
                b_sh = jnp.where(keep, pltpu.roll(b, d, 0), 0.0)
            b = b + a * b_sh
            a = a * a_sh
        return a, b

    def step(i, hs):
        t = (ntile - 1 - i) if reverse else i
        r0 = pl.multiple_of(t * SUBLANES, SUBLANES)
        new = []
        for bb in range(nb):
            a, b = tile_scan(a_ref[bb, pl.ds(r0, SUBLANES), :], b_ref[bb, pl.ds(r0, SUBLANES), :])
            h = a * hs[bb] + b
            o_ref[bb, pl.ds(r0, SUBLANES), :] = h
            new.append(h[0:1] if reverse else h[SUBLANES - 1:SUBLANES])
        return tuple(new)

    hs = lax.fori_loop(0, ntile, step, tuple(h_ref[bb] for bb in range(nb)))
    for bb in range(nb):
        h_ref[bb] = hs[bb]


def _lru_scan(a, b, nc_ctx, reverse):
    bsz, n, width = a.shape
    nc = n // CHUNK
    order = lambda c: _chunk_order(c, nc_ctx, nc, reverse)
    spec = pl.BlockSpec((bsz, CHUNK, width), lambda c: (0, order(c), 0))
    return pl.pallas_call(
        functools.partial(_lru_kernel, reverse=reverse),
        grid=(nc,),
        in_specs=[spec, spec],
        out_specs=spec,
        out_shape=jax.ShapeDtypeStruct((bsz, n, width), F32),
        scratch_shapes=[pltpu.VMEM((bsz, 1, width), F32)],
        compiler_params=_cparams(("arbitrary",)),
        name="lru_scan",
    )(a, b)


def _finish1_kernel(yf_ref, yb_ref, uf_ref, ub_ref, r_ref, kt0_ref, kt1_ref, v_ref, g_ref, gb_ref,
                    x_ref, mod_ref, rk_ref, lw_ref, lb_ref, w_ref, out_ref):
    y = yf_ref[0] + yb_ref[0]
    hs = _head_ones(C_WIDTH, C_HEADDIM)
    mean = jnp.dot(y, hs, precision=HIGHEST, preferred_element_type=F32) * (1.0 / C_HEADDIM)
    yc = y - mean
    var = jnp.dot(yc * yc, hs, precision=HIGHEST, preferred_element_type=F32) * (1.0 / C_HEADDIM)
    yn = yc * lax.rsqrt(var + RWKV_GN_EPS) * lw_ref[...] + lb_ref[...]
    kb = 0.5 * (kt0_ref[0] + kt1_ref[0])
    bonus = jnp.dot(r_ref[0] * kb * rk_ref[...], hs, precision=HIGHEST, preferred_element_type=F32)
    yn = yn + bonus * v_ref[0]
    yc_ = yn * g_ref[0]
    yd = (uf_ref[0] + ub_ref[0]) * _gelu(gb_ref[0])
    f = jnp.concatenate([yc_, yd], axis=1)
    g1 = mod_ref[0, :, 2 * D_MODEL:3 * D_MODEL]
    out_ref[0] = x_ref[0] + g1 * jnp.dot(f.astype(BF16), w_ref[...], preferred_element_type=F32)


def _finish1(yf, yb, uf, ub, r, kt0, kt1, v, g, gate_br, xs, modsel, r_k, ln_w, ln_b, w_out, ctx_tiles):
    bsz, n, d = xs.shape
    nt = n // ROWS
    half = pl.BlockSpec((1, ROWS, 512), lambda b, i: (b, i, 0))
    vec = pl.BlockSpec((1, 512), lambda b, i: (0, 0))
    return pl.pallas_call(
        _finish1_kernel,
        grid=(bsz, nt),
        in_specs=[half] * 10 + [pl.BlockSpec((1, ROWS, d), lambda b, i: (b, i, 0)),
                                _mod_spec(ctx_tiles), vec, vec, vec,
                                pl.BlockSpec((d, d), lambda b, i: (0, 0))],
        out_specs=pl.BlockSpec((1, ROWS, d), lambda b, i: (b, i, 0)),
        out_shape=jax.ShapeDtypeStruct((bsz, n, d), F32),
        compiler_params=_cparams(("parallel", "arbitrary")),
        name="finish_rwkv_lru",
    )(yf, yb, uf, ub, r, kt0, kt1, v, g, gate_br, xs, modsel, r_k, ln_w, ln_b, w_out)


def _max_arg(s):
    tiles = s.shape[0] // SUBLANES
    vals = [s[i * SUBLANES:(i + 1) * SUBLANES] for i in range(tiles)]
    row = lax.broadcasted_iota(jnp.int32, (SUBLANES, s.shape[1]), 0)
    ids = [row + i * SUBLANES for i in range(tiles)]
    while len(vals) > 1:
        nv, ni = [], []
        for a in range(0, len(vals) - 1, 2):
            take = vals[a + 1] > vals[a]
            nv.append(jnp.where(take, vals[a + 1], vals[a]))
            ni.append(jnp.where(take, ids[a + 1], ids[a]))
        if len(vals) % 2:
            nv.append(vals[-1])
            ni.append(ids[-1])
        vals, ids = nv, ni
    mx = jnp.max(vals[0], axis=0, keepdims=True)
    am = jnp.min(jnp.where(vals[0] == mx, ids[0], s.shape[0]), axis=0, keepdims=True)
    return mx, am


def _topk_rows(s, payload=None):
    iota = lax.broadcasted_iota(jnp.int32, s.shape, 0)
    vals, idxs = [], []
    for _ in range(PEER_TOPK):
        mx, am = _max_arg(s)
        sel = iota == am
        vals.append(mx)
        if payload is None:
            idxs.append(am)
        else:
            idxs.append(jnp.max(jnp.where(sel, payload, -1), axis=0, keepdims=True))
        s = jnp.where(sel, -jnp.inf, s)
    return vals, idxs


def _peer_candidates(sv0, si0, sv1, si1):
    grp = SUBLANES
    v1_all, i1_all = jnp.concatenate(sv1, axis=0), jnp.concatenate(si1, axis=0)
    v1_lo, i1_lo = v1_all[:grp], i1_all[:grp]
    row = lax.broadcasted_iota(jnp.int32, v1_lo.shape, 0)
    cand, cidx = [sv0[0] + v1_all], [si0[0] * N_KEYS + i1_all]
    for a in range(1, grp):
        cand.append(jnp.where(row < PEER_TOPK // (a + 1), sv0[a] + v1_lo, -jnp.inf))
        cidx.append(si0[a] * N_KEYS + i1_lo)
    cand.append(jnp.concatenate(sv0[grp:], axis=0) + sv1[0])
    cidx.append(jnp.concatenate(si0[grp:], axis=0) * N_KEYS + si1[0])
    return jnp.concatenate(cand, axis=0), jnp.concatenate(cidx, axis=0)


def _select_head(hn, wq_ref, keys_ref, k):
    half = PEER_DK // 2
    q = jnp.dot(hn, wq_ref[k], preferred_element_type=F32)
    sv, si = [], []
    for p in range(2):
        s = lax.dot_general(keys_ref[2 * k + p], q[:, p * half:(p + 1) * half],
                            (((1,), (1,)), ((), ())), preferred_element_type=F32)
        v, ix = _topk_rows(s)
        sv.append(v)
        si.append(ix)
    cand, cidx = _peer_candidates(sv[0], si[0], sv[1], si[1])
    best, eid = _topk_rows(cand, cidx)
    e = jnp.exp(jnp.concatenate(best, axis=0) - best[0])
    return jnp.concatenate(eid, axis=0), e / jnp.sum(e, axis=0, keepdims=True)


def _peer_kernel(x_ref, xn_ref, nw_ref, mod_ref, modn_ref, wq_ref, keys_ref, uv_hbm, uvw_hbm, o_ref,
                 hbuf, hnext, eid_v, eid_t, gate_s, *scratch, nblk):
    bufs = scratch[:PEER_NBUF]
    idx_smem, gsem, isem = scratch[PEER_NBUF:]
    i = pl.program_id(0)
    tb = x_ref.shape[0]
    ahead = PEER_NBUF - 1
    cur = i % 2
    nxt = 1 - cur
    ngroups = tb // PEER_NBUF
    gpp = ngroups // PEER_HEADS
    part = PEER_SLOTS // (2 * ROW_TILES)

    def select_piece(hn, k, slot):
        eid, gate = _select_head(hn, wq_ref, keys_ref, k)
        eid_v[pl.ds(pl.multiple_of(k * PEER_TOPK, PEER_TOPK), PEER_TOPK), :] = eid
        gate_s[slot, pl.ds(pl.multiple_of(k * PEER_TOPK, PEER_TOPK), PEER_TOPK), :] = gate

    def ids_to_smem(slot):
        eid_t[...] = eid_v[...].T
        return pltpu.make_async_copy(eid_t, idx_smem.at[slot], isem.at[slot])

    def issue(islot, t, n, lo, hi):
        for j in range(lo, hi):
            e = idx_smem[islot, t, j]
            pltpu.make_async_copy(uv_hbm.at[e], bufs[n].at[:, pl.ds(j * ROW_TILES, ROW_TILES), :],
                                  gsem.at[n]).start(priority=j % 2)

    hbuf[...] = _norm_mod(x_ref[...], nw_ref[...], mod_ref, 3)

    @pl.when(i == 0)
    def _():
        h0 = hbuf[...].astype(BF16)

        def piece0(k, carry):
            select_piece(h0, k, 0)
            return carry

        lax.fori_loop(0, PEER_HEADS, piece0, 0)
        cp = ids_to_smem(0)
        cp.start()
        cp.wait()
        for t0 in range(ahead):
            issue(0, t0, t0, 0, PEER_SLOTS)

    hnext[...] = _norm_mod(xn_ref[...], nw_ref[...], modn_ref, 3).astype(BF16)
    g2 = mod_ref[0, :, 5 * D_MODEL:6 * D_MODEL]
    lane = lax.broadcasted_iota(jnp.int32, (PEER_SLOTS, tb), 1)

    def consume(t, n, issue_part):
        pltpu.make_async_copy(uvw_hbm.at[pl.ds(0, 2)], bufs[n], gsem.at[n]).wait()
        xrow = hbuf[pl.ds(t, 1), :]
        acc = jnp.zeros((PEER_SLOTS, LANES), F32)
        for s in range(ROW_TILES):
            issue_part(s)
            us = bufs[n][0, pl.ds(s, PEER_SLOTS, stride=ROW_TILES), :]
            acc = acc + us * xrow[:, s * LANES:(s + 1) * LANES]
        act = jnp.sum(acc, axis=1, keepdims=True)
        gcol = jnp.sum(jnp.where(lane == t, gate_s[cur], 0.0), axis=1, keepdims=True)
        coef = jnp.broadcast_to(gcol * _gelu(act), (PEER_SLOTS, LANES))
        outs = []
        for s in range(ROW_TILES):
            issue_part(ROW_TILES + s)
            vs = bufs[n][1, pl.ds(s, PEER_SLOTS, stride=ROW_TILES), :]
            outs.append(jnp.sum(vs * coef, axis=0, keepdims=True))
        orow = jnp.concatenate(outs, axis=1)
        o_ref[pl.ds(t, 1), :] = x_ref[pl.ds(t, 1), :] + g2 * orow

    def group(g):
        for n in range(PEER_NBUF):
            t = g * PEER_NBUF + n
            consume(t, n, lambda k, t=t, n=n: issue(cur, t + ahead, (n + ahead) % PEER_NBUF,
                                                    k * part, (k + 1) * part))

    more = i + 1 < nblk

    def piece_and_groups(k, carry):
        @pl.when(more)
        def _():
            select_piece(hnext[...], k, nxt)

        for gg in range(gpp):
            group(k * gpp + gg)
        return carry

    lax.fori_loop(0, PEER_HEADS - 1, piece_and_groups, 0)

    @pl.when(more)
    def _():
        select_piece(hnext[...], PEER_HEADS - 1, nxt)
        ids_to_smem(nxt).start()

    for gg in range(gpp - 1):
        group((PEER_HEADS - 1) * gpp + gg)

    @pl.when(more)
    def _():
        pltpu.make_async_copy(eid_t, idx_smem.at[nxt], isem.at[nxt]).wait()

    for n in range(PEER_NBUF):
        t = (ngroups - 1) * PEER_NBUF + n
        if n == 0:
            consume(t, n, lambda k, t=t: issue(cur, t + ahead, ahead, k * part, (k + 1) * part))
        else:
            @pl.when(more)
            def _():
                issue(nxt, n - 1, n - 1, 0, PEER_SLOTS)
            consume(t, n, lambda k: None)


def _peer(xs, nw, modsel, wq, keys, u_tab, v_tab, ctx_tiles):
    bsz, n, d = xs.shape
    ntok = bsz * n
    nblk = ntok // PEER_TB
    tiles_per_batch = n // PEER_TB
    ctx_blocks = ctx_tiles * ROWS // PEER_TB
    slab = PEER_SLOTS * ROW_TILES
    assert PEER_TB % (PEER_NBUF * PEER_HEADS) == 0
    ne = u_tab.shape[0]
    uv = jnp.stack([u_tab.reshape(ne, ROW_TILES, LANES), v_tab.reshape(ne, ROW_TILES, LANES)], axis=1)
    uvw = uv.reshape(-1, slab, LANES)
    wq3 = wq.reshape(d, PEER_HEADS, PEER_DK).transpose(1, 0, 2)
    nxt_blk = lambda i: jnp.minimum(i + 1, nblk - 1)
    mod_of = lambda i: (2 * (i // tiles_per_batch) + jnp.where(i % tiles_per_batch >= ctx_blocks, 1, 0), 0, 0)
    out = pl.pallas_call(
        functools.partial(_peer_kernel, nblk=nblk),
        grid=(nblk,),
        in_specs=[pl.BlockSpec((PEER_TB, d), lambda i: (i, 0)),
                  pl.BlockSpec((PEER_TB, d), lambda i: (nxt_blk(i), 0)),
                  pl.BlockSpec((1, d), lambda i: (0, 0)),
                  pl.BlockSpec((1, 1, 6 * D_MODEL), mod_of),
                  pl.BlockSpec((1, 1, 6 * D_MODEL), lambda i: mod_of(nxt_blk(i))),
                  pl.BlockSpec(wq3.shape, lambda i: (0, 0, 0)),
                  pl.BlockSpec(keys.shape, lambda i: (0, 0, 0)),
                  pl.BlockSpec(memory_space=pl.ANY),
                  pl.BlockSpec(memory_space=pl.ANY)],
        out_specs=pl.BlockSpec((PEER_TB, d), lambda i: (i, 0)),
        out_shape=jax.ShapeDtypeStruct((ntok, d), F32),
        scratch_shapes=[pltpu.VMEM((PEER_TB, d), F32),
                        pltpu.VMEM((PEER_TB, d), BF16),
                        pltpu.VMEM((PEER_SLOTS, PEER_TB), jnp.int32),
                        pltpu.VMEM((PEER_TB, PEER_SLOTS), jnp.int32),
                        pltpu.VMEM((2, PEER_SLOTS, PEER_TB), F32),
                        *[pltpu.VMEM((2, slab, LANES), F32) for _ in range(PEER_NBUF)],
                        pltpu.SMEM((2, PEER_TB, PEER_SLOTS), jnp.int32),
                        pltpu.SemaphoreType.DMA((PEER_NBUF,)),
                        pltpu.SemaphoreType.DMA((2,))],
        compiler_params=_cparams(("arbitrary",)),
        name="peer",
    )(xs.reshape(ntok, d), xs.reshape(ntok, d), nw, modsel, modsel, wq3, keys, uv, uvw)
    return out.reshape(bsz, n, d)


def _final_kernel(x_ref, w_ref, o_ref):
    x = x_ref[0]
    ms = jnp.mean(x * x, axis=-1, keepdims=True)
    o_ref[0] = x * lax.rsqrt(ms + EPS) * w_ref[...]


def _final_norm(xs, w, ctx_tiles, seq):
    bsz, n, d = xs.shape
    return pl.pallas_call(
        _final_kernel,
        grid=(bsz, seq // ROWS),
        in_specs=[pl.BlockSpec((1, ROWS, d), lambda b, i: (b, i + ctx_tiles, 0)),
                  pl.BlockSpec((1, d), lambda b, i: (0, 0))],
        out_specs=pl.BlockSpec((1, ROWS, d), lambda b, i: (b, i, 0)),
        out_shape=jax.ShapeDtypeStruct((bsz, seq, d), F32),
        compiler_params=_cparams(("parallel", "arbitrary")),
        name="final_norm",
    )(xs, w)


def _block_diag(w):
    nb, d, e = w.shape
    eye = jnp.eye(nb, dtype=w.dtype)
    return (eye[:, None, :, None] * w[:, :, None, :]).reshape(nb * d, nb * e)


def _mixer0(xs, modsel, norm1, w_in, w_out, i_bias, f_bias, mlstm_norm, conv_w, conv_b, dt_bias,
            a_log, d_skip, ssd_norm, ctx_tiles, nc_ctx):
    q0, k0, v0, o0, ig0, fg0, z0, xbc0, dt0, end = 0, 256, 512, 1024, 1536, 1544, 1552, 2064, 3088, 3104
    pad = jnp.zeros((D_MODEL, LANES - 32), w_in.dtype)
    w_cat = jnp.concatenate([w_in[:, q0:ig0], w_in[:, xbc0:dt0], w_in[:, z0:xbc0],
                             w_in[:, ig0:z0], w_in[:, dt0:end], pad], axis=1).astype(BF16)
    qkvo, zx, gates = _project(xs, norm1, modsel, w_cat, (1536, 1536, LANES), ctx_tiles)
    gates_row = jnp.swapaxes(gates[:, :, :32], 1, 2)
    bias = jnp.concatenate([i_bias.reshape(-1), f_bias.reshape(-1), dt_bias.reshape(-1)])
    bias_col = jnp.pad(bias, (0, LANES - 32)).reshape(1, LANES)
    bias_row = bias.reshape(32, 1)
    alog = jnp.concatenate([jnp.zeros((16,), F32), a_log.reshape(-1)])
    alog_col = jnp.pad(alog, (0, LANES - 32)).reshape(1, LANES)
    alog_row = alog.reshape(32, 1)
    xbc_act = _conv4(zx, 1024, conv_w, conv_b.reshape(1, -1), ctx_tiles, True)
    hf = _mlstm_scan(qkvo, gates, gates_row, bias_col, bias_row, nc_ctx, 0)
    hb = _mlstm_scan(qkvo, gates, gates_row, bias_col, bias_row, nc_ctx, 1)
    yf = _ssd_scan(xbc_act, gates, gates_row, bias_col, bias_row, alog_col, alog_row, nc_ctx, 0)
    yb = _ssd_scan(xbc_act, gates, gates_row, bias_col, bias_row, alog_col, alog_row, nc_ctx, 1)
    dskip = jnp.repeat(d_skip, B_HEADDIM).reshape(1, -1)
    return _finish0(hf, hb, yf, yb, qkvo, zx, xbc_act, xs, modsel, mlstm_norm.reshape(1, -1),
                    ssd_norm.reshape(1, -1), dskip, w_out.astype(BF16), ctx_tiles)


def _mixer1(xs, modsel, norm1, w_in, w_out, mu, w0, w_up, a0, a_up, g_up, k_k, k_a, r_k, ln_w, ln_b,
            conv_w, conv_b, lam, wa, ba, wi, bi, ctx_tiles, nc_ctx):
    pr, gate_br, x_br = _project(xs, norm1, modsel, w_in.astype(BF16), (RWKV_COLS, D_WIDTH, D_WIDTH),
                                 ctx_tiles)
    r, v, kk, g, w_f, w_b, kt_f, kt_b, al_f, al_b = _rwkv_prep(
        pr, mu.reshape(1, -1), w0, w_up, a0, a_up, g_up, k_k.reshape(1, -1), k_a.reshape(1, -1), ctx_tiles)
    yf, yb = _rwkv_scan(r, v, kk, w_f, kt_f, al_f, w_b, kt_b, al_b, nc_ctx)
    xc = _conv4(x_br, D_WIDTH, conv_w, conv_b.reshape(1, -1), ctx_tiles, False)
    w_gate = jnp.concatenate([_block_diag(wa[0]), _block_diag(wi[0]),
                              _block_diag(wa[1]), _block_diag(wi[1])], axis=1)
    b_gate = jnp.concatenate([ba[0], bi[0], ba[1], bi[1]]).reshape(1, -1)
    a_f, b_f, a_b, b_b = _lru_gates(xc, w_gate, b_gate, lam)
    uf = _lru_scan(a_f, b_f, nc_ctx, False)
    ub = _lru_scan(a_b, b_b, nc_ctx, True)
    return _finish1(yf, yb, uf, ub, r, kt_f, kt_b, v, g, gate_br, xs, modsel, r_k.reshape(1, -1),
                    ln_w.reshape(1, -1), ln_b.reshape(1, -1), w_out.astype(BF16), ctx_tiles)


def kernel(x, c, ctx, c_ctx, mod_w, mod_b, norm1, norm2, peer_wq, peer_keys, peer_u, peer_v, ev_w_in, ev_w_out, ev_mlstm_i_bias, ev_mlstm_f_bias, ev_mlstm_norm, ev_ssd_conv_w, ev_ssd_conv_b, ev_ssd_dt_bias, ev_ssd_a_log, ev_ssd_d, ev_ssd_norm, od_w_in, od_w_out, od_rwkv_mu, od_rwkv_w0, od_rwkv_w_up, od_rwkv_a0, od_rwkv_a_up, od_rwkv_g_up, od_rwkv_k_k, od_rwkv_k_a, od_rwkv_r_k, od_rwkv_ln_w, od_rwkv_ln_b, od_lru_conv_w, od_lru_conv_b, od_lru_lambda, od_lru_wa, od_lru_ba, od_lru_wi, od_lru_bi, final_norm):
    bsz, seq, d = x.shape
    ctx_len = ctx.shape[1]
    depth = mod_w.shape[0]
    assert d == D_MODEL and ctx_len == ROWS and seq % ROWS == 0 and bsz < SUBLANES
    ctx_tiles = ctx_len // ROWS
    nc_ctx = ctx_len // CHUNK
    xs = jnp.concatenate([ctx, x], axis=1)
    srows = jnp.concatenate([c, c_ctx[None, :], jnp.zeros((SUBLANES - bsz - 1, d), F32)], axis=0)
    for i in range(depth):
        mod = _modulation(srows, mod_w[i], mod_b[i].reshape(1, -1))
        modsel = jnp.stack([jnp.broadcast_to(mod[bsz], (bsz, 6 * d)), mod[:bsz]], axis=1)
        modsel = modsel.reshape(2 * bsz, 1, 6 * d)
        j = i // 2
        n1 = norm1[i].reshape(1, -1)
        if i % 2 == 0:
            xs = _mixer0(xs, modsel, n1, ev_w_in[j], ev_w_out[j], ev_mlstm_i_bias[j], ev_mlstm_f_bias[j],
                         ev_mlstm_norm[j], ev_ssd_conv_w[j], ev_ssd_conv_b[j], ev_ssd_dt_bias[j],
                         ev_ssd_a_log[j], ev_ssd_d[j], ev_ssd_norm[j], ctx_tiles, nc_ctx)
        else:
            xs = _mixer1(xs, modsel, n1, od_w_in[j], od_w_out[j], od_rwkv_mu[j], od_rwkv_w0[j],
                         od_rwkv_w_up[j], od_rwkv_a0[j], od_rwkv_a_up[j], od_rwkv_g_up[j], od_rwkv_k_k[j],
                         od_rwkv_k_a[j], od_rwkv_r_k[j].reshape(-1), od_rwkv_ln_w[j], od_rwkv_ln_b[j],
                         od_lru_conv_w[j], od_lru_conv_b[j], od_lru_lambda[j], od_lru_wa[j], od_lru_ba[j],
                         od_lru_wi[j], od_lru_bi[j], ctx_tiles, nc_ctx)
        keys = peer_keys[i].reshape(2 * PEER_HEADS, N_KEYS, PEER_DK // 2)
        xs = _peer(xs, norm2[i].reshape(1, -1), modsel, peer_wq[i].astype(BF16), keys,
                   peer_u[i], peer_v[i], ctx_tiles)
    return _final_norm(xs, final_norm.reshape(1, -1), ctx_tiles, seq)
```
